```python
import math
import jax
import jax.numpy as jnp
from jax import lax
import numpy as np

D_MODEL = 1024
BATCH = 16
SEQ = 2048
DEPTH = 2

GRID_W = 64
CTX_LEN = 256
N_MIXERS = 2
NORM_EPS = 1e-6
S5_GROUP = 16
S5_GROUPS = D_MODEL // S5_GROUP
S5_STATE = 64
S5_CHUNK = 128
S5_DT_MIN = 1e-3
S5_DT_MAX = 1e-1
DA_HEAD_DIM = 64
DA_VDIM = 2 * DA_HEAD_DIM
DA_HEADS = D_MODEL // DA_VDIM
Q_BLOCK = 128
ROPE_BASE = 10000.0
N_EXPERTS = 32
TOP_K = 4
D_FF = D_MODEL
SWIGLU_LIMIT = 7.0
SWIGLU_ALPHA = 1.702
MOE_BLOCK = 128

kernel_name = 'hybrid_s5_diffattn_moe_dit'


def rmsnorm(x, g):
    xf = x.astype(jnp.float32)
    y = xf * lax.rsqrt(jnp.mean(xf * xf, axis=-1, keepdims=True) + NORM_EPS)
    return (y * g.astype(jnp.float32)).astype(x.dtype)


def axial_rope(n_tokens, dim):
    rows = n_tokens // GRID_W
    row = jnp.repeat(jnp.arange(rows, dtype=jnp.float32), GRID_W)
    col = jnp.tile(jnp.arange(GRID_W, dtype=jnp.float32), rows)
    n_freq = dim // 4
    inv_freq = jnp.exp(-math.log(ROPE_BASE) * jnp.arange(n_freq, dtype=jnp.float32) / n_freq)
    ang = jnp.concatenate([row[:, None] * inv_freq, col[:, None] * inv_freq], axis=-1)
    return jnp.cos(ang), jnp.sin(ang)


def apply_rope(x, cos, sin):
    half = x.shape[-1] // 2
    x1, x2 = x[..., :half], x[..., half:]
    out = jnp.concatenate([x1 * cos - x2 * sin, x2 * cos + x1 * sin], axis=-1)
    return out.astype(x.dtype)


def s5_discretise(a_re, a_im, log_dt, b_re, b_im):
    a = lax.complex(a_re.astype(jnp.float32), a_im.astype(jnp.float32))
    dt = jnp.exp(log_dt.astype(jnp.float32))[:, None]
    log_lam = a * dt
    b = lax.complex(b_re.astype(jnp.float32), b_im.astype(jnp.float32))
    b_bar = ((jnp.exp(log_lam) - 1.0) / a)[..., None] * b
    return log_lam, b_bar


def _linear_combine(left, right):
    a1, b1 = left
    a2, b2 = right
    return a1 * a2, a2 * b1 + b2


def s5_scan(u, log_lam, b_bar, c_mat, h0):
    bsz, n = u.shape[0], u.shape[1]
    n_chunks = n // S5_CHUNK
    u_chunks = jnp.moveaxis(u.reshape((bsz, n_chunks, S5_CHUNK) + u.shape[2:]), 1, 0)
    lam = jnp.exp(log_lam)
    steps = jnp.arange(1, S5_CHUNK + 1, dtype=jnp.float32)[:, None, None]
    lam_pow = jnp.exp(log_lam[None] * steps)

    def step(h, u_c):
        bu = jnp.einsum('btgc,gpc->btgp', u_c.astype(jnp.complex64), b_bar)
        lam_b = jnp.broadcast_to(lam, bu.shape)
        _, hs = lax.associative_scan(_linear_combine, (lam_b, bu), axis=1)
        hs = hs + lam_pow[None] * h[:, None]
        y = jnp.einsum('btgp,gcp->btgc', hs, c_mat).real
        return hs[:, -1], y

    h_fin, ys = lax.scan(step, h0, u_chunks)
    return jnp.moveaxis(ys, 0, 1).reshape(u.shape), h_fin


def s5_mixer(h_lat, h_ctx, a_re, a_im, log_dt, b_re, b_im, c_re, c_im, d_skip, w_glu, b_glu, need_ctx):
    def grouped(h):
        return h.astype(jnp.float32).reshape(h.shape[:2] + (S5_GROUPS, S5_GROUP))
    u_l, u_c = grouped(h_lat), grouped(h_ctx)
    y_l = jnp.zeros_like(u_l)
    y_c = jnp.zeros_like(u_c)
    for d in range(2):
        log_lam, b_bar = s5_discretise(a_re[d], a_im[d], log_dt[d], b_re[d], b_im[d])
        c_mat = lax.complex(c_re[d].astype(jnp.float32), c_im[d].astype(jnp.float32))
        orient = (lambda t: jnp.flip(t, axis=1)) if d == 1 else (lambda t: t)
        h0 = jnp.zeros((u_c.shape[0], S5_GROUPS, S5_STATE), jnp.complex64)
        yc, h_ctx_fin = s5_scan(orient(u_c), log_lam, b_bar, c_mat, h0)
        yl, _ = s5_scan(orient(u_l), log_lam, b_bar, c_mat, h_ctx_fin)
        y_l = y_l + orient(yl)
        y_c = y_c + orient(yc)
    d_g = d_skip.astype(jnp.float32).reshape(S5_GROUPS, S5_GROUP)

    def finish(y, u, dtype):
        y = (y + d_g * u).reshape(y.shape[:2] + (D_MODEL,))
        y = jax.nn.gelu(y).astype(dtype)
        a, g = jnp.split(y @ w_glu + b_glu, 2, axis=-1)
        return a * jax.nn.sigmoid(g)

    out_l = finish(y_l, u_l, h_lat.dtype)
    out_c = finish(y_c, u_c, h_ctx.dtype) if need_ctx else None
    return out_l, out_c


def diff_core(q, k, v, lam):
    s = jnp.einsum('bqhcd,bkhcd->bhcqk', q, k).astype(jnp.float32) * (DA_HEAD_DIM ** -0.5)
    p = jax.nn.softmax(s, axis=-1)
    a = p[:, :, 0] - lam * p[:, :, 1]
    return jnp.einsum('bhqk,bkhe->bqhe', a.astype(v.dtype), v)


def diff_attention(h_lat, h_ctx, w_qkv, w_o, q_gain, k_gain, lam_q1, lam_k1, lam_q2, lam_k2,
                   sub_gain, lambda_init, need_ctx):
    bsz, n_lat = h_lat.shape[0], h_lat.shape[1]

    def project(h):
        q, k, v = jnp.split(h @ w_qkv, 3, axis=-1)
        shp = h.shape[:2] + (DA_HEADS, 2, DA_HEAD_DIM)
        q = rmsnorm(q.reshape(shp), q_gain)
        k = rmsnorm(k.reshape(shp), k_gain)
        v = v.reshape(h.shape[:2] + (DA_HEADS, DA_VDIM))
        return q, k, v

    q_l, k_l, v_l = project(h_lat)
    cos, sin = axial_rope(n_lat, DA_HEAD_DIM)
    cos = cos[:, None, None, :]
    sin = sin[:, None, None, :]
    q_l = apply_rope(q_l, cos, sin)
    k_l = apply_rope(k_l, cos, sin)
    q_c, k_c, v_c = project(h_ctx)
    k_all = jnp.concatenate([k_c, k_l], axis=1)
    v_all = jnp.concatenate([v_c, v_l], axis=1)
    lam = (jnp.exp(jnp.sum(lam_q1.astype(jnp.float32) * lam_k1.astype(jnp.float32)))
           - jnp.exp(jnp.sum(lam_q2.astype(jnp.float32) * lam_k2.astype(jnp.float32)))
           + lambda_init)

    def finish(o):
        o = rmsnorm(o, sub_gain) * (1.0 - lambda_init)
        return o.reshape(o.shape[:2] + (D_MODEL,)) @ w_o

    n_blk = n_lat // Q_BLOCK
    q_blocks = jnp.moveaxis(q_l.reshape((bsz, n_blk, Q_BLOCK) + q_l.shape[2:]), 1, 0)
    o_blocks = lax.map(lambda qb: diff_core(qb, k_all, v_all, lam), q_blocks)
    o_l = jnp.moveaxis(o_blocks, 0, 1).reshape((bsz, n_lat, DA_HEADS, DA_VDIM))
    y_l = finish(o_l)
    y_c = finish(diff_core(q_c, k_c, v_c, lam)) if need_ctx else None
    return y_l, y_c


def moe_ffn(h, w_router, b_router, w_gu, b_gu, w_down, b_down):
    n_tok, d = h.shape
    logits = (h @ w_router).astype(jnp.float32) + b_router.astype(jnp.float32)
    top_logit, top_e = lax.top_k(logits, TOP_K)
    gates = jax.nn.softmax(top_logit, axis=-1)
    nk = n_tok * TOP_K
    flat_e = top_e.reshape(nk)
    flat_tok = jnp.arange(nk, dtype=jnp.int32) // TOP_K
    flat_gate = gates.reshape(nk)
    order = jnp.argsort(flat_e, stable=True)
    sorted_e = flat_e[order]
    counts = jnp.bincount(flat_e, length=N_EXPERTS)
    padded = (counts + MOE_BLOCK - 1) // MOE_BLOCK * MOE_BLOCK
    pad_end = jnp.cumsum(padded)
    pad_start = pad_end - padded
    start = jnp.cumsum(counts) - counts
    dest = pad_start[sorted_e] + jnp.arange(nk, dtype=jnp.int32) - start[sorted_e]
    n_blocks = -(-nk // MOE_BLOCK) + N_EXPERTS
    buf = n_blocks * MOE_BLOCK
    tok_buf = jnp.zeros((buf,), jnp.int32).at[dest].set(flat_tok[order])
    gate_buf = jnp.zeros((buf,), jnp.float32).at[dest].set(flat_gate[order])
    block_start = jnp.arange(n_blocks, dtype=jnp.int32) * MOE_BLOCK
    block_e = jnp.minimum(jnp.searchsorted(pad_end, block_start, side='right'), N_EXPERTS - 1)

    def expert_block(args):
        toks, e = args
        xb = h[toks]
        gu = xb @ w_gu[e] + b_gu[e]
        gate, up = jnp.split(gu, 2, axis=-1)
        gate = jnp.minimum(gate, SWIGLU_LIMIT)
        up = jnp.clip(up, -SWIGLU_LIMIT, SWIGLU_LIMIT)
        act = (up + 1.0) * gate * jax.nn.sigmoid(SWIGLU_ALPHA * gate)
        return act @ w_down[e] + b_down[e]

    y = lax.map(expert_block, (tok_buf.reshape(n_blocks, MOE_BLOCK), block_e))
    y = y.reshape(buf, d) * gate_buf[:, None].astype(y.dtype)
    return jnp.zeros_like(h).at[tok_buf].add(y)


def setup_inputs(seed: int = 0) -> dict:
    key = jax.random.key(seed)
    ks = iter(jax.random.split(key, 48))
    f32 = jnp.float32
    n_a = (DEPTH + N_MIXERS - 1) // N_MIXERS
    n_b = DEPTH // N_MIXERS
    G, P, CG = S5_GROUPS, S5_STATE, S5_GROUP

    def nrm(shape, std):
        return jax.random.normal(next(ks), shape, f32) * std

    x = nrm((BATCH, SEQ, D_MODEL), 1.0)
    c = nrm((BATCH, D_MODEL), 1.0)
    ctx = nrm((BATCH, CTX_LEN, D_MODEL), 1.0)
    c_ctx = nrm((D_MODEL,), 1.0)
    w_ada = nrm((DEPTH, D_MODEL, 6 * D_MODEL), 0.3 * D_MODEL ** -0.5)
    b_ada = nrm((DEPTH, 6 * D_MODEL), 0.02)
    g_mix = 1.0 + nrm((DEPTH, D_MODEL), 0.02)
    g_ffn = 1.0 + nrm((DEPTH, D_MODEL), 0.02)
    s5_a_re = -0.5 + nrm((n_a, 2, G, P), 0.01)
    s5_a_im = math.pi * jnp.arange(P, dtype=f32) + nrm((n_a, 2, G, P), 0.01)
    s5_log_dt = jax.random.uniform(next(ks), (n_a, 2, G), f32, math.log(S5_DT_MIN), math.log(S5_DT_MAX))
    s5_b_re = nrm((n_a, 2, G, P, CG), (2 * CG) ** -0.5)
    s5_b_im = nrm((n_a, 2, G, P, CG), (2 * CG) ** -0.5)
    s5_c_re = nrm((n_a, 2, G, CG, P), P ** -0.5)
    s5_c_im = nrm((n_a, 2, G, CG, P), P ** -0.5)
    s5_d = nrm((n_a, D_MODEL), 0.5)
    s5_w_glu = nrm((n_a, D_MODEL, 2 * D_MODEL), D_MODEL ** -0.5)
    s5_b_glu = nrm((n_a, 2 * D_MODEL), 0.02)
    da_w_qkv = nrm((n_b, D_MODEL, 3 * D_MODEL), D_MODEL ** -0.5)
    da_w_o = nrm((n_b, D_MODEL, D_MODEL), D_MODEL ** -0.5)
    da_q_gain = 1.0 + nrm((n_b, DA_HEAD_DIM), 0.02)
    da_k_gain = 1.0 + nrm((n_b, DA_HEAD_DIM), 0.02)
    da_lam_q1 = nrm((n_b, DA_HEAD_DIM), 0.1)
    da_lam_k1 = nrm((n_b, DA_HEAD_DIM), 0.1)
    da_lam_q2 = nrm((n_b, DA_HEAD_DIM), 0.1)
    da_lam_k2 = nrm((n_b, DA_HEAD_DIM), 0.1)
    da_sub_gain = 1.0 + nrm((n_b, DA_VDIM), 0.02)
    moe_w_router = nrm((DEPTH, D_MODEL, N_EXPERTS), D_MODEL ** -0.5)
    moe_b_router = nrm((DEPTH, N_EXPERTS), 0.01)
    moe_w_gu = nrm((DEPTH, N_EXPERTS, D_MODEL, 2 * D_FF), D_MODEL ** -0.5)
    moe_b_gu = nrm((DEPTH, N_EXPERTS, 2 * D_FF), 0.02)
    moe_w_down = nrm((DEPTH, N_EXPERTS, D_FF, D_MODEL), D_FF ** -0.5)
    moe_b_down = nrm((DEPTH, N_EXPERTS, D_MODEL), 0.02)
    return {'x': x, 'c': c, 'ctx': ctx, 'c_ctx': c_ctx,
            'w_ada': w_ada, 'b_ada': b_ada, 'g_mix': g_mix, 'g_ffn': g_ffn,
            's5_a_re': s5_a_re, 's5_a_im': s5_a_im, 's5_log_dt': s5_log_dt,
            's5_b_re': s5_b_re, 's5_b_im': s5_b_im, 's5_c_re': s5_c_re, 's5_c_im': s5_c_im,
            's5_d': s5_d, 's5_w_glu': s5_w_glu, 's5_b_glu': s5_b_glu,
            'da_w_qkv': da_w_qkv, 'da_w_o': da_w_o, 'da_q_gain': da_q_gain, 'da_k_gain': da_k_gain,
            'da_lam_q1': da_lam_q1, 'da_lam_k1': da_lam_k1, 'da_lam_q2': da_lam_q2, 'da_lam_k2': da_lam_k2,
            'da_sub_gain': da_sub_gain,
            'moe_w_router': moe_w_router, 'moe_b_router': moe_b_router, 'moe_w_gu': moe_w_gu,
            'moe_b_gu': moe_b_gu, 'moe_w_down': moe_w_down, 'moe_b_down': moe_b_down}


def reference(x, c, ctx, c_ctx, w_ada, b_ada, g_mix, g_ffn,
              s5_a_re, s5_a_im, s5_log_dt, s5_b_re, s5_b_im, s5_c_re, s5_c_im,
              s5_d, s5_w_glu, s5_b_glu,
              da_w_qkv, da_w_o, da_q_gain, da_k_gain, da_lam_q1, da_lam_k1, da_lam_q2, da_lam_k2,
              da_sub_gain,
              moe_w_router, moe_b_router, moe_w_gu, moe_b_gu, moe_w_down, moe_b_down):
    silu_c = jax.nn.silu(c)
    silu_cc = jax.nn.silu(c_ctx)
    for i in range(DEPTH):
        last = i == DEPTH - 1
        j = i // N_MIXERS
        mod_l = (silu_c @ w_ada[i] + b_ada[i])[:, None, :]
        mod_c = silu_cc @ w_ada[i] + b_ada[i]
        sh1, sc1, gt1, sh2, sc2, gt2 = jnp.split(mod_l, 6, axis=-1)
        csh1, csc1, cgt1, csh2, csc2, cgt2 = jnp.split(mod_c, 6, axis=-1)
        h_l = rmsnorm(x, g_mix[i]) * (1.0 + sc1) + sh1
        h_c = rmsnorm(ctx, g_mix[i]) * (1.0 + csc1) + csh1
        if i % N_MIXERS == 0:
            y_l, y_c = s5_mixer(h_l, h_c, s5_a_re[j], s5_a_im[j], s5_log_dt[j], s5_b_re[j], s5_b_im[j],
                                s5_c_re[j], s5_c_im[j], s5_d[j], s5_w_glu[j], s5_b_glu[j], not last)
        else:
            lambda_init = 0.8 - 0.6 * math.exp(-0.3 * i)
            y_l, y_c = diff_attention(h_l, h_c, da_w_qkv[j], da_w_o[j], da_q_gain[j], da_k_gain[j],
                                      da_lam_q1[j], da_lam_k1[j], da_lam_q2[j], da_lam_k2[j],
                                      da_sub_gain[j], lambda_init, not last)
        x = x + gt1 * y_l
        if not last:
            ctx = ctx + cgt1 * y_c
        h_l = rmsnorm(x, g_ffn[i]) * (1.0 + sc2) + sh2
        if last:
            out = moe_ffn(h_l.reshape(-1, D_MODEL), moe_w_router[i], moe_b_router[i], moe_w_gu[i],
                          moe_b_gu[i], moe_w_down[i], moe_b_down[i])
            x = x + gt2 * out.reshape(x.shape)
        else:
            h_c = rmsnorm(ctx, g_ffn[i]) * (1.0 + csc2) + csh2
            n_c = h_c.shape[0] * h_c.shape[1]
            tokens = jnp.concatenate([h_c.reshape(-1, D_MODEL), h_l.reshape(-1, D_MODEL)], axis=0)
            out = moe_ffn(tokens, moe_w_router[i], moe_b_router[i], moe_w_gu[i],
                          moe_b_gu[i], moe_w_down[i], moe_b_down[i])
            ctx = ctx + cgt2 * out[:n_c].reshape(ctx.shape)
            x = x + gt2 * out[n_c:].reshape(x.shape)
    return x
```

```python
import functools
import math

import jax
import jax.numpy as jnp
from jax import lax
from jax.experimental import pallas as pl
from jax.experimental.pallas import tpu as pltpu

NORM_EPS = 1e-6
N_MIXERS = 2
S5_GROUP = 16
S5_STATE = 64
DA_HEAD_DIM = 64
DA_VDIM = 2 * DA_HEAD_DIM
GRID_W = 64
ROPE_BASE = 10000.0
N_EXPERTS = 32
TOP_K = 4
SWIGLU_LIMIT = 7.0
SWIGLU_ALPHA = 1.702

LANES = 128
SUBLANES = 8
VMEM_LIMIT_BYTES = 56 * 1024 * 1024

S5_SLAB_GROUPS = LANES // S5_GROUP
S5_SLAB_STATE = S5_SLAB_GROUPS * S5_STATE
S5_CHUNK = 64
S5_UNROLL = 4
TM_TILE = 32
ROW_TILE = 256
ROUTER_TILE = 512
EXPERT_ROWS = 256
COMBINE_TILE = 256
Q_TILE = 256
MOD_ROWS = 24

F32 = jnp.float32
BF16 = jnp.bfloat16


def _cparams(sem, vmem=VMEM_LIMIT_BYTES):
    return pltpu.CompilerParams(dimension_semantics=sem, vmem_limit_bytes=vmem)


def _rms(x):
    return x * lax.rsqrt(jnp.mean(x * x, axis=-1, keepdims=True) + NORM_EPS)


def _ada_kernel(c_ref, w_ref, b_ref, o_ref):
    cv = c_ref[...]
    s = cv * jax.nn.sigmoid(cv)
    o_ref[...] = jnp.dot(s, w_ref[...], precision=lax.Precision.HIGHEST,
                         preferred_element_type=F32) + b_ref[...]


def _ada(cc, w_ada, b_ada):
    depth, d, d6 = w_ada.shape
    nj = d6 // d
    return pl.pallas_call(
        _ada_kernel,
        out_shape=jax.ShapeDtypeStruct((depth, MOD_ROWS, d6), F32),
        grid=(depth, nj),
        in_specs=[pl.BlockSpec((MOD_ROWS, d), lambda i, j: (0, 0)),
                  pl.BlockSpec((None, d, d), lambda i, j: (i, 0, j)),
                  pl.BlockSpec((None, 1, d), lambda i, j: (i, 0, j))],
        out_specs=pl.BlockSpec((None, MOD_ROWS, d), lambda i, j: (i, 0, j)),
        compiler_params=_cparams(("parallel", "parallel")),
        name="ada",
    )(cc, w_ada, b_ada.reshape(depth, 1, d6))


def _prenorm_tm_kernel(ctx_ref, x_ref, mod_ref, g_ref, o_ref, *, n_ctx_tiles, nb, d):
    i = pl.program_id(0)
    tt = o_ref.shape[1] // nb
    g = g_ref[...]

    def emit(src_ref, mod_row):
        for b in range(nb):
            r = mod_row(b)
            sh = mod_ref[r:r + 1, 0:d]
            sc = mod_ref[r:r + 1, d:2 * d]
            h = (_rms(src_ref[b]) * g) * (1.0 + sc) + sh
            for s in range(d // LANES):
                o_ref[s, pl.ds(b, tt, stride=nb), :] = h[:, s * LANES:(s + 1) * LANES]

    @pl.when(i < n_ctx_tiles)
    def _():
        emit(ctx_ref, lambda b: nb)

    @pl.when(i >= n_ctx_tiles)
    def _():
        emit(x_ref, lambda b: b)


def _prenorm_tm(ctx, x, mod, g):
    nb, n_ctx, d = ctx.shape
    n_lat = x.shape[1]
    tt = TM_TILE
    nct, nlt = n_ctx // tt, n_lat // tt
    kern = functools.partial(_prenorm_tm_kernel, n_ctx_tiles=nct, nb=nb, d=d)
    return pl.pallas_call(
        kern,
        out_shape=jax.ShapeDtypeStruct((d // LANES, (n_ctx + n_lat) * nb, LANES), F32),
        grid=(nct + nlt,),
        in_specs=[pl.BlockSpec((nb, tt, d), lambda i: (0, jnp.minimum(i, nct - 1), 0)),
                  pl.BlockSpec((nb, tt, d), lambda i: (0, jnp.maximum(i - nct, 0), 0)),
                  pl.BlockSpec((MOD_ROWS, 2 * d), lambda i: (0, 0)),
                  pl.BlockSpec((1, d), lambda i: (0, 0))],
        out_specs=pl.BlockSpec((d // LANES, tt * nb, LANES), lambda i: (0, i, 0)),
        compiler_params=_cparams(("parallel",)),
        name="prenorm_tm",
    )(ctx, x, mod, g)


def _s5_disc_kernel(ar_ref, ai_ref, ldt_ref, br_ref, bi_ref, lr_ref, li_ref, bbr_ref, bbi_ref):
    ar, ai = ar_ref[...], ai_ref[...]
    dt = jnp.exp(ldt_ref[...])
    mag = jnp.exp(ar * dt)
    lr = mag * jnp.cos(ai * dt)
    li = mag * jnp.sin(ai * dt)
    nr, ni = lr - 1.0, li
    den = ar * ar + ai * ai
    qr = (nr * ar + ni * ai) / den
    qi = (ni * ar - nr * ai) / den
    br, bi = br_ref[...], bi_ref[...]
    lr_ref[...] = lr
    li_ref[...] = li
    bbr_ref[...] = qr * br - qi * bi
    bbi_ref[...] = qr * bi + qi * br


def _s5_disc(a_re, a_im, log_dt, b_re, b_im):
    nd, g, p = a_re.shape
    cg = b_re.shape[-1]
    gp = g * p
    col = lambda a: a.reshape(nd, gp, 1)
    ldt = jnp.broadcast_to(log_dt[:, :, None], (nd, g, p))
    spec1 = pl.BlockSpec((None, gp, 1), lambda i: (i, 0, 0))
    specb = pl.BlockSpec((None, gp, cg), lambda i: (i, 0, 0))
    lr, li, bbr, bbi = pl.pallas_call(
        _s5_disc_kernel,
        out_shape=[jax.ShapeDtypeStruct((nd, gp, 1), F32)] * 2 + [jax.ShapeDtypeStruct((nd, gp, cg), F32)] * 2,
        grid=(nd,),
        in_specs=[spec1, spec1, spec1, specb, specb],
        out_specs=[spec1, spec1, specb, specb],
        compiler_params=_cparams(("parallel",)),
        name="s5_disc",
    )(col(a_re), col(a_im), col(ldt), b_re.reshape(nd, gp, cg), b_im.reshape(nd, gp, cg))
    return (lr.reshape(nd, g, p), li.reshape(nd, g, p),
            bbr.reshape(nd, g, p, cg), bbi.reshape(nd, g, p, cg))


def _s5_slab_params(lam_re, lam_im, bb_re, bb_im, c_re, c_im):
    nd, g, p = lam_re.shape
    cg = bb_re.shape[-1]
    ns, sg = g // S5_SLAB_GROUPS, S5_SLAB_GROUPS
    eye = jnp.eye(sg, dtype=F32)

    def b_mat(bb):
        bb = bb.reshape(nd, ns, sg, p, cg)
        return jnp.einsum('dsjpc,jk->dsjckp', bb, eye).reshape(nd, ns, sg * cg, sg * p)

    def c_mat(cm):
        cm = cm.reshape(nd, ns, sg, cg, p)
        return jnp.einsum('dsjcp,jk->dskpjc', cm, eye).reshape(nd, ns, sg * p, sg * cg)

    bm = jnp.concatenate([b_mat(bb_re), b_mat(bb_im)], axis=-1).astype(BF16)
    lam = lambda a: a.reshape(nd, ns, 1, sg * p)
    return bm, c_mat(c_re).astype(BF16), c_mat(c_im).astype(BF16), lam(lam_re), lam(lam_im)


def _s5_scan_kernel(u_ref, bm_ref, cr_ref, ci_ref, lr_ref, li_ref, y_ref, hs_ref, st_ref):
    d = pl.program_id(0)
    c = pl.program_id(3)
    t_len = u_ref.shape[0]
    ns = S5_SLAB_STATE

    @pl.when(c == 0)
    def _():
        st_ref[...] = jnp.zeros_like(st_ref)

    u = u_ref[...].reshape(t_len * SUBLANES, LANES).astype(BF16)
    hs_ref[...] = jnp.dot(u, bm_ref[...], preferred_element_type=F32)

    lr = jnp.broadcast_to(lr_ref[...], (SUBLANES, ns))
    li = jnp.broadcast_to(li_ref[...], (SUBLANES, ns))

    def steps(j, carry):
        hr, hi = carry
        for q in range(S5_UNROLL):
            s = j * S5_UNROLL + q
            t = jnp.where(d == 0, s, t_len - 1 - s)
            row = pl.multiple_of(t * SUBLANES, SUBLANES)
            br = hs_ref[pl.ds(row, SUBLANES), 0:ns]
            bi = hs_ref[pl.ds(row, SUBLANES), ns:2 * ns]
            nr = lr * hr - li * hi + br
            ni = lr * hi + li * hr + bi
            hs_ref[pl.ds(row, SUBLANES), 0:ns] = nr
            hs_ref[pl.ds(row, SUBLANES), ns:2 * ns] = ni
            hr, hi = nr, ni
        return hr, hi

    hr, hi = lax.fori_loop(0, t_len // S5_UNROLL, steps, (st_ref[0], st_ref[1]))
    st_ref[0] = hr
    st_ref[1] = hi

    y = (jnp.dot(hs_ref[:, 0:ns].astype(BF16), cr_ref[...], preferred_element_type=F32)
         - jnp.dot(hs_ref[:, ns:2 * ns].astype(BF16), ci_ref[...], preferred_element_type=F32))
    y_ref[...] = y.reshape(t_len, SUBLANES, LANES)


def _s5_scan(u_t, bm, cr, ci, lam_r, lam_i, nb, n_ctx, n_tot):
    nslab = u_t.shape[0]
    nbo = nb // SUBLANES
    t_len = S5_CHUNK
    ncc, nc = n_ctx // t_len, n_tot // t_len
    u5 = u_t.reshape(nslab, n_tot, nbo, SUBLANES, LANES)

    def chunk(d, c):
        back = jnp.where(c < ncc, ncc - 1 - c, ncc + nc - 1 - c)
        return jnp.where(d == 0, c, back)

    pspec = lambda r, k: pl.BlockSpec((None, None, r, k), lambda d, bo, s, c: (d, s, 0, 0))
    return pl.pallas_call(
        _s5_scan_kernel,
        out_shape=jax.ShapeDtypeStruct((2, nslab, n_tot, nbo, SUBLANES, LANES), F32),
        grid=(2, nbo, nslab, nc),
        in_specs=[pl.BlockSpec((None, t_len, None, SUBLANES, LANES),
                               lambda d, bo, s, c: (s, chunk(d, c), bo, 0, 0)),
                  pspec(LANES, 2 * S5_SLAB_STATE), pspec(S5_SLAB_STATE, LANES), pspec(S5_SLAB_STATE, LANES),
                  pspec(1, S5_SLAB_STATE), pspec(1, S5_SLAB_STATE)],
        out_specs=pl.BlockSpec((None, None, t_len, None, SUBLANES, LANES),
                               lambda d, bo, s, c: (d, s, chunk(d, c), bo, 0, 0)),
        scratch_shapes=[pltpu.VMEM((t_len * SUBLANES, 2 * S5_SLAB_STATE), F32),
                        pltpu.VMEM((2, SUBLANES, S5_SLAB_STATE), F32)],
        compiler_params=_cparams(("parallel", "parallel", "parallel", "arbitrary")),
        name="s5_scan",
    )(u5, bm, cr, ci, lam_r, lam_i)


def _s5_out_kernel(y_ref, u_ref, dsk_ref, w_ref, bg_ref, ctx_ref, x_ref, mod_ref, o_ref, tr_ref,
                   *, n_ctx_tiles, nb, d):
    i = pl.program_id(0)
    nslab = d // LANES
    rows = u_ref.shape[1]
    tt = rows // nb
    y = (y_ref[0] + y_ref[1]).reshape(nslab, rows, LANES)
    v = jax.nn.gelu(y + dsk_ref[...] * u_ref[...])
    lhs = jnp.concatenate([v[s] for s in range(nslab)], axis=-1).astype(BF16)
    z = jnp.dot(lhs, w_ref[...], preferred_element_type=F32) + bg_ref[...]
    m = z[:, 0:d] * jax.nn.sigmoid(z[:, d:2 * d])
    for s in range(nslab):
        tr_ref[s] = m[:, s * LANES:(s + 1) * LANES]

    def emit(src_ref, mod_row):
        for b in range(nb):
            r = mod_row(b)
            gt = mod_ref[r:r + 1, :]
            mb = jnp.concatenate([tr_ref[s, pl.ds(b, tt, stride=nb), :] for s in range(nslab)], axis=-1)
            o_ref[b] = src_ref[b] + gt * mb

    @pl.when(i < n_ctx_tiles)
    def _():
        emit(ctx_ref, lambda b: nb)

    @pl.when(i >= n_ctx_tiles)
    def _():
        emit(x_ref, lambda b: b)


def _s5_out(y_t, u_t, d_skip, w_glu, b_glu, ctx, x, gt1):
    nb, n_ctx, d = ctx.shape
    n_lat = x.shape[1]
    n_tot = n_ctx + n_lat
    nslab, nbo = d // LANES, nb // SUBLANES
    tt = TM_TILE
    nct, nlt = n_ctx // tt, n_lat // tt
    kern = functools.partial(_s5_out_kernel, n_ctx_tiles=nct, nb=nb, d=d)
    return pl.pallas_call(
        kern,
        out_shape=jax.ShapeDtypeStruct((nb, n_tot, d), F32),
        grid=(nct + nlt,),
        in_specs=[pl.BlockSpec((2, nslab, tt, nbo, SUBLANES, LANES), lambda i: (0, 0, i, 0, 0, 0)),
                  pl.BlockSpec((nslab, tt * nb, LANES), lambda i: (0, i, 0)),
                  pl.BlockSpec((nslab, 1, LANES), lambda i: (0, 0, 0)),
                  pl.BlockSpec((d, 2 * d), lambda i: (0, 0)),
                  pl.BlockSpec((1, 2 * d), lambda i: (0, 0)),
                  pl.BlockSpec((nb, tt, d), lambda i: (0, jnp.minimum(i, nct - 1), 0)),
                  pl.BlockSpec((nb, tt, d), lambda i: (0, jnp.maximum(i - nct, 0), 0)),
                  pl.BlockSpec((MOD_ROWS, d), lambda i: (0, 0))],
        out_specs=pl.BlockSpec((nb, tt, d), lambda i: (0, i, 0)),
        scratch_shapes=[pltpu.VMEM((nslab, tt * nb, LANES), F32)],
        compiler_params=_cparams(("parallel",)),
        name="s5_out",
    )(y_t, u_t, d_skip.reshape(nslab, 1, LANES), w_glu.astype(BF16), b_glu.reshape(1, 2 * d), ctx, x, gt1)


def _s5_layer(xs_ctx, xs_lat, mod, g_mix, a_re, a_im, log_dt, b_re, b_im, c_re, c_im, d_skip, w_glu, b_glu):
    nb, n_ctx, d = xs_ctx.shape
    n_tot = n_ctx + xs_lat.shape[1]
    u_t = _prenorm_tm(xs_ctx, xs_lat, mod[:, 0:2 * d], g_mix.reshape(1, d))
    lam_re, lam_im, bb_re, bb_im = _s5_disc(a_re, a_im, log_dt, b_re, b_im)
    bm, cr, ci, lam_r, lam_i = _s5_slab_params(lam_re, lam_im, bb_re, bb_im, c_re, c_im)
    y_t = _s5_scan(u_t, bm, cr, ci, lam_r, lam_i, nb, n_ctx, n_tot)
    return _s5_out(y_t, u_t, d_skip, w_glu, b_glu, xs_ctx, xs_lat, mod[:, 2 * d:3 * d])


NEG_PAD = -1e30
NEG_MASK = -3e38


def _split_bf16(a):
    hi = a.astype(BF16)
    return hi, (a - hi.astype(F32)).astype(BF16)


def _router_kernel(x_ref, mod_ref, g_ref, w_ref, bias_ref, h_ref, sel_ref, cnt_ref, carry_ref,
                   *, n_ctx_tiles, nb, d):
    b = pl.program_id(0)
    j = pl.program_id(1)
    rows = x_ref.shape[0]

    @pl.when((b == 0) & (j == 0))
    def _():
        carry_ref[...] = jnp.zeros_like(carry_ref)

    row = jnp.where(j < n_ctx_tiles, nb, b)
    mrow = mod_ref[pl.ds(row, 1), :]
    h = (_rms(x_ref[...]) * g_ref[...]) * (1.0 + mrow[:, d:2 * d]) + mrow[:, 0:d]
    h_ref[...] = h

    hi, lo = _split_bf16(h)
    whi, wlo = _split_bf16(w_ref[...])
    dot = functools.partial(jnp.dot, preferred_element_type=F32)
    logits = dot(hi, whi) + dot(lo, whi) + dot(hi, wlo) + bias_ref[...]

    lane = lax.broadcasted_iota(jnp.int32, (rows, LANES), 1)
    work = logits
    tops, idxs, hots = [], [], []
    for _ in range(TOP_K):
        m = jnp.max(work, axis=-1, keepdims=True)
        idx = jnp.min(jnp.where(work == m, lane, LANES), axis=-1, keepdims=True)
        hot = lane == idx
        work = jnp.where(hot, NEG_MASK, work)
        tops.append(m)
        idxs.append(idx)
        hots.append(hot)
    exps = [jnp.exp(m - tops[0]) for m in tops]
    denom = exps[0] + exps[1] + exps[2] + exps[3]

    multi = jnp.zeros((rows, LANES), F32)
    for hot in hots:
        multi = multi + jnp.where(hot, 1.0, 0.0)
    r_i = lax.broadcasted_iota(jnp.int32, (rows, rows), 0)
    c_i = lax.broadcasted_iota(jnp.int32, (rows, rows), 1)
    tri = jnp.where(c_i < r_i, 1.0, 0.0).astype(BF16)
    before = dot(tri, multi.astype(BF16)) + carry_ref[0:1, :]
    carry_ref[0:1, :] = carry_ref[0:1, :] + jnp.sum(multi, axis=0, keepdims=True)

    sel = jnp.zeros((rows, LANES), F32)
    for k in range(TOP_K):
        rank = jnp.sum(jnp.where(hots[k], before, 0.0), axis=-1, keepdims=True)
        sel = jnp.where(lane == k, idxs[k].astype(F32), sel)
        sel = jnp.where(lane == TOP_K + k, exps[k] / denom, sel)
        sel = jnp.where(lane == 2 * TOP_K + k, rank, sel)
    sel_ref[...] = sel
    cnt_ref[...] = jnp.broadcast_to(carry_ref[0:1, :], cnt_ref.shape)


def _router(xs, n_ctx, mod_shsc, g, w_router, b_router):
    nb, t_len, d = xs.shape
    r = ROW_TILE
    nj = t_len // r
    ne = w_router.shape[1]
    w_pad = jnp.zeros((d, LANES), F32).at[:, :ne].set(w_router)
    b_pad = jnp.full((1, LANES), NEG_PAD, F32).at[0, :ne].set(b_router)
    kern = functools.partial(_router_kernel, n_ctx_tiles=n_ctx // r, nb=nb, d=d)
    return pl.pallas_call(
        kern,
        out_shape=[jax.ShapeDtypeStruct((nb, t_len, d), F32),
                   jax.ShapeDtypeStruct((nb, t_len, LANES), F32),
                   jax.ShapeDtypeStruct((SUBLANES, LANES), F32)],
        grid=(nb, nj),
        in_specs=[pl.BlockSpec((None, r, d), lambda b, j: (b, j, 0)),
                  pl.BlockSpec((MOD_ROWS, 2 * d), lambda b, j: (0, 0)),
                  pl.BlockSpec((1, d), lambda b, j: (0, 0)),
                  pl.BlockSpec((d, LANES), lambda b, j: (0, 0)),
                  pl.BlockSpec((1, LANES), lambda b, j: (0, 0))],
        out_specs=[pl.BlockSpec((None, r, d), lambda b, j: (b, j, 0)),
                   pl.BlockSpec((None, r, LANES), lambda b, j: (b, j, 0)),
                   pl.BlockSpec((SUBLANES, LANES), lambda b, j: (0, 0))],
        scratch_shapes=[pltpu.VMEM((SUBLANES, LANES), F32)],
        compiler_params=_cparams(("arbitrary", "arbitrary")),
        name="router",
    )(xs, mod_shsc, g.reshape(1, d), w_pad, b_pad)


def _dispatch_kernel(dest_ref, h_ref, zero_ref, xbuf_ref, sem):
    del zero_ref
    rows = h_ref.shape[0]

    def body(r, carry):
        for k in range(TOP_K):
            dst = dest_ref[r * TOP_K + k]
            pltpu.make_async_copy(h_ref.at[pl.ds(r, 1), :], xbuf_ref.at[pl.ds(dst, 1), :], sem).start()
        return carry

    lax.fori_loop(0, rows, body, 0)
    for _ in range(TOP_K):
        pltpu.make_async_copy(h_ref, xbuf_ref.at[pl.ds(0, rows), :], sem).wait()


def _dispatch(h2, dest_flat, n_buf):
    nb, t_len, d = h2.shape
    r = ROW_TILE
    nj = t_len // r
    return pl.pallas_call(
        _dispatch_kernel,
        out_shape=jax.ShapeDtypeStruct((n_buf, d), F32),
        grid=(nb, nj),
        in_specs=[pl.BlockSpec((r * TOP_K,), lambda b, j: (b * nj + j,), memory_space=pltpu.SMEM),
                  pl.BlockSpec((r, d), lambda b, j: (b * nj + j, 0)),
                  pl.BlockSpec(memory_space=pl.ANY)],
        out_specs=pl.BlockSpec(memory_space=pl.ANY),
        scratch_shapes=[pltpu.SemaphoreType.DMA(())],
        input_output_aliases={2: 0},
        compiler_params=_cparams(("arbitrary", "arbitrary")),
        name="dispatch",
    )(dest_flat, h2.reshape(nb * t_len, d), jnp.zeros((n_buf, d), F32))


def _expert_kernel(be_ref, first_ref, nused_ref, x_ref, wgu_ref, bgu_ref, wd_ref, bd_ref, y_ref,
                   wgu_bf, wd_bf):
    i = pl.program_id(0)
    dff = wd_ref.shape[0]

    @pl.when(i < nused_ref[0])
    def _():
        @pl.when(first_ref[i] == 1)
        def _():
            wgu_bf[...] = wgu_ref[...].astype(BF16)
            wd_bf[...] = wd_ref[...].astype(BF16)

        gu = jnp.dot(x_ref[...].astype(BF16), wgu_bf[...], preferred_element_type=F32) + bgu_ref[...]
        gate = jnp.minimum(gu[:, 0:dff], SWIGLU_LIMIT)
        up = jnp.clip(gu[:, dff:2 * dff], -SWIGLU_LIMIT, SWIGLU_LIMIT)
        act = (up + 1.0) * gate * jax.nn.sigmoid(SWIGLU_ALPHA * gate)
        y_ref[...] = jnp.dot(act.astype(BF16), wd_bf[...], preferred_element_type=F32) + bd_ref[...]

    @pl.when(i >= nused_ref[0])
    def _():
        y_ref[...] = jnp.zeros_like(y_ref)


def _experts(xbuf, block_e, first, n_used, w_gu, b_gu, w_down, b_down):
    n_buf, d = xbuf.shape
    ne, _, dff2 = w_gu.shape
    dff = dff2 // 2
    rb = EXPERT_ROWS
    blk = lambda i, nu: jnp.minimum(i, nu[0] - 1)
    grid_spec = pltpu.PrefetchScalarGridSpec(
        num_scalar_prefetch=3,
        grid=(n_buf // rb,),
        in_specs=[pl.BlockSpec((rb, d), lambda i, be, fi, nu: (blk(i, nu), 0)),
                  pl.BlockSpec((None, d, dff2), lambda i, be, fi, nu: (be[blk(i, nu)], 0, 0)),
                  pl.BlockSpec((None, 1, dff2), lambda i, be, fi, nu: (be[blk(i, nu)], 0, 0)),
                  pl.BlockSpec((None, dff, d), lambda i, be, fi, nu: (be[blk(i, nu)], 0, 0)),
                  pl.BlockSpec((None, 1, d), lambda i, be, fi, nu: (be[blk(i, nu)], 0, 0))],
        out_specs=pl.BlockSpec((rb, d), lambda i, be, fi, nu: (i, 0)),
        scratch_shapes=[pltpu.VMEM((d, dff2), BF16), pltpu.VMEM((dff, d), BF16)])
    return pl.pallas_call(
        _expert_kernel,
        out_shape=jax.ShapeDtypeStruct((n_buf, d), F32),
        grid_spec=grid_spec,
        compiler_params=_cparams(("arbitrary",)),
        name="experts",
    )(block_e, first, n_used, xbuf, w_gu, b_gu.reshape(ne, 1, dff2), w_down, b_down.reshape(ne, 1, d))


def _combine_kernel(dcur_ref, dnext_ref, sel_ref, x_ref, mod_ref, ybuf_ref, o_ref, gath_ref, sem,
                    *, n_ctx_tiles, nb):
    b = pl.program_id(0)
    j = pl.program_id(1)
    nj = pl.num_programs(1)
    step = b * nj + j
    n_steps = pl.num_programs(0) * nj
    slot = lax.rem(step, 2)
    rows = x_ref.shape[0]

    def issue(dref, to_slot):
        def body(r, carry):
            for k in range(TOP_K):
                src = dref[r * TOP_K + k]
                pltpu.make_async_copy(ybuf_ref.at[pl.ds(src, 1), :],
                                      gath_ref.at[to_slot, k, pl.ds(r, 1), :], sem.at[to_slot]).start()
            return carry
        lax.fori_loop(0, rows, body, 0)

    @pl.when(step == 0)
    def _():
        issue(dcur_ref, 0)

    @pl.when(step + 1 < n_steps)
    def _():
        issue(dnext_ref, 1 - slot)

    for k in range(TOP_K):
        pltpu.make_async_copy(ybuf_ref.at[pl.ds(0, rows), :], gath_ref.at[slot, k], sem.at[slot]).wait()

    sel = sel_ref[...]
    acc = sel[:, TOP_K:TOP_K + 1] * gath_ref[slot, 0]
    for k in range(1, TOP_K):
        acc = acc + sel[:, TOP_K + k:TOP_K + k + 1] * gath_ref[slot, k]
    row = jnp.where(j < n_ctx_tiles, nb, b)
    o_ref[...] = x_ref[...] + mod_ref[pl.ds(row, 1), :] * acc


def _combine(ybuf, dest_flat, sel, xs, n_ctx, mod_gt):
    nb, t_len, d = xs.shape
    r = ROW_TILE
    nj = t_len // r
    kern = functools.partial(_combine_kernel, n_ctx_tiles=n_ctx // r, nb=nb)
    last = nb * nj - 1
    return pl.pallas_call(
        kern,
        out_shape=jax.ShapeDtypeStruct((nb, t_len, d), F32),
        grid=(nb, nj),
        in_specs=[pl.BlockSpec((r * TOP_K,), lambda b, j: (b * nj + j,), memory_space=pltpu.SMEM),
                  pl.BlockSpec((r * TOP_K,), lambda b, j: (jnp.minimum(b * nj + j + 1, last),),
                               memory_space=pltpu.SMEM),
                  pl.BlockSpec((None, r, LANES), lambda b, j: (b, j, 0)),
                  pl.BlockSpec((None, r, d), lambda b, j: (b, j, 0)),
                  pl.BlockSpec((MOD_ROWS, d), lambda b, j: (0, 0)),
                  pl.BlockSpec(memory_space=pl.ANY)],
        out_specs=pl.BlockSpec((None, r, d), lambda b, j: (b, j, 0)),
        scratch_shapes=[pltpu.VMEM((2, TOP_K, r, d), F32), pltpu.SemaphoreType.DMA((2,))],
        compiler_params=_cparams(("arbitrary", "arbitrary")),
        name="combine",
    )(dest_flat, dest_flat, sel, xs, mod_gt, ybuf)


def _moe_layer(xs, n_ctx, mod_shsc, mod_gt, g, w_router, b_router, w_gu, b_gu, w_down, b_down):
    nb, t_len, d = xs.shape
    ne = w_router.shape[1]
    rb = EXPERT_ROWS
    h2, sel, cnt = _router(xs, n_ctx, mod_shsc, g, w_router, b_router)
    e_idx = sel[..., 0:TOP_K].astype(jnp.int32)
    rank = sel[..., 2 * TOP_K:3 * TOP_K].astype(jnp.int32)
    counts = cnt[0, :ne].astype(jnp.int32)
    padded = (counts + rb - 1) // rb * rb
    pad_end = jnp.cumsum(padded)
    pad_start = pad_end - padded
    dest = (pad_start[e_idx] + rank).reshape(-1)
    n_blocks = nb * t_len * TOP_K // rb + ne
    block_start = jnp.arange(n_blocks, dtype=jnp.int32) * rb
    block_e = jnp.minimum(jnp.searchsorted(pad_end, block_start, side='right'), ne - 1).astype(jnp.int32)
    first = (block_start == pad_start[block_e]).astype(jnp.int32)
    n_used = (pad_end[ne - 1] // rb).astype(jnp.int32).reshape(1)
    xbuf = _dispatch(h2, dest, n_blocks * rb)
    ybuf = _experts(xbuf, block_e, first, n_used, w_gu, b_gu, w_down, b_down)
    return _combine(ybuf, dest, sel, xs, n_ctx, mod_gt)


def _qkv_kernel(x_ref, mod_ref, g_ref, w_ref, q_ref, k_ref, v_ref, *, n_ctx_tiles, nb, d):
    b = pl.program_id(0)
    j = pl.program_id(1)
    row = jnp.where(j < n_ctx_tiles, nb, b)
    mrow = mod_ref[pl.ds(row, 1), :]
    h = (_rms(x_ref[...]) * g_ref[...]) * (1.0 + mrow[:, d:2 * d]) + mrow[:, 0:d]
    qkv = jnp.dot(h.astype(BF16), w_ref[...], preferred_element_type=F32)
    for hd in range(d // DA_VDIM):
        lo = hd * DA_VDIM
        q_ref[hd] = qkv[:, lo:lo + DA_VDIM].astype(BF16)
        k_ref[hd] = qkv[:, d + lo:d + lo + DA_VDIM].astype(BF16)
        v_ref[hd] = qkv[:, 2 * d + lo:2 * d + lo + DA_VDIM].astype(BF16)


def _qkv(xs, n_ctx, mod_shsc, g, w_qkv):
    nb, t_len, d = xs.shape
    r = ROW_TILE
    nh = d // DA_VDIM
    kern = functools.partial(_qkv_kernel, n_ctx_tiles=n_ctx // r, nb=nb, d=d)
    hspec = pl.BlockSpec((None, nh, r, DA_VDIM), lambda b, j: (b, 0, j, 0))
    return pl.pallas_call(
        kern,
        out_shape=[jax.ShapeDtypeStruct((nb, nh, t_len, DA_VDIM), BF16)] * 3,
        grid=(nb, t_len // r),
        in_specs=[pl.BlockSpec((None, r, d), lambda b, j: (b, j, 0)),
                  pl.BlockSpec((MOD_ROWS, 2 * d), lambda b, j: (0, 0)),
                  pl.BlockSpec((1, d), lambda b, j: (0, 0)),
                  pl.BlockSpec((d, 3 * d), lambda b, j: (0, 0))],
        out_specs=[hspec, hspec, hspec],
        compiler_params=_cparams(("parallel", "parallel")),
        name="qkv",
    )(xs, mod_shsc, g.reshape(1, d), w_qkv.astype(BF16))


def _attn_kernel(q_ref, k_ref, v_ref, qg_ref, kg_ref, cosq_ref, sinq_ref, cosk_ref, sink_ref, lam_ref,
                 sg_ref, o_ref, kn_ref, *, n_ctx, lambda_init):
    qi = pl.program_id(2)
    n_tot = k_ref.shape[0]
    tq = q_ref.shape[0]
    dh = DA_HEAD_DIM
    lane = lax.broadcasted_iota(jnp.int32, (1, DA_VDIM), 1)
    comp0 = lane < dh
    low_half = lax.rem(lane, dh) < dh // 2

    def head_norm(xv, gain):
        x2 = xv * xv
        s0 = jnp.sum(jnp.where(comp0, x2, 0.0), axis=-1, keepdims=True)
        s1 = jnp.sum(jnp.where(comp0, 0.0, x2), axis=-1, keepdims=True)
        ms = jnp.where(comp0, s0, s1) * (1.0 / dh)
        return xv * lax.rsqrt(ms + NORM_EPS) * gain

    def rope(xv, cos, sin_signed):
        partner = jnp.where(low_half, pltpu.roll(xv, DA_VDIM - dh // 2, axis=1),
                            pltpu.roll(xv, dh // 2, axis=1))
        return xv * cos + partner * sin_signed

    @pl.when(qi == 0)
    def _():
        kg = kg_ref[...]
        kn_ref[0:n_ctx, :] = head_norm(k_ref[0:n_ctx, :].astype(F32), kg).astype(BF16)
        for c0 in range(n_ctx, n_tot, tq):
            kk = head_norm(k_ref[c0:c0 + tq, :].astype(F32), kg)
            kn_ref[c0:c0 + tq, :] = rope(kk, cosk_ref[c0 - n_ctx:c0 - n_ctx + tq, :],
                                         sink_ref[c0 - n_ctx:c0 - n_ctx + tq, :]).astype(BF16)

    qn = rope(head_norm(q_ref[...].astype(F32), qg_ref[...]), cosq_ref[...], sinq_ref[...]) * (dh ** -0.5)
    q0 = jnp.where(comp0, qn, 0.0).astype(BF16)
    q1 = jnp.where(comp0, 0.0, qn).astype(BF16)
    kn = kn_ref[...]
    dims = (((1,), (1,)), ((), ()))
    s0 = lax.dot_general(q0, kn, dims, preferred_element_type=F32)
    s1 = lax.dot_general(q1, kn, dims, preferred_element_type=F32)
    e0 = jnp.exp(s0 - jnp.max(s0, axis=-1, keepdims=True))
    e1 = jnp.exp(s1 - jnp.max(s1, axis=-1, keepdims=True))
    lam_v = lam_ref[...]
    lam = (jnp.exp(jnp.sum(lam_v[0:1] * lam_v[1:2], axis=-1, keepdims=True))
           - jnp.exp(jnp.sum(lam_v[2:3] * lam_v[3:4], axis=-1, keepdims=True)) + lambda_init)
    a = e0 * (1.0 / jnp.sum(e0, axis=-1, keepdims=True)) - lam * (e1 * (1.0 / jnp.sum(e1, axis=-1, keepdims=True)))
    o = jnp.dot(a.astype(BF16), v_ref[...], preferred_element_type=F32)
    o_ref[...] = (_rms(o) * sg_ref[...] * (1.0 - lambda_init)).astype(BF16)


def _rope_tables(n_tokens):
    rows = n_tokens // GRID_W
    row = jnp.repeat(jnp.arange(rows, dtype=F32), GRID_W)
    col = jnp.tile(jnp.arange(GRID_W, dtype=F32), rows)
    n_freq = DA_HEAD_DIM // 4
    inv_freq = jnp.exp(-math.log(ROPE_BASE) * jnp.arange(n_freq, dtype=F32) / n_freq)
    ang = jnp.concatenate([row[:, None] * inv_freq, col[:, None] * inv_freq], axis=-1)
    cos, sin = jnp.cos(ang), jnp.sin(ang)
    return jnp.tile(cos, (1, 4)), jnp.tile(jnp.concatenate([-sin, sin], axis=-1), (1, 2))


def _attention(q, k, v, n_ctx, q_gain, k_gain, lam_vecs, sub_gain, lambda_init):
    nb, nh, n_tot, dv = k.shape
    n_lat = n_tot - n_ctx
    tq = Q_TILE
    qoff = n_ctx // tq
    cos_t, sin_t = _rope_tables(n_lat)
    gain2 = lambda gv: jnp.tile(gv, 2).reshape(1, dv)
    lam_pad = jnp.zeros((SUBLANES, dv), F32).at[0:4, 0:DA_HEAD_DIM].set(lam_vecs)
    kern = functools.partial(_attn_kernel, n_ctx=n_ctx, lambda_init=lambda_init)
    const = lambda shape: pl.BlockSpec(shape, lambda b, h, i: (0, 0))
    return pl.pallas_call(
        kern,
        out_shape=jax.ShapeDtypeStruct((nb, n_lat, nh * dv), BF16),
        grid=(nb, nh, n_lat // tq),
        in_specs=[pl.BlockSpec((None, None, tq, dv), lambda b, h, i: (b, h, i + qoff, 0)),
                  pl.BlockSpec((None, None, n_tot, dv), lambda b, h, i: (b, h, 0, 0)),
                  pl.BlockSpec((None, None, n_tot, dv), lambda b, h, i: (b, h, 0, 0)),
                  const((1, dv)), const((1, dv)),
                  pl.BlockSpec((tq, dv), lambda b, h, i: (i, 0)),
                  pl.BlockSpec((tq, dv), lambda b, h, i: (i, 0)),
                  const((n_lat, dv)), const((n_lat, dv)),
                  const((SUBLANES, dv)), const((1, dv))],
        out_specs=pl.BlockSpec((None, tq, dv), lambda b, h, i: (b, i, h)),
        scratch_shapes=[pltpu.VMEM((n_tot, dv), BF16)],
        compiler_params=_cparams(("parallel", "parallel", "arbitrary")),
        name="attention",
    )(q, k, v, gain2(q_gain), gain2(k_gain), cos_t, sin_t, cos_t, sin_t, lam_pad, sub_gain.reshape(1, dv))


def _oproj_kernel(o_ref, w_ref, x_ref, mod_ref, out_ref):
    b = pl.program_id(0)
    y = jnp.dot(o_ref[...], w_ref[...], preferred_element_type=F32)
    out_ref[...] = x_ref[...] + mod_ref[pl.ds(b, 1), :] * y


def _oproj(o, w_o, xs, n_ctx, mod_gt):
    nb, n_lat, d = o.shape
    r = ROW_TILE
    off = n_ctx // r
    return pl.pallas_call(
        _oproj_kernel,
        out_shape=jax.ShapeDtypeStruct((nb, n_lat, d), F32),
        grid=(nb, n_lat // r),
        in_specs=[pl.BlockSpec((None, r, d), lambda b, j: (b, j, 0)),
                  pl.BlockSpec((d, d), lambda b, j: (0, 0)),
                  pl.BlockSpec((None, r, d), lambda b, j: (b, j + off, 0)),
                  pl.BlockSpec((MOD_ROWS, d), lambda b, j: (0, 0))],
        out_specs=pl.BlockSpec((None, r, d), lambda b, j: (b, j, 0)),
        compiler_params=_cparams(("parallel", "parallel")),
        name="oproj",
    )(o, w_o.astype(BF16), xs, mod_gt)


def _attn_layer(xs, n_ctx, mod, g_mix, w_qkv, w_o, q_gain, k_gain, lq1, lk1, lq2, lk2, sub_gain, lambda_init):
    d = xs.shape[-1]
    q, k, v = _qkv(xs, n_ctx, mod[:, 0:2 * d], g_mix, w_qkv)
    o = _attention(q, k, v, n_ctx, q_gain, k_gain, jnp.stack([lq1, lk1, lq2, lk2]), sub_gain, lambda_init)
    return _oproj(o, w_o, xs, n_ctx, mod[:, 2 * d:3 * d])


def kernel(x, c, ctx, c_ctx, w_ada, b_ada, g_mix, g_ffn, s5_a_re, s5_a_im, s5_log_dt, s5_b_re, s5_b_im, s5_c_re, s5_c_im, s5_d, s5_w_glu, s5_b_glu, da_w_qkv, da_w_o, da_q_gain, da_k_gain, da_lam_q1, da_lam_k1, da_lam_q2, da_lam_k2, da_sub_gain, moe_w_router, moe_b_router, moe_w_gu, moe_b_gu, moe_w_down, moe_b_down):
    nb, n_lat, d = x.shape
    n_ctx = ctx.shape[1]
    depth = w_ada.shape[0]
    assert depth == 2 and nb % SUBLANES == 0 and nb + 1 <= MOD_ROWS
    cc = jnp.concatenate([c, c_ctx[None], jnp.zeros((MOD_ROWS - nb - 1, d), F32)], axis=0)
    mod = _ada(cc, w_ada, b_ada)
    xs = _s5_layer(ctx, x, mod[0], g_mix[0], s5_a_re[0], s5_a_im[0], s5_log_dt[0], s5_b_re[0], s5_b_im[0],
                   s5_c_re[0], s5_c_im[0], s5_d[0], s5_w_glu[0], s5_b_glu[0])
    xs = _moe_layer(xs, n_ctx, mod[0][:, 3 * d:5 * d], mod[0][:, 5 * d:6 * d], g_ffn[0],
                    moe_w_router[0], moe_b_router[0], moe_w_gu[0], moe_b_gu[0], moe_w_down[0], moe_b_down[0])
    lambda_init = 0.8 - 0.6 * math.exp(-0.3 * 1)
    x1 = _attn_layer(xs, n_ctx, mod[1], g_mix[1], da_w_qkv[0], da_w_o[0], da_q_gain[0], da_k_gain[0],
                     da_lam_q1[0], da_lam_k1[0], da_lam_q2[0], da_lam_k2[0], da_sub_gain[0], lambda_init)
    return _moe_layer(x1, 0, mod[1][:, 3 * d:5 * d], mod[1][:, 5 * d:6 * d], g_ffn[1],
                      moe_w_router[1], moe_b_router[1], moe_w_gu[1], moe_b_gu[1], moe_w_down[1], moe_b_down[1])
```

```python
import functools
import math

import jax
import jax.numpy as jnp
from jax import lax
from jax.experimental import pallas as pl
from jax.experimental.pallas import tpu as pltpu

NORM_EPS = 1e-6
N_MIXERS = 2
S5_GROUP = 16
S5_STATE = 64
DA_HEAD_DIM = 64
DA_VDIM = 2 * DA_HEAD_DIM
GRID_W = 64
ROPE_BASE = 10000.0
N_EXPERTS = 32
TOP_K = 4
SWIGLU_LIMIT = 7.0
SWIGLU_ALPHA = 1.702

LANES = 128
SUBLANES = 8
VMEM_LIMIT_BYTES = 56 * 1024 * 1024

S5_SLAB_GROUPS = LANES // S5_GROUP
S5_SLAB_STATE = S5_SLAB_GROUPS * S5_STATE
S5_CHUNK = 64
S5_UNROLL = 4
TM_TILE = 32
ROW_TILE = 256
ROUTER_TILE = 512
EXPERT_ROWS = 256
COMBINE_TILE = 256
Q_TILE = 256
MOD_ROWS = 24

F32 = jnp.float32
BF16 = jnp.bfloat16


def _cparams(sem, vmem=VMEM_LIMIT_BYTES):
    return pltpu.CompilerParams(dimension_semantics=sem, vmem_limit_bytes=vmem)


def _rms(x):
    return x * lax.rsqrt(jnp.mean(x * x, axis=-1, keepdims=True) + NORM_EPS)


def _ada_kernel(c_ref, w_ref, b_ref, o_ref):
    cv = c_ref[...]
    s = cv * jax.nn.sigmoid(cv)
    o_ref[...] = jnp.dot(s, w_ref[...], precision=lax.Precision.HIGHEST,
                         preferred_element_type=F32) + b_ref[...]


def _ada(cc, w_ada, b_ada):
    depth, d, d6 = w_ada.shape
    nj = d6 // d
    return pl.pallas_call(
        _ada_kernel,
        out_shape=jax.ShapeDtypeStruct((depth, MOD_ROWS, d6), F32),
        grid=(depth, nj),
        in_specs=[pl.BlockSpec((MOD_ROWS, d), lambda i, j: (0, 0)),
                  pl.BlockSpec((None, d, d), lambda i, j: (i, 0, j)),
                  pl.BlockSpec((None, 1, d), lambda i, j: (i, 0, j))],
        out_specs=pl.BlockSpec((None, MOD_ROWS, d), lambda i, j: (i, 0, j)),
        compiler_params=_cparams(("parallel", "parallel")),
        name="ada",
    )(cc, w_ada, b_ada.reshape(depth, 1, d6))


def _prenorm_tm_kernel(ctx_ref, x_ref, mod_ref, g_ref, o_ref, *, n_ctx_tiles, nb, d):
    i = pl.program_id(0)
    tt = o_ref.shape[1] // nb
    g = g_ref[...]

    def emit(src_ref, mod_row):
        for b in range(nb):
            r = mod_row(b)
            sh = mod_ref[r:r + 1, 0:d]
            sc = mod_ref[r:r + 1, d:2 * d]
            h = (_rms(src_ref[b]) * g) * (1.0 + sc) + sh
            for s in range(d // LANES):
                o_ref[s, pl.ds(b, tt, stride=nb), :] = h[:, s * LANES:(s + 1) * LANES]

    @pl.when(i < n_ctx_tiles)
    def _():
        emit(ctx_ref, lambda b: nb)

    @pl.when(i >= n_ctx_tiles)
    def _():
        emit(x_ref, lambda b: b)


def _prenorm_tm(ctx, x, mod, g):
    nb, n_ctx, d = ctx.shape
    n_lat = x.shape[1]
    tt = TM_TILE
    nct, nlt = n_ctx // tt, n_lat // tt
    kern = functools.partial(_prenorm_tm_kernel, n_ctx_tiles=nct, nb=nb, d=d)
    return pl.pallas_call(
        kern,
        out_shape=jax.ShapeDtypeStruct((d // LANES, (n_ctx + n_lat) * nb, LANES), F32),
        grid=(nct + nlt,),
        in_specs=[pl.BlockSpec((nb, tt, d), lambda i: (0, jnp.minimum(i, nct - 1), 0)),
                  pl.BlockSpec((nb, tt, d), lambda i: (0, jnp.maximum(i - nct, 0), 0)),
                  pl.BlockSpec((MOD_ROWS, 2 * d), lambda i: (0, 0)),
                  pl.BlockSpec((1, d), lambda i: (0, 0))],
        out_specs=pl.BlockSpec((d // LANES, tt * nb, LANES), lambda i: (0, i, 0)),
        compiler_params=_cparams(("parallel",)),
        name="prenorm_tm",
    )(ctx, x, mod, g)


def _s5_disc_kernel(ar_ref, ai_ref, ldt_ref, br_ref, bi_ref, lr_ref, li_ref, bbr_ref, bbi_ref):
    ar, ai = ar_ref[...], ai_ref[...]
    dt = jnp.exp(ldt_ref[...])
    mag = jnp.exp(ar * dt)
    lr = mag * jnp.cos(ai * dt)
    li = mag * jnp.sin(ai * dt)
    nr, ni = lr - 1.0, li
    den = ar * ar + ai * ai
    qr = (nr * ar + ni * ai) / den
    qi = (ni * ar - nr * ai) / den
    br, bi = br_ref[...], bi_ref[...]
    lr_ref[...] = lr
    li_ref[...] = li
    bbr_ref[...] = qr * br - qi * bi
    bbi_ref[...] = qr * bi + qi * br


def _s5_disc(a_re, a_im, log_dt, b_re, b_im):
    nd, g, p = a_re.shape
    cg = b_re.shape[-1]
    gp = g * p
    col = lambda a: a.reshape(nd, gp, 1)
    ldt = jnp.broadcast_to(log_dt[:, :, None], (nd, g, p))
    spec1 = pl.BlockSpec((None, gp, 1), lambda i: (i, 0, 0))
    specb = pl.BlockSpec((None, gp, cg), lambda i: (i, 0, 0))
    lr, li, bbr, bbi = pl.pallas_call(
        _s5_disc_kernel,
        out_shape=[jax.ShapeDtypeStruct((nd, gp, 1), F32)] * 2 + [jax.ShapeDtypeStruct((nd, gp, cg), F32)] * 2,
        grid=(nd,),
        in_specs=[spec1, spec1, spec1, specb, specb],
        out_specs=[spec1, spec1, specb, specb],
        compiler_params=_cparams(("parallel",)),
        name="s5_disc",
    )(col(a_re), col(a_im), col(ldt), b_re.reshape(nd, gp, cg), b_im.reshape(nd, gp, cg))
    return (lr.reshape(nd, g, p), li.reshape(nd, g, p),
            bbr.reshape(nd, g, p, cg), bbi.reshape(nd, g, p, cg))


def _s5_slab_params(lam_re, lam_im, bb_re, bb_im, c_re, c_im):
    nd, g, p = lam_re.shape
    cg = bb_re.shape[-1]
    ns, sg = g // S5_SLAB_GROUPS, S5_SLAB_GROUPS
    eye = jnp.eye(sg, dtype=F32)

    def b_mat(bb):
        bb = bb.reshape(nd, ns, sg, p, cg)
        return jnp.einsum('dsjpc,jk->dsjckp', bb, eye).reshape(nd, ns, sg * cg, sg * p)

    def c_mat(cm):
        cm = cm.reshape(nd, ns, sg, cg, p)
        return jnp.einsum('dsjcp,jk->dskpjc', cm, eye).reshape(nd, ns, sg * p, sg * cg)

    bm = jnp.concatenate([b_mat(bb_re), b_mat(bb_im)], axis=-1).astype(BF16)
    lam = lambda a: a.reshape(nd, ns, 1, sg * p)
    return bm, c_mat(c_re).astype(BF16), c_mat(c_im).astype(BF16), lam(lam_re), lam(lam_im)


def _s5_scan_kernel(u_ref, bm_ref, cr_ref, ci_ref, lr_ref, li_ref, y_ref, hs_ref, st_ref):
    d = pl.program_id(0)
    c = pl.program_id(3)
    t_len = u_ref.shape[0]
    ns = S5_SLAB_STATE

    @pl.when(c == 0)
    def _():
        st_ref[...] = jnp.zeros_like(st_ref)

    u = u_ref[...].reshape(t_len * SUBLANES, LANES).astype(BF16)
    hs_ref[...] = jnp.dot(u, bm_ref[...], preferred_element_type=F32)

    lr = jnp.broadcast_to(lr_ref[...], (SUBLANES, ns))
    li = jnp.broadcast_to(li_ref[...], (SUBLANES, ns))

    def steps(j, carry):
        hr, hi = carry
        for q in range(S5_UNROLL):
            s = j * S5_UNROLL + q
            t = jnp.where(d == 0, s, t_len - 1 - s)
            row = pl.multiple_of(t * SUBLANES, SUBLANES)
            br = hs_ref[pl.ds(row, SUBLANES), 0:ns]
            bi = hs_ref[pl.ds(row, SUBLANES), ns:2 * ns]
            nr = lr * hr - li * hi + br
            ni = lr * hi + li * hr + bi
            hs_ref[pl.ds(row, SUBLANES), 0:ns] = nr
            hs_ref[pl.ds(row, SUBLANES), ns:2 * ns] = ni
            hr, hi = nr, ni
        return hr, hi

    hr, hi = lax.fori_loop(0, t_len // S5_UNROLL, steps, (st_ref[0], st_ref[1]))
    st_ref[0] = hr
    st_ref[1] = hi

    y = (jnp.dot(hs_ref[:, 0:ns].astype(BF16), cr_ref[...], preferred_element_type=F32)
         - jnp.dot(hs_ref[:, ns:2 * ns].astype(BF16), ci_ref[...], preferred_element_type=F32))
    y_ref[...] = y.reshape(t_len, SUBLANES, LANES)


def _s5_scan(u_t, bm, cr, ci, lam_r, lam_i, nb, n_ctx, n_tot):
    nslab = u_t.shape[0]
    nbo = nb // SUBLANES
    t_len = S5_CHUNK
    ncc, nc = n_ctx // t_len, n_tot // t_len
    u5 = u_t.reshape(nslab, n_tot, nbo, SUBLANES, LANES)

    def chunk(d, c):
        back = jnp.where(c < ncc, ncc - 1 - c, ncc + nc - 1 - c)
        return jnp.where(d == 0, c, back)

    pspec = lambda r, k: pl.BlockSpec((None, None, r, k), lambda d, bo, s, c: (d, s, 0, 0))
    return pl.pallas_call(
        _s5_scan_kernel,
        out_shape=jax.ShapeDtypeStruct((2, nslab, n_tot, nbo, SUBLANES, LANES), F32),
        grid=(2, nbo, nslab, nc),
        in_specs=[pl.BlockSpec((None, t_len, None, SUBLANES, LANES),
                               lambda d, bo, s, c: (s, chunk(d, c), bo, 0, 0)),
                  pspec(LANES, 2 * S5_SLAB_STATE), pspec(S5_SLAB_STATE, LANES), pspec(S5_SLAB_STATE, LANES),
                  pspec(1, S5_SLAB_STATE), pspec(1, S5_SLAB_STATE)],
        out_specs=pl.BlockSpec((None, None, t_len, None, SUBLANES, LANES),
                               lambda d, bo, s, c: (d, s, chunk(d, c), bo, 0, 0)),
        scratch_shapes=[pltpu.VMEM((t_len * SUBLANES, 2 * S5_SLAB_STATE), F32),
                        pltpu.VMEM((2, SUBLANES, S5_SLAB_STATE), F32)],
        compiler_params=_cparams(("parallel", "parallel", "parallel", "arbitrary")),
        name="s5_scan",
    )(u5, bm, cr, ci, lam_r, lam_i)


def _s5_out_kernel(y_ref, u_ref, dsk_ref, w_ref, bg_ref, ctx_ref, x_ref, mod_ref, o_ref, tr_ref,
                   *, n_ctx_tiles, nb, d):
    i = pl.program_id(0)
    nslab = d // LANES
    rows = u_ref.shape[1]
    tt = rows // nb
    y = (y_ref[0] + y_ref[1]).reshape(nslab, rows, LANES)
    v = jax.nn.gelu(y + dsk_ref[...] * u_ref[...])
    lhs = jnp.concatenate([v[s] for s in range(nslab)], axis=-1).astype(BF16)
    z = jnp.dot(lhs, w_ref[...], preferred_element_type=F32) + bg_ref[...]
    m = z[:, 0:d] * jax.nn.sigmoid(z[:, d:2 * d])
    for s in range(nslab):
        tr_ref[s] = m[:, s * LANES:(s + 1) * LANES]

    def emit(src_ref, mod_row):
        for b in range(nb):
            r = mod_row(b)
            gt = mod_ref[r:r + 1, :]
            mb = jnp.concatenate([tr_ref[s, pl.ds(b, tt, stride=nb), :] for s in range(nslab)], axis=-1)
            o_ref[b] = src_ref[b] + gt * mb

    @pl.when(i < n_ctx_tiles)
    def _():
        emit(ctx_ref, lambda b: nb)

    @pl.when(i >= n_ctx_tiles)
    def _():
        emit(x_ref, lambda b: b)


def _s5_out(y_t, u_t, d_skip, w_glu, b_glu, ctx, x, gt1):
    nb, n_ctx, d = ctx.shape
    n_lat = x.shape[1]
    n_tot = n_ctx + n_lat
    nslab, nbo = d // LANES, nb // SUBLANES
    tt = TM_TILE
    nct, nlt = n_ctx // tt, n_lat // tt
    kern = functools.partial(_s5_out_kernel, n_ctx_tiles=nct, nb=nb, d=d)
    return pl.pallas_call(
        kern,
        out_shape=jax.ShapeDtypeStruct((nb, n_tot, d), F32),
        grid=(nct + nlt,),
        in_specs=[pl.BlockSpec((2, nslab, tt, nbo, SUBLANES, LANES), lambda i: (0, 0, i, 0, 0, 0)),
                  pl.BlockSpec((nslab, tt * nb, LANES), lambda i: (0, i, 0)),
                  pl.BlockSpec((nslab, 1, LANES), lambda i: (0, 0, 0)),
                  pl.BlockSpec((d, 2 * d), lambda i: (0, 0)),
                  pl.BlockSpec((1, 2 * d), lambda i: (0, 0)),
                  pl.BlockSpec((nb, tt, d), lambda i: (0, jnp.minimum(i, nct - 1), 0)),
                  pl.BlockSpec((nb, tt, d), lambda i: (0, jnp.maximum(i - nct, 0), 0)),
                  pl.BlockSpec((MOD_ROWS, d), lambda i: (0, 0))],
        out_specs=pl.BlockSpec((nb, tt, d), lambda i: (0, i, 0)),
        scratch_shapes=[pltpu.VMEM((nslab, tt * nb, LANES), F32)],
        compiler_params=_cparams(("parallel",)),
        name="s5_out",
    )(y_t, u_t, d_skip.reshape(nslab, 1, LANES), w_glu.astype(BF16), b_glu.reshape(1, 2 * d), ctx, x, gt1)


def _s5_layer(xs_ctx, xs_lat, mod, g_mix, a_re, a_im, log_dt, b_re, b_im, c_re, c_im, d_skip, w_glu, b_glu):
    nb, n_ctx, d = xs_ctx.shape
    n_tot = n_ctx + xs_lat.shape[1]
    u_t = _prenorm_tm(xs_ctx, xs_lat, mod[:, 0:2 * d], g_mix.reshape(1, d))
    lam_re, lam_im, bb_re, bb_im = _s5_disc(a_re, a_im, log_dt, b_re, b_im)
    bm, cr, ci, lam_r, lam_i = _s5_slab_params(lam_re, lam_im, bb_re, bb_im, c_re, c_im)
    y_t = _s5_scan(u_t, bm, cr, ci, lam_r, lam_i, nb, n_ctx, n_tot)
    return _s5_out(y_t, u_t, d_skip, w_glu, b_glu, xs_ctx, xs_lat, mod[:, 2 * d:3 * d])


NEG_PAD = -1e30
NEG_MASK = -3e38


def _split_bf16(a):
    hi = a.astype(BF16)
    return hi, (a - hi.astype(F32)).astype(BF16)


def _router_kernel(x_ref, mod_ref, g_ref, w_ref, bias_ref, h_ref, sel_ref, selt_ref, meta_ref, cnt_ref,
                   carry_ref, *, n_ctx_tiles, nb, d):
    b = pl.program_id(0)
    j = pl.program_id(1)
    rows = x_ref.shape[0]

    @pl.when((b == 0) & (j == 0))
    def _():
        carry_ref[...] = jnp.zeros_like(carry_ref)

    row = jnp.where(j < n_ctx_tiles, nb, b)
    mrow = mod_ref[pl.ds(row, 1), :]
    h = (_rms(x_ref[...]) * g_ref[...]) * (1.0 + mrow[:, d:2 * d]) + mrow[:, 0:d]
    h_ref[...] = h.astype(BF16)

    hi, lo = _split_bf16(h)
    whi, wlo = _split_bf16(w_ref[...])
    dot = functools.partial(jnp.dot, preferred_element_type=F32)
    logits = dot(hi, whi) + dot(lo, whi) + dot(hi, wlo) + bias_ref[...]

    lane = lax.broadcasted_iota(jnp.int32, (rows, LANES), 1)
    work = logits
    tops, idxs, hots = [], [], []
    for _ in range(TOP_K):
        m = jnp.max(work, axis=-1, keepdims=True)
        idx = jnp.min(jnp.where(work == m, lane, LANES), axis=-1, keepdims=True)
        hot = lane == idx
        work = jnp.where(hot, NEG_MASK, work)
        tops.append(m)
        idxs.append(idx)
        hots.append(hot)
    exps = [jnp.exp(m - tops[0]) for m in tops]
    denom = exps[0] + exps[1] + exps[2] + exps[3]

    multi = jnp.zeros((rows, LANES), F32)
    for hot in hots:
        multi = multi + jnp.where(hot, 1.0, 0.0)
    r_i = lax.broadcasted_iota(jnp.int32, (rows, rows), 0)
    c_i = lax.broadcasted_iota(jnp.int32, (rows, rows), 1)
    tri = jnp.where(c_i < r_i, 1.0, 0.0).astype(BF16)
    in_tile = dot(tri, multi.astype(BF16))
    cnt_raw = jnp.sum(multi, axis=0, keepdims=True).astype(jnp.int32)
    cnt_tile = (((cnt_raw + (SUBLANES - 1)) // SUBLANES) * SUBLANES).astype(F32)
    e_r = lax.broadcasted_iota(jnp.int32, (LANES, LANES), 0)
    e_c = lax.broadcasted_iota(jnp.int32, (LANES, LANES), 1)
    below = jnp.where(e_r < e_c, 1.0, 0.0).astype(BF16)
    prefix = dot(jnp.broadcast_to(cnt_tile, (SUBLANES, LANES)).astype(BF16), below)[0:1, :]
    slot = in_tile + prefix

    sel = jnp.zeros((rows, LANES), F32)
    for k in range(TOP_K):
        pos = jnp.sum(jnp.where(hots[k], slot, 0.0), axis=-1, keepdims=True)
        sel = jnp.where(lane == k, idxs[k].astype(F32), sel)
        sel = jnp.where(lane == TOP_K + k, exps[k] / denom, sel)
        sel = jnp.where(lane == 2 * TOP_K + k, pos, sel)
    sel_ref[...] = sel
    selt_ref[...] = sel.T[2 * TOP_K:2 * TOP_K + SUBLANES, :].astype(jnp.int32)

    carry = carry_ref[0:1, :]
    sub = lax.broadcasted_iota(jnp.int32, (SUBLANES, LANES), 0)
    meta = jnp.where(sub == 0, cnt_tile, jnp.where(sub == 1, carry, jnp.where(sub == 2, prefix, 0.0)))
    meta_ref[...] = meta.astype(jnp.int32)
    carry_ref[0:1, :] = carry + cnt_tile
    cnt_ref[...] = jnp.broadcast_to(carry + cnt_tile, cnt_ref.shape).astype(jnp.int32)


def _router(xs, n_ctx, mod_shsc, g, w_router, b_router):
    nb, t_len, d = xs.shape
    r = ROW_TILE
    nj = t_len // r
    ne = w_router.shape[1]
    w_pad = jnp.zeros((d, LANES), F32).at[:, :ne].set(w_router)
    b_pad = jnp.full((1, LANES), NEG_PAD, F32).at[0, :ne].set(b_router)
    kern = functools.partial(_router_kernel, n_ctx_tiles=n_ctx // r, nb=nb, d=d)
    tile = lambda b, j: (b * nj + j, 0)
    return pl.pallas_call(
        kern,
        out_shape=[jax.ShapeDtypeStruct((nb * t_len, d), BF16),
                   jax.ShapeDtypeStruct((nb * t_len, LANES), F32),
                   jax.ShapeDtypeStruct((nb * nj * SUBLANES, r), jnp.int32),
                   jax.ShapeDtypeStruct((nb * nj, SUBLANES, LANES), jnp.int32),
                   jax.ShapeDtypeStruct((SUBLANES, LANES), jnp.int32)],
        grid=(nb, nj),
        in_specs=[pl.BlockSpec((None, r, d), lambda b, j: (b, j, 0)),
                  pl.BlockSpec((MOD_ROWS, 2 * d), lambda b, j: (0, 0)),
                  pl.BlockSpec((1, d), lambda b, j: (0, 0)),
                  pl.BlockSpec((d, LANES), lambda b, j: (0, 0)),
                  pl.BlockSpec((1, LANES), lambda b, j: (0, 0))],
        out_specs=[pl.BlockSpec((r, d), tile),
                   pl.BlockSpec((r, LANES), tile),
                   pl.BlockSpec((SUBLANES, r), tile),
                   pl.BlockSpec((None, SUBLANES, LANES), lambda b, j: (b * nj + j, 0, 0)),
                   pl.BlockSpec((SUBLANES, LANES), lambda b, j: (0, 0))],
        scratch_shapes=[pltpu.VMEM((SUBLANES, LANES), F32)],
        compiler_params=_cparams(("arbitrary", "arbitrary")),
        name="router",
    )(xs, mod_shsc, g.reshape(1, d), w_pad, b_pad)


def _plan_kernel(cnt_ref, ps_ref, be_ref, first_ref, nused_ref, *, ne, rb):
    n_blocks = be_ref.shape[0]

    def fill(i, carry):
        be_ref[i] = ne - 1
        first_ref[i] = 0
        return carry
    lax.fori_loop(0, n_blocks, fill, 0)

    def per_expert(e, acc):
        nblk = (cnt_ref[0, e] + (rb - 1)) // rb
        ps_ref[e] = acc
        b0 = acc // rb

        def mark(i, carry):
            be_ref[b0 + i] = e
            first_ref[b0 + i] = jnp.where(i == 0, 1, 0)
            return carry
        lax.fori_loop(0, nblk, mark, 0)
        return acc + nblk * rb

    total = lax.fori_loop(0, ne, per_expert, 0)

    def tail(e, carry):
        ps_ref[e] = total
        return carry
    lax.fori_loop(ne, ps_ref.shape[0], tail, 0)
    nused_ref[0] = total // rb


def _plan(cnt, ne, n_blocks):
    smem = pl.BlockSpec(memory_space=pltpu.SMEM)
    return pl.pallas_call(
        functools.partial(_plan_kernel, ne=ne, rb=EXPERT_ROWS),
        out_shape=[jax.ShapeDtypeStruct((LANES,), jnp.int32),
                   jax.ShapeDtypeStruct((n_blocks,), jnp.int32),
                   jax.ShapeDtypeStruct((n_blocks,), jnp.int32),
                   jax.ShapeDtypeStruct((1,), jnp.int32)],
        in_specs=[smem],
        out_specs=[smem, smem, smem, smem],
        name="plan",
    )(cnt)


_SEG_SIZES = tuple(p for p in (ROW_TILE >> s for s in range(ROW_TILE.bit_length())) if p >= SUBLANES)
TILE_SLOTS = ROW_TILE * TOP_K + N_EXPERTS * SUBLANES


def _segment_copies(n, src, dst, make_copy, wait=False):
    for p in _SEG_SIZES:
        take = (n & p) != 0

        @pl.when(take)
        def _():
            cp = make_copy(pl.multiple_of(src, SUBLANES), pl.multiple_of(dst, SUBLANES), p)
            cp.wait() if wait else cp.start()
        step = jnp.where(take, p, 0)
        src = src + step
        dst = dst + step


def _dispatch_kernel(meta_ref, ps_ref, cnt_ref, h_ref, selt_ref, xbuf_ref, xs_ref, zero_ref, sem, zsem,
                     *, ne, rb):
    rows = h_ref.shape[0]
    slots = xs_ref.shape[0]

    @pl.when(pl.program_id(0) == 0)
    def _():
        zero_ref[...] = jnp.zeros_like(zero_ref)

        def pad_copy(_, dst, p):
            return pltpu.make_async_copy(zero_ref.at[pl.ds(0, p), :], xbuf_ref.at[pl.ds(dst, p), :], zsem)

        for wait in (False, True):
            def pad_rows(e, carry, wait=wait):
                cnt = cnt_ref[0, e]
                _segment_copies((-cnt) & (rb - 1), 0, ps_ref[e] + cnt, pad_copy, wait)
                return carry
            lax.fori_loop(0, ne, pad_rows, 0)

    pos = selt_ref[0:TOP_K, :]
    s_iota = lax.broadcasted_iota(jnp.int32, (slots, rows), 0)
    perm = jnp.zeros((slots, rows), F32)
    for k in range(TOP_K):
        perm = perm + jnp.where(s_iota == pos[k:k + 1, :], 1.0, 0.0)
    xs_ref[...] = jnp.dot(perm.astype(BF16), h_ref[...], preferred_element_type=F32)

    def seg_copy(src, dst, p):
        return pltpu.make_async_copy(xs_ref.at[pl.ds(src, p), :], xbuf_ref.at[pl.ds(dst, p), :], sem)

    for wait in (False, True):
        def per_expert(e, carry, wait=wait):
            _segment_copies(meta_ref[0, e], meta_ref[2, e], ps_ref[e] + meta_ref[1, e], seg_copy, wait)
            return carry
        lax.fori_loop(0, ne, per_expert, 0)


def _dispatch(h2, selt, meta, ps, cnt, ne, n_buf):
    n_tok, d = h2.shape
    r = ROW_TILE
    smem = pl.BlockSpec(memory_space=pltpu.SMEM)
    return pl.pallas_call(
        functools.partial(_dispatch_kernel, ne=ne, rb=EXPERT_ROWS),
        out_shape=jax.ShapeDtypeStruct((n_buf, d), F32),
        grid=(n_tok // r,),
        in_specs=[pl.BlockSpec((None, SUBLANES, LANES), lambda i: (i, 0, 0), memory_space=pltpu.SMEM),
                  smem, smem,
                  pl.BlockSpec((r, d), lambda i: (i, 0)),
                  pl.BlockSpec((SUBLANES, r), lambda i: (i, 0))],
        out_specs=pl.BlockSpec(memory_space=pl.ANY),
        scratch_shapes=[pltpu.VMEM((TILE_SLOTS, d), F32), pltpu.VMEM((EXPERT_ROWS, d), F32),
                        pltpu.SemaphoreType.DMA(()), pltpu.SemaphoreType.DMA(())],
        compiler_params=_cparams(("arbitrary",)),
        name="dispatch",
    )(meta, ps, cnt, h2, selt)


def _expert_kernel(be_ref, first_ref, nused_ref, x_ref, wgu_ref, bgu_ref, wd_ref, bd_ref, y_ref,
                   wgu_bf, wd_bf):
    i = pl.program_id(0)
    dff = wd_ref.shape[0]

    @pl.when(i < nused_ref[0])
    def _():
        @pl.when(first_ref[i] == 1)
        def _():
            wgu_bf[...] = wgu_ref[...].astype(BF16)
            wd_bf[...] = wd_ref[...].astype(BF16)

        gu = jnp.dot(x_ref[...].astype(BF16), wgu_bf[...], preferred_element_type=F32) + bgu_ref[...]
        gate = jnp.minimum(gu[:, 0:dff], SWIGLU_LIMIT)
        up = jnp.clip(gu[:, dff:2 * dff], -SWIGLU_LIMIT, SWIGLU_LIMIT)
        act = (up + 1.0) * gate * jax.nn.sigmoid(SWIGLU_ALPHA * gate)
        y_ref[...] = jnp.dot(act.astype(BF16), wd_bf[...], preferred_element_type=F32) + bd_ref[...]

    @pl.when(i >= nused_ref[0])
    def _():
        y_ref[...] = jnp.zeros_like(y_ref)


def _experts(xbuf, block_e, first, n_used, layer, w_gu, b_gu, w_down, b_down):
    n_buf, d = xbuf.shape
    depth, ne, _, dff2 = w_gu.shape
    dff = dff2 // 2
    rb = EXPERT_ROWS
    blk = lambda i, nu: jnp.minimum(i, nu[0] - 1)
    wmap = lambda i, be, fi, nu: (layer, be[blk(i, nu)], 0, 0)
    grid_spec = pltpu.PrefetchScalarGridSpec(
        num_scalar_prefetch=3,
        grid=(n_buf // rb,),
        in_specs=[pl.BlockSpec((rb, d), lambda i, be, fi, nu: (blk(i, nu), 0)),
                  pl.BlockSpec((None, None, d, dff2), wmap),
                  pl.BlockSpec((None, None, 1, dff2), wmap),
                  pl.BlockSpec((None, None, dff, d), wmap),
                  pl.BlockSpec((None, None, 1, d), wmap)],
        out_specs=pl.BlockSpec((rb, d), lambda i, be, fi, nu: (i, 0)),
        scratch_shapes=[pltpu.VMEM((d, dff2), BF16), pltpu.VMEM((dff, d), BF16)])
    return pl.pallas_call(
        _expert_kernel,
        out_shape=jax.ShapeDtypeStruct((n_buf, d), F32),
        grid_spec=grid_spec,
        compiler_params=_cparams(("arbitrary",)),
        name="experts",
    )(block_e, first, n_used, xbuf, w_gu, b_gu.reshape(depth, ne, 1, dff2), w_down,
      b_down.reshape(depth, ne, 1, d))


def _combine_kernel(meta_ref, metan_ref, ps_ref, sel_ref, x_ref, mod_ref, ybuf_ref, o_ref, ys_ref, sem,
                    *, n_ctx_tiles, nb, ne):
    b = pl.program_id(0)
    j = pl.program_id(1)
    nj = pl.num_programs(1)
    step = b * nj + j
    n_steps = pl.num_programs(0) * nj
    slot = lax.rem(step, 2)
    rows = x_ref.shape[0]
    slots = ys_ref.shape[1]

    def segments(mref, to_slot, wait):
        def seg_copy(src, dst, p):
            return pltpu.make_async_copy(ybuf_ref.at[pl.ds(dst, p), :], ys_ref.at[to_slot, pl.ds(src, p), :],
                                         sem.at[to_slot])

        def per_expert(e, carry):
            _segment_copies(mref[0, e], mref[2, e], ps_ref[e] + mref[1, e], seg_copy, wait)
            return carry
        lax.fori_loop(0, ne, per_expert, 0)

    @pl.when(step == 0)
    def _():
        ys_ref[...] = jnp.zeros_like(ys_ref)
        segments(meta_ref, 0, False)

    @pl.when(step + 1 < n_steps)
    def _():
        segments(metan_ref, 1 - slot, False)

    segments(meta_ref, slot, True)

    sel = sel_ref[...]
    l_iota = lax.broadcasted_iota(jnp.int32, (rows, slots), 1).astype(F32)
    w = jnp.zeros((rows, slots), F32)
    for k in range(TOP_K):
        w = w + jnp.where(l_iota == sel[:, 2 * TOP_K + k:2 * TOP_K + k + 1], sel[:, TOP_K + k:TOP_K + k + 1], 0.0)
    whi, wlo = _split_bf16(w)
    ys = ys_ref[slot].astype(BF16)
    acc = jnp.dot(whi, ys, preferred_element_type=F32) + jnp.dot(wlo, ys, preferred_element_type=F32)
    row = jnp.where(j < n_ctx_tiles, nb, b)
    o_ref[...] = x_ref[...] + mod_ref[pl.ds(row, 1), :] * acc


def _combine(ybuf, meta, ps, sel, xs, n_ctx, mod_gt, ne):
    nb, t_len, d = xs.shape
    r = ROW_TILE
    nj = t_len // r
    kern = functools.partial(_combine_kernel, n_ctx_tiles=n_ctx // r, nb=nb, ne=ne)
    last = nb * nj - 1
    mspec = lambda f: pl.BlockSpec((None, SUBLANES, LANES), f, memory_space=pltpu.SMEM)
    return pl.pallas_call(
        kern,
        out_shape=jax.ShapeDtypeStruct((nb, t_len, d), F32),
        grid=(nb, nj),
        in_specs=[mspec(lambda b, j: (b * nj + j, 0, 0)),
                  mspec(lambda b, j: (jnp.minimum(b * nj + j + 1, last), 0, 0)),
                  pl.BlockSpec(memory_space=pltpu.SMEM),
                  pl.BlockSpec((r, LANES), lambda b, j: (b * nj + j, 0)),
                  pl.BlockSpec((None, r, d), lambda b, j: (b, j, 0)),
                  pl.BlockSpec((MOD_ROWS, d), lambda b, j: (0, 0)),
                  pl.BlockSpec(memory_space=pl.ANY)],
        out_specs=pl.BlockSpec((None, r, d), lambda b, j: (b, j, 0)),
        scratch_shapes=[pltpu.VMEM((2, TILE_SLOTS, d), F32), pltpu.SemaphoreType.DMA((2,))],
        compiler_params=_cparams(("arbitrary", "arbitrary")),
        name="combine",
    )(meta, meta, ps, sel, xs, mod_gt, ybuf)


def _moe_layer(xs, n_ctx, mod_shsc, mod_gt, g, layer, w_router, b_router, w_gu, b_gu, w_down, b_down):
    nb, t_len, d = xs.shape
    ne = w_router.shape[-1]
    n_tiles = nb * t_len // ROW_TILE
    n_blocks = n_tiles * TILE_SLOTS // EXPERT_ROWS + ne
    h2, sel, selt, meta, cnt = _router(xs, n_ctx, mod_shsc, g, w_router[layer], b_router[layer])
    ps, block_e, first, n_used = _plan(cnt, ne, n_blocks)
    xbuf = _dispatch(h2, selt, meta, ps, cnt, ne, n_blocks * EXPERT_ROWS)
    ybuf = _experts(xbuf, block_e, first, n_used, layer, w_gu, b_gu, w_down, b_down)
    return _combine(ybuf, meta, ps, sel, xs, n_ctx, mod_gt, ne)


def _qkv_kernel(x_ref, mod_ref, g_ref, w_ref, q_ref, k_ref, v_ref, *, n_ctx_tiles, nb, d):
    b = pl.program_id(0)
    j = pl.program_id(1)
    row = jnp.where(j < n_ctx_tiles, nb, b)
    mrow = mod_ref[pl.ds(row, 1), :]
    h = (_rms(x_ref[...]) * g_ref[...]) * (1.0 + mrow[:, d:2 * d]) + mrow[:, 0:d]
    qkv = jnp.dot(h.astype(BF16), w_ref[...], preferred_element_type=F32)
    for hd in range(d // DA_VDIM):
        lo = hd * DA_VDIM
        q_ref[hd] = qkv[:, lo:lo + DA_VDIM].astype(BF16)
        k_ref[hd] = qkv[:, d + lo:d + lo + DA_VDIM].astype(BF16)
        v_ref[hd] = qkv[:, 2 * d + lo:2 * d + lo + DA_VDIM].astype(BF16)


def _qkv(xs, n_ctx, mod_shsc, g, w_qkv):
    nb, t_len, d = xs.shape
    r = ROW_TILE
    nh = d // DA_VDIM
    kern = functools.partial(_qkv_kernel, n_ctx_tiles=n_ctx // r, nb=nb, d=d)
    hspec = pl.BlockSpec((None, nh, r, DA_VDIM), lambda b, j: (b, 0, j, 0))
    return pl.pallas_call(
        kern,
        out_shape=[jax.ShapeDtypeStruct((nb, nh, t_len, DA_VDIM), BF16)] * 3,
        grid=(nb, t_len // r),
        in_specs=[pl.BlockSpec((None, r, d), lambda b, j: (b, j, 0)),
                  pl.BlockSpec((MOD_ROWS, 2 * d), lambda b, j: (0, 0)),
                  pl.BlockSpec((1, d), lambda b, j: (0, 0)),
                  pl.BlockSpec((d, 3 * d), lambda b, j: (0, 0))],
        out_specs=[hspec, hspec, hspec],
        compiler_params=_cparams(("parallel", "parallel")),
        name="qkv",
    )(xs, mod_shsc, g.reshape(1, d), w_qkv.astype(BF16))


def _attn_kernel(q_ref, k_ref, v_ref, qg_ref, kg_ref, cosq_ref, sinq_ref, cosk_ref, sink_ref, lam_ref,
                 sg_ref, o_ref, kn_ref, *, n_ctx, lambda_init):
    qi = pl.program_id(2)
    n_tot = k_ref.shape[0]
    tq = q_ref.shape[0]
    dh = DA_HEAD_DIM
    lane = lax.broadcasted_iota(jnp.int32, (1, DA_VDIM), 1)
    comp0 = lane < dh
    low_half = lax.rem(lane, dh) < dh // 2

    def head_norm(xv, gain):
        x2 = xv * xv
        s0 = jnp.sum(jnp.where(comp0, x2, 0.0), axis=-1, keepdims=True)
        s1 = jnp.sum(jnp.where(comp0, 0.0, x2), axis=-1, keepdims=True)
        ms = jnp.where(comp0, s0, s1) * (1.0 / dh)
        return xv * lax.rsqrt(ms + NORM_EPS) * gain

    def rope(xv, cos, sin_signed):
        partner = jnp.where(low_half, pltpu.roll(xv, DA_VDIM - dh // 2, axis=1),
                            pltpu.roll(xv, dh // 2, axis=1))
        return xv * cos + partner * sin_signed

    @pl.when(qi == 0)
    def _():
        kg = kg_ref[...]
        kn_ref[0:n_ctx, :] = head_norm(k_ref[0:n_ctx, :].astype(F32), kg).astype(BF16)
        for c0 in range(n_ctx, n_tot, tq):
            kk = head_norm(k_ref[c0:c0 + tq, :].astype(F32), kg)
            kn_ref[c0:c0 + tq, :] = rope(kk, cosk_ref[c0 - n_ctx:c0 - n_ctx + tq, :],
                                         sink_ref[c0 - n_ctx:c0 - n_ctx + tq, :]).astype(BF16)

    qn = rope(head_norm(q_ref[...].astype(F32), qg_ref[...]), cosq_ref[...], sinq_ref[...]) * (dh ** -0.5)
    q0 = jnp.where(comp0, qn, 0.0).astype(BF16)
    q1 = jnp.where(comp0, 0.0, qn).astype(BF16)
    kn = kn_ref[...]
    dims = (((1,), (1,)), ((), ()))
    s0 = lax.dot_general(q0, kn, dims, preferred_element_type=F32)
    s1 = lax.dot_general(q1, kn, dims, preferred_element_type=F32)
    e0 = jnp.exp(s0 - jnp.max(s0, axis=-1, keepdims=True))
    e1 = jnp.exp(s1 - jnp.max(s1, axis=-1, keepdims=True))
    lam_v = lam_ref[...]
    lam = (jnp.exp(jnp.sum(lam_v[0:1] * lam_v[1:2], axis=-1, keepdims=True))
           - jnp.exp(jnp.sum(lam_v[2:3] * lam_v[3:4], axis=-1, keepdims=True)) + lambda_init)
    a = e0 * (1.0 / jnp.sum(e0, axis=-1, keepdims=True)) - lam * (e1 * (1.0 / jnp.sum(e1, axis=-1, keepdims=True)))
    o = jnp.dot(a.astype(BF16), v_ref[...], preferred_element_type=F32)
    o_ref[...] = (_rms(o) * sg_ref[...] * (1.0 - lambda_init)).astype(BF16)


def _rope_tables(n_tokens):
    rows = n_tokens // GRID_W
    row = jnp.repeat(jnp.arange(rows, dtype=F32), GRID_W)
    col = jnp.tile(jnp.arange(GRID_W, dtype=F32), rows)
    n_freq = DA_HEAD_DIM // 4
    inv_freq = jnp.exp(-math.log(ROPE_BASE) * jnp.arange(n_freq, dtype=F32) / n_freq)
    ang = jnp.concatenate([row[:, None] * inv_freq, col[:, None] * inv_freq], axis=-1)
    cos, sin = jnp.cos(ang), jnp.sin(ang)
    return jnp.tile(cos, (1, 4)), jnp.tile(jnp.concatenate([-sin, sin], axis=-1), (1, 2))


def _attention(q, k, v, n_ctx, q_gain, k_gain, lam_vecs, sub_gain, lambda_init):
    nb, nh, n_tot, dv = k.shape
    n_lat = n_tot - n_ctx
    tq = Q_TILE
    qoff = n_ctx // tq
    cos_t, sin_t = _rope_tables(n_lat)
    gain2 = lambda gv: jnp.tile(gv, 2).reshape(1, dv)
    lam_pad = jnp.zeros((SUBLANES, dv), F32).at[0:4, 0:DA_HEAD_DIM].set(lam_vecs)
    kern = functools.partial(_attn_kernel, n_ctx=n_ctx, lambda_init=lambda_init)
    const = lambda shape: pl.BlockSpec(shape, lambda b, h, i: (0, 0))
    return pl.pallas_call(
        kern,
        out_shape=jax.ShapeDtypeStruct((nb, n_lat, nh * dv), BF16),
        grid=(nb, nh, n_lat // tq),
        in_specs=[pl.BlockSpec((None, None, tq, dv), lambda b, h, i: (b, h, i + qoff, 0)),
                  pl.BlockSpec((None, None, n_tot, dv), lambda b, h, i: (b, h, 0, 0)),
                  pl.BlockSpec((None, None, n_tot, dv), lambda b, h, i: (b, h, 0, 0)),
                  const((1, dv)), const((1, dv)),
                  pl.BlockSpec((tq, dv), lambda b, h, i: (i, 0)),
                  pl.BlockSpec((tq, dv), lambda b, h, i: (i, 0)),
                  const((n_lat, dv)), const((n_lat, dv)),
                  const((SUBLANES, dv)), const((1, dv))],
        out_specs=pl.BlockSpec((None, tq, dv), lambda b, h, i: (b, i, h)),
        scratch_shapes=[pltpu.VMEM((n_tot, dv), BF16)],
        compiler_params=_cparams(("parallel", "parallel", "arbitrary")),
        name="attention",
    )(q, k, v, gain2(q_gain), gain2(k_gain), cos_t, sin_t, cos_t, sin_t, lam_pad, sub_gain.reshape(1, dv))


def _oproj_kernel(o_ref, w_ref, x_ref, mod_ref, out_ref):
    b = pl.program_id(0)
    y = jnp.dot(o_ref[...], w_ref[...], preferred_element_type=F32)
    out_ref[...] = x_ref[...] + mod_ref[pl.ds(b, 1), :] * y


def _oproj(o, w_o, xs, n_ctx, mod_gt):
    nb, n_lat, d = o.shape
    r = ROW_TILE
    off = n_ctx // r
    return pl.pallas_call(
        _oproj_kernel,
        out_shape=jax.ShapeDtypeStruct((nb, n_lat, d), F32),
        grid=(nb, n_lat // r),
        in_specs=[pl.BlockSpec((None, r, d), lambda b, j: (b, j, 0)),
                  pl.BlockSpec((d, d), lambda b, j: (0, 0)),
                  pl.BlockSpec((None, r, d), lambda b, j: (b, j + off, 0)),
                  pl.BlockSpec((MOD_ROWS, d), lambda b, j: (0, 0))],
        out_specs=pl.BlockSpec((None, r, d), lambda b, j: (b, j, 0)),
        compiler_params=_cparams(("parallel", "parallel")),
        name="oproj",
    )(o, w_o.astype(BF16), xs, mod_gt)


def _attn_layer(xs, n_ctx, mod, g_mix, w_qkv, w_o, q_gain, k_gain, lq1, lk1, lq2, lk2, sub_gain, lambda_init):
    d = xs.shape[-1]
    q, k, v = _qkv(xs, n_ctx, mod[:, 0:2 * d], g_mix, w_qkv)
    o = _attention(q, k, v, n_ctx, q_gain, k_gain, jnp.stack([lq1, lk1, lq2, lk2]), sub_gain, lambda_init)
    return _oproj(o, w_o, xs, n_ctx, mod[:, 2 * d:3 * d])


def kernel(x, c, ctx, c_ctx, w_ada, b_ada, g_mix, g_ffn, s5_a_re, s5_a_im, s5_log_dt, s5_b_re, s5_b_im, s5_c_re, s5_c_im, s5_d, s5_w_glu, s5_b_glu, da_w_qkv, da_w_o, da_q_gain, da_k_gain, da_lam_q1, da_lam_k1, da_lam_q2, da_lam_k2, da_sub_gain, moe_w_router, moe_b_router, moe_w_gu, moe_b_gu, moe_w_down, moe_b_down):
    nb, n_lat, d = x.shape
    n_ctx = ctx.shape[1]
    depth = w_ada.shape[0]
    assert depth == 2 and nb % SUBLANES == 0 and nb + 1 <= MOD_ROWS
    cc = jnp.concatenate([c, c_ctx[None], jnp.zeros((MOD_ROWS - nb - 1, d), F32)], axis=0)
    mod = _ada(cc, w_ada, b_ada)
    xs = _s5_layer(ctx, x, mod[0], g_mix[0], s5_a_re[0], s5_a_im[0], s5_log_dt[0], s5_b_re[0], s5_b_im[0],
                   s5_c_re[0], s5_c_im[0], s5_d[0], s5_w_glu[0], s5_b_glu[0])
    xs = _moe_layer(xs, n_ctx, mod[0][:, 3 * d:5 * d], mod[0][:, 5 * d:6 * d], g_ffn[0], 0,
                    moe_w_router, moe_b_router, moe_w_gu, moe_b_gu, moe_w_down, moe_b_down)
    lambda_init = 0.8 - 0.6 * math.exp(-0.3 * 1)
    x1 = _attn_layer(xs, n_ctx, mod[1], g_mix[1], da_w_qkv[0], da_w_o[0], da_q_gain[0], da_k_gain[0],
                     da_lam_q1[0], da_lam_k1[0], da_lam_q2[0], da_lam_k2[0], da_sub_gain[0], lambda_init)
    return _moe_layer(x1, 0, mod[1][:, 3 * d:5 * d], mod[1][:, 5 * d:6 * d], g_ffn[1], 1,
                      moe_w_router, moe_b_router, moe_w_gu, moe_b_gu, moe_w_down, moe_b_down)
```

```python
import functools
import math

import jax
import jax.numpy as jnp
from jax import lax
from jax.experimental import pallas as pl
from jax.experimental.pallas import tpu as pltpu

NORM_EPS = 1e-6
N_MIXERS = 2
S5_GROUP = 16
S5_STATE = 64
DA_HEAD_DIM = 64
DA_VDIM = 2 * DA_HEAD_DIM
GRID_W = 64
ROPE_BASE = 10000.0
N_EXPERTS = 32
TOP_K = 4
SWIGLU_LIMIT = 7.0
SWIGLU_ALPHA = 1.702
LOG2_E = 1.4426950408889634

LANES = 128
SUBLANES = 8
VMEM_LIMIT_BYTES = 56 * 1024 * 1024

S5_SLAB_GROUPS = LANES // S5_GROUP
S5_SLAB_STATE = S5_SLAB_GROUPS * S5_STATE
S5_CHUNK = 64
S5_ROW_BLOCKS = 4
TM_TILE = 32
ROW_TILE = 256
ROUTER_TILE = 512
EXPERT_ROWS = 512
COMBINE_TILE = 256
Q_TILE = 512
ATTN_ROW_BLOCK = 128
HEADS_PER_STEP = 2
MOD_ROWS = 24

F32 = jnp.float32
BF16 = jnp.bfloat16


def _cparams(sem, vmem=VMEM_LIMIT_BYTES):
    return pltpu.CompilerParams(dimension_semantics=sem, vmem_limit_bytes=vmem)


def _rms(x):
    return x * lax.rsqrt(jnp.mean(x * x, axis=-1, keepdims=True) + NORM_EPS)


def _ada_kernel(c_ref, w_ref, b_ref, o_ref):
    cv = c_ref[...]
    s = cv * jax.nn.sigmoid(cv)
    o_ref[...] = jnp.dot(s, w_ref[...], precision=lax.Precision.HIGHEST,
                         preferred_element_type=F32) + b_ref[...]


def _ada(cc, w_ada, b_ada):
    depth, d, d6 = w_ada.shape
    nj = d6 // d
    return pl.pallas_call(
        _ada_kernel,
        out_shape=jax.ShapeDtypeStruct((depth, MOD_ROWS, d6), F32),
        grid=(depth, nj),
        in_specs=[pl.BlockSpec((MOD_ROWS, d), lambda i, j: (0, 0)),
                  pl.BlockSpec((None, d, d), lambda i, j: (i, 0, j)),
                  pl.BlockSpec((None, 1, d), lambda i, j: (i, 0, j))],
        out_specs=pl.BlockSpec((None, MOD_ROWS, d), lambda i, j: (i, 0, j)),
        compiler_params=_cparams(("parallel", "parallel")),
        name="ada",
    )(cc, w_ada, b_ada.reshape(depth, 1, d6))


def _prenorm_tm_kernel(ctx_ref, x_ref, mod_ref, g_ref, o_ref, *, n_ctx_tiles, nb, d):
    i = pl.program_id(0)
    tt = o_ref.shape[1] // nb
    g = g_ref[...]

    def emit(src_ref, mod_row):
        for b in range(nb):
            r = mod_row(b)
            sh = mod_ref[r:r + 1, 0:d]
            sc = mod_ref[r:r + 1, d:2 * d]
            h = (_rms(src_ref[b]) * g) * (1.0 + sc) + sh
            for s in range(d // LANES):
                o_ref[s, pl.ds(b, tt, stride=nb), :] = h[:, s * LANES:(s + 1) * LANES]

    @pl.when(i < n_ctx_tiles)
    def _():
        emit(ctx_ref, lambda b: nb)

    @pl.when(i >= n_ctx_tiles)
    def _():
        emit(x_ref, lambda b: b)


def _prenorm_tm(ctx, x, mod, g):
    nb, n_ctx, d = ctx.shape
    n_lat = x.shape[1]
    tt = TM_TILE
    nct, nlt = n_ctx // tt, n_lat // tt
    kern = functools.partial(_prenorm_tm_kernel, n_ctx_tiles=nct, nb=nb, d=d)
    return pl.pallas_call(
        kern,
        out_shape=jax.ShapeDtypeStruct((d // LANES, (n_ctx + n_lat) * nb, LANES), F32),
        grid=(nct + nlt,),
        in_specs=[pl.BlockSpec((nb, tt, d), lambda i: (0, jnp.minimum(i, nct - 1), 0)),
                  pl.BlockSpec((nb, tt, d), lambda i: (0, jnp.maximum(i - nct, 0), 0)),
                  pl.BlockSpec((MOD_ROWS, 2 * d), lambda i: (0, 0)),
                  pl.BlockSpec((1, d), lambda i: (0, 0))],
        out_specs=pl.BlockSpec((d // LANES, tt * nb, LANES), lambda i: (0, i, 0)),
        compiler_params=_cparams(("parallel",)),
        name="prenorm_tm",
    )(ctx, x, mod, g)


def _s5_disc_kernel(ar_ref, ai_ref, ldt_ref, br_ref, bi_ref, lr_ref, li_ref, bbr_ref, bbi_ref):
    ar, ai = ar_ref[...], ai_ref[...]
    dt = jnp.exp(ldt_ref[...])
    mag = jnp.exp(ar * dt)
    lr = mag * jnp.cos(ai * dt)
    li = mag * jnp.sin(ai * dt)
    nr, ni = lr - 1.0, li
    den = ar * ar + ai * ai
    qr = (nr * ar + ni * ai) / den
    qi = (ni * ar - nr * ai) / den
    br, bi = br_ref[...], bi_ref[...]
    lr_ref[...] = lr
    li_ref[...] = li
    bbr_ref[...] = qr * br - qi * bi
    bbi_ref[...] = qr * bi + qi * br


def _s5_disc(a_re, a_im, log_dt, b_re, b_im):
    nd, g, p = a_re.shape
    cg = b_re.shape[-1]
    gp = g * p
    col = lambda a: a.reshape(nd, gp, 1)
    ldt = jnp.broadcast_to(log_dt[:, :, None], (nd, g, p))
    spec1 = pl.BlockSpec((None, gp, 1), lambda i: (i, 0, 0))
    specb = pl.BlockSpec((None, gp, cg), lambda i: (i, 0, 0))
    lr, li, bbr, bbi = pl.pallas_call(
        _s5_disc_kernel,
        out_shape=[jax.ShapeDtypeStruct((nd, gp, 1), F32)] * 2 + [jax.ShapeDtypeStruct((nd, gp, cg), F32)] * 2,
        grid=(nd,),
        in_specs=[spec1, spec1, spec1, specb, specb],
        out_specs=[spec1, spec1, specb, specb],
        compiler_params=_cparams(("parallel",)),
        name="s5_disc",
    )(col(a_re), col(a_im), col(ldt), b_re.reshape(nd, gp, cg), b_im.reshape(nd, gp, cg))
    return (lr.reshape(nd, g, p), li.reshape(nd, g, p),
            bbr.reshape(nd, g, p, cg), bbi.reshape(nd, g, p, cg))


def _s5_slab_params(lam_re, lam_im, bb_re, bb_im, c_re, c_im):
    nd, g, p = lam_re.shape
    cg = bb_re.shape[-1]
    ns, sg = g // S5_SLAB_GROUPS, S5_SLAB_GROUPS
    eye = jnp.eye(sg, dtype=F32)

    def b_mat(bb):
        bb = bb.reshape(nd, ns, sg, p, cg)
        return jnp.einsum('dsjpc,jk->dsjckp', bb, eye).reshape(nd, ns, sg * cg, sg * p)

    def c_mat(cm):
        cm = cm.reshape(nd, ns, sg, cg, p)
        return jnp.einsum('dsjcp,jk->dskpjc', cm, eye).reshape(nd, ns, sg * p, sg * cg)

    bm = jnp.concatenate([b_mat(bb_re), b_mat(bb_im)], axis=-1).astype(BF16)
    lam = lambda a: a.reshape(nd, ns, 1, sg * p)
    return bm, c_mat(c_re).astype(BF16), c_mat(c_im).astype(BF16), lam(lam_re), lam(lam_im)


def _s5_scan_kernel(u_ref, bm_ref, cr_ref, ci_ref, lr_ref, li_ref, y_ref, x0_ref, x1_ref, h0_ref, h1_ref, st_ref):
    d = pl.program_id(0)
    c = pl.program_id(2)
    t_len, nb = u_ref.shape[0], u_ref.shape[1] * u_ref.shape[2]
    ns = S5_SLAB_STATE
    xs, hs = (x0_ref, x1_ref), (h0_ref, h1_ref)

    @pl.when(c == 0)
    def _():
        for ref in xs + hs:
            ref[...] = jnp.zeros_like(ref)

    def stages(in_ref, bu_ref, h_ref, out_ref):
        lr = jnp.broadcast_to(lr_ref[...], (SUBLANES, ns))
        li = jnp.broadcast_to(li_ref[...], (SUBLANES, ns))
        fresh = c <= 1
        state = [(jnp.where(fresh, 0.0, st_ref[0, o]), jnp.where(fresh, 0.0, st_ref[1, o]))
                 for o in range(nb // SUBLANES)]
        tb = t_len // S5_ROW_BLOCKS
        for j in range(S5_ROW_BLOCKS):
            rsl = slice(j * tb * nb, (j + 1) * tb * nb)
            u = u_ref[j * tb:(j + 1) * tb].reshape(tb * nb, LANES).astype(BF16)
            in_ref[rsl, :] = jnp.dot(u, bm_ref[...], preferred_element_type=F32)

            for s in range(j * tb, (j + 1) * tb):
                t = jnp.where(d == 0, s, t_len - 1 - s)
                for o in range(nb // SUBLANES):
                    hr, hi = state[o]
                    row = pl.multiple_of(t * nb + o * SUBLANES, SUBLANES)
                    br = bu_ref[pl.ds(row, SUBLANES), 0:ns]
                    bi = bu_ref[pl.ds(row, SUBLANES), ns:2 * ns]
                    nr = lr * hr - li * hi + br
                    ni = lr * hi + li * hr + bi
                    h_ref[pl.ds(row, SUBLANES), 0:ns] = nr
                    h_ref[pl.ds(row, SUBLANES), ns:2 * ns] = ni
                    state[o] = (nr, ni)

            y = (jnp.dot(out_ref[rsl, 0:ns].astype(BF16), cr_ref[...], preferred_element_type=F32)
                 - jnp.dot(out_ref[rsl, ns:2 * ns].astype(BF16), ci_ref[...], preferred_element_type=F32))
            y_ref[j * tb:(j + 1) * tb] = y.reshape((tb,) + y_ref.shape[1:])
        for o in range(nb // SUBLANES):
            st_ref[0, o] = state[o][0]
            st_ref[1, o] = state[o][1]

    phase = lax.rem(c, 2)
    for r in range(2):
        @pl.when(phase == r)
        def _(r=r):
            stages(xs[r], xs[1 - r], hs[1 - r], hs[r])


def _s5_scan(u_t, bm, cr, ci, lam_r, lam_i, nb, n_ctx, n_tot):
    nslab = u_t.shape[0]
    nbo = nb // SUBLANES
    t_len = S5_CHUNK
    ncc, nc = n_ctx // t_len, n_tot // t_len
    u5 = u_t.reshape(nslab, n_tot, nbo, SUBLANES, LANES)

    def chunk(d, c):
        c = jnp.clip(c, 0, nc - 1)
        back = jnp.where(c < ncc, ncc - 1 - c, ncc + nc - 1 - c)
        return jnp.where(d == 0, c, back)

    pspec = lambda r, k: pl.BlockSpec((None, None, r, k), lambda d, s, c: (d, s, 0, 0))
    rows = t_len * nb
    return pl.pallas_call(
        _s5_scan_kernel,
        out_shape=jax.ShapeDtypeStruct((2, nslab, n_tot, nbo, SUBLANES, LANES), F32),
        grid=(2, nslab, nc + 2),
        in_specs=[pl.BlockSpec((None, t_len, nbo, SUBLANES, LANES), lambda d, s, c: (s, chunk(d, c), 0, 0, 0)),
                  pspec(LANES, 2 * S5_SLAB_STATE), pspec(S5_SLAB_STATE, LANES), pspec(S5_SLAB_STATE, LANES),
                  pspec(1, S5_SLAB_STATE), pspec(1, S5_SLAB_STATE)],
        out_specs=pl.BlockSpec((None, None, t_len, nbo, SUBLANES, LANES),
                               lambda d, s, c: (d, s, chunk(d, c - 2), 0, 0, 0)),
        scratch_shapes=[pltpu.VMEM((rows, 2 * S5_SLAB_STATE), F32)] * 4
                       + [pltpu.VMEM((2, nbo, SUBLANES, S5_SLAB_STATE), F32)],
        compiler_params=_cparams(("parallel", "parallel", "arbitrary")),
        name="s5_scan",
    )(u5, bm, cr, ci, lam_r, lam_i)


def _s5_out_kernel(y_ref, u_ref, dsk_ref, w_ref, bg_ref, ctx_ref, x_ref, mod_ref, o_ref, tr_ref,
                   *, n_ctx_tiles, nb, d):
    i = pl.program_id(0)
    nslab = d // LANES
    rows = u_ref.shape[1]
    tt = rows // nb
    y = (y_ref[0] + y_ref[1]).reshape(nslab, rows, LANES)
    v = jax.nn.gelu(y + dsk_ref[...] * u_ref[...])
    lhs = jnp.concatenate([v[s] for s in range(nslab)], axis=-1).astype(BF16)
    z = jnp.dot(lhs, w_ref[...], preferred_element_type=F32) + bg_ref[...]
    m = z[:, 0:d] * jax.nn.sigmoid(z[:, d:2 * d])
    for s in range(nslab):
        tr_ref[s] = m[:, s * LANES:(s + 1) * LANES]

    def emit(src_ref, mod_row):
        for b in range(nb):
            r = mod_row(b)
            gt = mod_ref[r:r + 1, :]
            mb = jnp.concatenate([tr_ref[s, pl.ds(b, tt, stride=nb), :] for s in range(nslab)], axis=-1)
            o_ref[b] = src_ref[b] + gt * mb

    @pl.when(i < n_ctx_tiles)
    def _():
        emit(ctx_ref, lambda b: nb)

    @pl.when(i >= n_ctx_tiles)
    def _():
        emit(x_ref, lambda b: b)


def _s5_out(y_t, u_t, d_skip, w_glu, b_glu, ctx, x, gt1):
    nb, n_ctx, d = ctx.shape
    n_lat = x.shape[1]
    n_tot = n_ctx + n_lat
    nslab, nbo = d // LANES, nb // SUBLANES
    tt = TM_TILE
    nct, nlt = n_ctx // tt, n_lat // tt
    kern = functools.partial(_s5_out_kernel, n_ctx_tiles=nct, nb=nb, d=d)
    return pl.pallas_call(
        kern,
        out_shape=jax.ShapeDtypeStruct((nb, n_tot, d), F32),
        grid=(nct + nlt,),
        in_specs=[pl.BlockSpec((2, nslab, tt, nbo, SUBLANES, LANES), lambda i: (0, 0, i, 0, 0, 0)),
                  pl.BlockSpec((nslab, tt * nb, LANES), lambda i: (0, i, 0)),
                  pl.BlockSpec((nslab, 1, LANES), lambda i: (0, 0, 0)),
                  pl.BlockSpec((d, 2 * d), lambda i: (0, 0)),
                  pl.BlockSpec((1, 2 * d), lambda i: (0, 0)),
                  pl.BlockSpec((nb, tt, d), lambda i: (0, jnp.minimum(i, nct - 1), 0)),
                  pl.BlockSpec((nb, tt, d), lambda i: (0, jnp.maximum(i - nct, 0), 0)),
                  pl.BlockSpec((MOD_ROWS, d), lambda i: (0, 0))],
        out_specs=pl.BlockSpec((nb, tt, d), lambda i: (0, i, 0)),
        scratch_shapes=[pltpu.VMEM((nslab, tt * nb, LANES), F32)],
        compiler_params=_cparams(("parallel",)),
        name="s5_out",
    )(y_t, u_t, d_skip.reshape(nslab, 1, LANES), w_glu.astype(BF16), b_glu.reshape(1, 2 * d), ctx, x, gt1)


def _s5_layer(xs_ctx, xs_lat, mod, g_mix, a_re, a_im, log_dt, b_re, b_im, c_re, c_im, d_skip, w_glu, b_glu):
    nb, n_ctx, d = xs_ctx.shape
    n_tot = n_ctx + xs_lat.shape[1]
    u_t = _prenorm_tm(xs_ctx, xs_lat, mod[:, 0:2 * d], g_mix.reshape(1, d))
    lam_re, lam_im, bb_re, bb_im = _s5_disc(a_re, a_im, log_dt, b_re, b_im)
    bm, cr, ci, lam_r, lam_i = _s5_slab_params(lam_re, lam_im, bb_re, bb_im, c_re, c_im)
    y_t = _s5_scan(u_t, bm, cr, ci, lam_r, lam_i, nb, n_ctx, n_tot)
    return _s5_out(y_t, u_t, d_skip, w_glu, b_glu, xs_ctx, xs_lat, mod[:, 2 * d:3 * d])


NEG_PAD = -1e30
NEG_MASK = -3e38


def _split_bf16(a):
    hi = a.astype(BF16)
    return hi, (a - hi.astype(F32)).astype(BF16)


def _router_kernel(x_ref, mod_ref, g_ref, w_ref, bias_ref, h_ref, sel_ref, selt_ref, meta_ref, cnt_ref,
                   carry_ref, *, n_ctx_tiles, nb, d):
    b = pl.program_id(0)
    j = pl.program_id(1)
    rows = x_ref.shape[0]

    @pl.when((b == 0) & (j == 0))
    def _():
        carry_ref[...] = jnp.zeros_like(carry_ref)

    row = jnp.where(j < n_ctx_tiles, nb, b)
    mrow = mod_ref[pl.ds(row, 1), :]
    h = (_rms(x_ref[...]) * g_ref[...]) * (1.0 + mrow[:, d:2 * d]) + mrow[:, 0:d]
    h_ref[...] = h.astype(BF16)

    hi, lo = _split_bf16(h)
    whi, wlo = _split_bf16(w_ref[...])
    dot = functools.partial(jnp.dot, preferred_element_type=F32)
    logits = dot(hi, whi) + dot(lo, whi) + dot(hi, wlo) + bias_ref[...]

    lane = lax.broadcasted_iota(jnp.int32, (rows, LANES), 1)
    work = logits
    tops, idxs, hots = [], [], []
    for _ in range(TOP_K):
        m = jnp.max(work, axis=-1, keepdims=True)
        idx = jnp.min(jnp.where(work == m, lane, LANES), axis=-1, keepdims=True)
        hot = lane == idx
        work = jnp.where(hot, NEG_MASK, work)
        tops.append(m)
        idxs.append(idx)
        hots.append(hot)
    exps = [jnp.exp(m - tops[0]) for m in tops]
    denom = exps[0] + exps[1] + exps[2] + exps[3]

    multi = jnp.zeros((rows, LANES), F32)
    for hot in hots:
        multi = multi + jnp.where(hot, 1.0, 0.0)
    r_i = lax.broadcasted_iota(jnp.int32, (rows, rows), 0)
    c_i = lax.broadcasted_iota(jnp.int32, (rows, rows), 1)
    tri = jnp.where(c_i < r_i, 1.0, 0.0).astype(BF16)
    in_tile = dot(tri, multi.astype(BF16))
    cnt_raw = jnp.sum(multi, axis=0, keepdims=True).astype(jnp.int32)
    cnt_tile = (((cnt_raw + (SUBLANES - 1)) // SUBLANES) * SUBLANES).astype(F32)
    e_r = lax.broadcasted_iota(jnp.int32, (LANES, LANES), 0)
    e_c = lax.broadcasted_iota(jnp.int32, (LANES, LANES), 1)
    below = jnp.where(e_r < e_c, 1.0, 0.0).astype(BF16)
    prefix = dot(jnp.broadcast_to(cnt_tile, (SUBLANES, LANES)).astype(BF16), below)[0:1, :]
    slot = in_tile + prefix

    sel = jnp.zeros((rows, LANES), F32)
    for k in range(TOP_K):
        pos = jnp.sum(jnp.where(hots[k], slot, 0.0), axis=-1, keepdims=True)
        sel = jnp.where(lane == k, idxs[k].astype(F32), sel)
        sel = jnp.where(lane == TOP_K + k, exps[k] / denom, sel)
        sel = jnp.where(lane == 2 * TOP_K + k, pos, sel)
    sel_ref[...] = sel
    selt_ref[...] = sel.T[2 * TOP_K:2 * TOP_K + SUBLANES, :].astype(jnp.int32)

    carry = carry_ref[0:1, :]
    sub = lax.broadcasted_iota(jnp.int32, (SUBLANES, LANES), 0)
    meta = jnp.where(sub == 0, cnt_tile, jnp.where(sub == 1, carry, jnp.where(sub == 2, prefix, 0.0)))
    meta_ref[...] = meta.astype(jnp.int32)
    carry_ref[0:1, :] = carry + cnt_tile
    cnt_ref[...] = jnp.broadcast_to(carry + cnt_tile, cnt_ref.shape).astype(jnp.int32)


def _router(xs, n_ctx, mod_shsc, g, w_router, b_router):
    nb, t_len, d = xs.shape
    r = ROW_TILE
    nj = t_len // r
    ne = w_router.shape[1]
    w_pad = jnp.zeros((d, LANES), F32).at[:, :ne].set(w_router)
    b_pad = jnp.full((1, LANES), NEG_PAD, F32).at[0, :ne].set(b_router)
    kern = functools.partial(_router_kernel, n_ctx_tiles=n_ctx // r, nb=nb, d=d)
    tile = lambda b, j: (b * nj + j, 0)
    return pl.pallas_call(
        kern,
        out_shape=[jax.ShapeDtypeStruct((nb * t_len, d), BF16),
                   jax.ShapeDtypeStruct((nb * t_len, LANES), F32),
                   jax.ShapeDtypeStruct((nb * nj * SUBLANES, r), jnp.int32),
                   jax.ShapeDtypeStruct((nb * nj, SUBLANES, LANES), jnp.int32),
                   jax.ShapeDtypeStruct((SUBLANES, LANES), jnp.int32)],
        grid=(nb, nj),
        in_specs=[pl.BlockSpec((None, r, d), lambda b, j: (b, j, 0)),
                  pl.BlockSpec((MOD_ROWS, 2 * d), lambda b, j: (0, 0)),
                  pl.BlockSpec((1, d), lambda b, j: (0, 0)),
                  pl.BlockSpec((d, LANES), lambda b, j: (0, 0)),
                  pl.BlockSpec((1, LANES), lambda b, j: (0, 0))],
        out_specs=[pl.BlockSpec((r, d), tile),
                   pl.BlockSpec((r, LANES), tile),
                   pl.BlockSpec((SUBLANES, r), tile),
                   pl.BlockSpec((None, SUBLANES, LANES), lambda b, j: (b * nj + j, 0, 0)),
                   pl.BlockSpec((SUBLANES, LANES), lambda b, j: (0, 0))],
        scratch_shapes=[pltpu.VMEM((SUBLANES, LANES), F32)],
        compiler_params=_cparams(("arbitrary", "arbitrary")),
        name="router",
    )(xs, mod_shsc, g.reshape(1, d), w_pad, b_pad)


def _plan_kernel(cnt_ref, ps_ref, be_ref, first_ref, nused_ref, *, ne, rb):
    n_blocks = be_ref.shape[0]

    def fill(i, carry):
        be_ref[i] = ne - 1
        first_ref[i] = 0
        return carry
    lax.fori_loop(0, n_blocks, fill, 0)

    def per_expert(e, acc):
        nblk = (cnt_ref[0, e] + (rb - 1)) // rb
        ps_ref[e] = acc
        b0 = acc // rb

        def mark(i, carry):
            be_ref[b0 + i] = e
            first_ref[b0 + i] = jnp.where(i == 0, 1, 0)
            return carry
        lax.fori_loop(0, nblk, mark, 0)
        return acc + nblk * rb

    total = lax.fori_loop(0, ne, per_expert, 0)

    def tail(e, carry):
        ps_ref[e] = total
        return carry
    lax.fori_loop(ne, ps_ref.shape[0], tail, 0)
    nused_ref[0] = total // rb


def _plan(cnt, ne, n_blocks):
    smem = pl.BlockSpec(memory_space=pltpu.SMEM)
    return pl.pallas_call(
        functools.partial(_plan_kernel, ne=ne, rb=EXPERT_ROWS),
        out_shape=[jax.ShapeDtypeStruct((LANES,), jnp.int32),
                   jax.ShapeDtypeStruct((n_blocks,), jnp.int32),
                   jax.ShapeDtypeStruct((n_blocks,), jnp.int32),
                   jax.ShapeDtypeStruct((1,), jnp.int32)],
        in_specs=[smem],
        out_specs=[smem, smem, smem, smem],
        name="plan",
    )(cnt)


_SEG_SIZES = tuple(p for p in (ROW_TILE >> s for s in range(ROW_TILE.bit_length())) if p >= SUBLANES)
TILE_SLOTS = ROW_TILE * TOP_K + N_EXPERTS * SUBLANES


def _segment_copies(n, src, dst, make_copy, wait=False):
    for p in _SEG_SIZES:
        take = (n & p) != 0

        @pl.when(take)
        def _():
            cp = make_copy(pl.multiple_of(src, SUBLANES), pl.multiple_of(dst, SUBLANES), p)
            cp.wait() if wait else cp.start()
        step = jnp.where(take, p, 0)
        src = src + step
        dst = dst + step


def _dispatch_kernel(meta_ref, metap_ref, ps_ref, cnt_ref, h_ref, selt_ref, xbuf_ref, xs_ref, zero_ref, sem,
                     zsem, *, ne, rb):
    rows = h_ref.shape[0]
    slots = xs_ref.shape[1]

    @pl.when(pl.program_id(0) == 0)
    def _():
        zero_ref[...] = jnp.zeros_like(zero_ref)

        def pad_copy(_, dst, p):
            return pltpu.make_async_copy(zero_ref.at[pl.ds(0, p), :], xbuf_ref.at[pl.ds(dst, p), :], zsem)

        for wait in (False, True):
            def pad_rows(e, carry, wait=wait):
                cnt = cnt_ref[0, e]
                _segment_copies((-cnt) & (rb - 1), 0, ps_ref[e] + cnt, pad_copy, wait)
                return carry
            lax.fori_loop(0, ne, pad_rows, 0)

    pos = selt_ref[0:TOP_K, :]
    s_iota = lax.broadcasted_iota(jnp.int32, (slots, rows), 0)
    perm = jnp.zeros((slots, rows), F32)
    for k in range(TOP_K):
        perm = perm + jnp.where(s_iota == pos[k:k + 1, :], 1.0, 0.0)
    step = pl.program_id(0)
    slot = lax.rem(step, 2)
    xs_ref[slot] = jnp.dot(perm.astype(BF16), h_ref[...], preferred_element_type=F32)

    def segments(mref, which, wait):
        def seg_copy(src, dst, p):
            return pltpu.make_async_copy(xs_ref.at[which, pl.ds(src, p), :], xbuf_ref.at[pl.ds(dst, p), :],
                                         sem.at[which])

        def per_expert(e, carry):
            _segment_copies(mref[0, e], mref[2, e], ps_ref[e] + mref[1, e], seg_copy, wait)
            return carry
        lax.fori_loop(0, ne, per_expert, 0)

    segments(meta_ref, slot, False)

    @pl.when(step > 0)
    def _():
        segments(metap_ref, 1 - slot, True)

    @pl.when(step == pl.num_programs(0) - 1)
    def _():
        segments(meta_ref, slot, True)


def _dispatch(h2, selt, meta, ps, cnt, ne, n_buf):
    n_tok, d = h2.shape
    r = ROW_TILE
    smem = pl.BlockSpec(memory_space=pltpu.SMEM)
    return pl.pallas_call(
        functools.partial(_dispatch_kernel, ne=ne, rb=EXPERT_ROWS),
        out_shape=jax.ShapeDtypeStruct((n_buf, d), F32),
        grid=(n_tok // r,),
        in_specs=[pl.BlockSpec((None, SUBLANES, LANES), lambda i: (i, 0, 0), memory_space=pltpu.SMEM),
                  pl.BlockSpec((None, SUBLANES, LANES), lambda i: (jnp.maximum(i - 1, 0), 0, 0),
                               memory_space=pltpu.SMEM),
                  smem, smem,
                  pl.BlockSpec((r, d), lambda i: (i, 0)),
                  pl.BlockSpec((SUBLANES, r), lambda i: (i, 0))],
        out_specs=pl.BlockSpec(memory_space=pl.ANY),
        scratch_shapes=[pltpu.VMEM((2, TILE_SLOTS, d), F32), pltpu.VMEM((_SEG_SIZES[0], d), F32),
                        pltpu.SemaphoreType.DMA((2,)), pltpu.SemaphoreType.DMA(())],
        compiler_params=_cparams(("arbitrary",)),
        name="dispatch",
    )(meta, meta, ps, cnt, h2, selt)


def _expert_kernel(be_ref, first_ref, nused_ref, x_ref, wgu_ref, bgu_ref, wd_ref, bd_ref, y_ref,
                   wgu_bf, wd_bf):
    i = pl.program_id(0)
    dff = wd_ref.shape[0]

    @pl.when(i < nused_ref[0])
    def _():
        @pl.when(first_ref[i] == 1)
        def _():
            wgu_bf[...] = wgu_ref[...].astype(BF16)
            wd_bf[...] = wd_ref[...].astype(BF16)

        gu = jnp.dot(x_ref[...].astype(BF16), wgu_bf[...], preferred_element_type=F32) + bgu_ref[...]
        gate = jnp.minimum(gu[:, 0:dff], SWIGLU_LIMIT)
        up = jnp.clip(gu[:, dff:2 * dff], -SWIGLU_LIMIT, SWIGLU_LIMIT)
        act = (up + 1.0) * gate * jax.nn.sigmoid(SWIGLU_ALPHA * gate)
        y_ref[...] = jnp.dot(act.astype(BF16), wd_bf[...], preferred_element_type=F32) + bd_ref[...]

    @pl.when(i >= nused_ref[0])
    def _():
        y_ref[...] = jnp.zeros_like(y_ref)


def _experts(xbuf, block_e, first, n_used, layer, w_gu, b_gu, w_down, b_down):
    n_buf, d = xbuf.shape
    depth, ne, _, dff2 = w_gu.shape
    dff = dff2 // 2
    rb = EXPERT_ROWS
    blk = lambda i, nu: jnp.minimum(i, nu[0] - 1)
    wmap = lambda i, be, fi, nu: (layer, be[blk(i, nu)], 0, 0)
    grid_spec = pltpu.PrefetchScalarGridSpec(
        num_scalar_prefetch=3,
        grid=(n_buf // rb,),
        in_specs=[pl.BlockSpec((rb, d), lambda i, be, fi, nu: (blk(i, nu), 0)),
                  pl.BlockSpec((None, None, d, dff2), wmap),
                  pl.BlockSpec((None, None, 1, dff2), wmap),
                  pl.BlockSpec((None, None, dff, d), wmap),
                  pl.BlockSpec((None, None, 1, d), wmap)],
        out_specs=pl.BlockSpec((rb, d), lambda i, be, fi, nu: (i, 0)),
        scratch_shapes=[pltpu.VMEM((d, dff2), BF16), pltpu.VMEM((dff, d), BF16)])
    return pl.pallas_call(
        _expert_kernel,
        out_shape=jax.ShapeDtypeStruct((n_buf, d), F32),
        grid_spec=grid_spec,
        compiler_params=_cparams(("arbitrary",)),
        name="experts",
    )(block_e, first, n_used, xbuf, w_gu, b_gu.reshape(depth, ne, 1, dff2), w_down,
      b_down.reshape(depth, ne, 1, d))


def _combine_kernel(meta_ref, metan_ref, ps_ref, sel_ref, x_ref, mod_ref, ybuf_ref, o_ref, ys_ref, sem,
                    *, n_ctx_tiles, nb, ne):
    b = pl.program_id(0)
    j = pl.program_id(1)
    nj = pl.num_programs(1)
    step = b * nj + j
    n_steps = pl.num_programs(0) * nj
    slot = lax.rem(step, 2)
    rows = x_ref.shape[0]
    slots = ys_ref.shape[1]

    def segments(mref, to_slot, wait):
        def seg_copy(src, dst, p):
            return pltpu.make_async_copy(ybuf_ref.at[pl.ds(dst, p), :], ys_ref.at[to_slot, pl.ds(src, p), :],
                                         sem.at[to_slot])

        def per_expert(e, carry):
            _segment_copies(mref[0, e], mref[2, e], ps_ref[e] + mref[1, e], seg_copy, wait)
            return carry
        lax.fori_loop(0, ne, per_expert, 0)

    @pl.when(step == 0)
    def _():
        ys_ref[...] = jnp.zeros_like(ys_ref)
        segments(meta_ref, 0, False)

    @pl.when(step + 1 < n_steps)
    def _():
        segments(metan_ref, 1 - slot, False)

    segments(meta_ref, slot, True)

    sel = sel_ref[...]
    l_iota = lax.broadcasted_iota(jnp.int32, (rows, slots), 1).astype(F32)
    w = jnp.zeros((rows, slots), F32)
    for k in range(TOP_K):
        w = w + jnp.where(l_iota == sel[:, 2 * TOP_K + k:2 * TOP_K + k + 1], sel[:, TOP_K + k:TOP_K + k + 1], 0.0)
    whi, wlo = _split_bf16(w)
    ys = ys_ref[slot].astype(BF16)
    acc = jnp.dot(whi, ys, preferred_element_type=F32) + jnp.dot(wlo, ys, preferred_element_type=F32)
    row = jnp.where(j < n_ctx_tiles, nb, b)
    o_ref[...] = x_ref[...] + mod_ref[pl.ds(row, 1), :] * acc


def _combine(ybuf, meta, ps, sel, xs, n_ctx, mod_gt, ne):
    nb, t_len, d = xs.shape
    r = ROW_TILE
    nj = t_len // r
    kern = functools.partial(_combine_kernel, n_ctx_tiles=n_ctx // r, nb=nb, ne=ne)
    last = nb * nj - 1
    mspec = lambda f: pl.BlockSpec((None, SUBLANES, LANES), f, memory_space=pltpu.SMEM)
    return pl.pallas_call(
        kern,
        out_shape=jax.ShapeDtypeStruct((nb, t_len, d), F32),
        grid=(nb, nj),
        in_specs=[mspec(lambda b, j: (b * nj + j, 0, 0)),
                  mspec(lambda b, j: (jnp.minimum(b * nj + j + 1, last), 0, 0)),
                  pl.BlockSpec(memory_space=pltpu.SMEM),
                  pl.BlockSpec((r, LANES), lambda b, j: (b * nj + j, 0)),
                  pl.BlockSpec((None, r, d), lambda b, j: (b, j, 0)),
                  pl.BlockSpec((MOD_ROWS, d), lambda b, j: (0, 0)),
                  pl.BlockSpec(memory_space=pl.ANY)],
        out_specs=pl.BlockSpec((None, r, d), lambda b, j: (b, j, 0)),
        scratch_shapes=[pltpu.VMEM((2, TILE_SLOTS, d), F32), pltpu.SemaphoreType.DMA((2,))],
        compiler_params=_cparams(("arbitrary", "arbitrary")),
        name="combine",
    )(meta, meta, ps, sel, xs, mod_gt, ybuf)


def _moe_layer(xs, n_ctx, mod_shsc, mod_gt, g, layer, w_router, b_router, w_gu, b_gu, w_down, b_down):
    nb, t_len, d = xs.shape
    ne = w_router.shape[-1]
    n_tiles = nb * t_len // ROW_TILE
    n_blocks = -(-n_tiles * TILE_SLOTS // EXPERT_ROWS) + ne
    h2, sel, selt, meta, cnt = _router(xs, n_ctx, mod_shsc, g, w_router[layer], b_router[layer])
    ps, block_e, first, n_used = _plan(cnt, ne, n_blocks)
    xbuf = _dispatch(h2, selt, meta, ps, cnt, ne, n_blocks * EXPERT_ROWS)
    ybuf = _experts(xbuf, block_e, first, n_used, layer, w_gu, b_gu, w_down, b_down)
    return _combine(ybuf, meta, ps, sel, xs, n_ctx, mod_gt, ne)


def _qkv_kernel(x_ref, mod_ref, g_ref, w_ref, q_ref, k_ref, v_ref, *, n_ctx_tiles, nb, d):
    b = pl.program_id(0)
    j = pl.program_id(1)
    row = jnp.where(j < n_ctx_tiles, nb, b)
    mrow = mod_ref[pl.ds(row, 1), :]
    h = (_rms(x_ref[...]) * g_ref[...]) * (1.0 + mrow[:, d:2 * d]) + mrow[:, 0:d]
    qkv = jnp.dot(h.astype(BF16), w_ref[...], preferred_element_type=F32)
    for hd in range(d // DA_VDIM):
        lo = hd * DA_VDIM
        q_ref[hd] = qkv[:, lo:lo + DA_VDIM].astype(BF16)
        k_ref[hd] = qkv[:, d + lo:d + lo + DA_VDIM].astype(BF16)
        v_ref[hd] = qkv[:, 2 * d + lo:2 * d + lo + DA_VDIM].astype(BF16)


def _qkv(xs, n_ctx, mod_shsc, g, w_qkv):
    nb, t_len, d = xs.shape
    r = ROW_TILE
    nh = d // DA_VDIM
    kern = functools.partial(_qkv_kernel, n_ctx_tiles=n_ctx // r, nb=nb, d=d)
    nct = n_ctx // r
    hspec = pl.BlockSpec((None, nh, r, DA_VDIM), lambda b, j: (b, 0, j, 0))
    qspec = pl.BlockSpec((None, nh, r, DA_VDIM), lambda b, j: (b, 0, jnp.maximum(j - nct, 0), 0))
    kv_shape = jax.ShapeDtypeStruct((nb, nh, t_len, DA_VDIM), BF16)
    return pl.pallas_call(
        kern,
        out_shape=[jax.ShapeDtypeStruct((nb, nh, t_len - n_ctx, DA_VDIM), BF16), kv_shape, kv_shape],
        grid=(nb, t_len // r),
        in_specs=[pl.BlockSpec((None, r, d), lambda b, j: (b, j, 0)),
                  pl.BlockSpec((MOD_ROWS, 2 * d), lambda b, j: (0, 0)),
                  pl.BlockSpec((1, d), lambda b, j: (0, 0)),
                  pl.BlockSpec((d, 3 * d), lambda b, j: (0, 0))],
        out_specs=[qspec, hspec, hspec],
        compiler_params=_cparams(("parallel", "arbitrary")),
        name="qkv",
    )(xs, mod_shsc, g.reshape(1, d), w_qkv.astype(BF16))


def _attn_kernel(q_ref, k_ref, v_ref, qg_ref, kg_ref, cosq_ref, sinq_ref, cosk_ref, sink_ref, lam_ref,
                 sg_ref, o_ref, kn_ref, *, n_ctx, lambda_init):
    qi = pl.program_id(2)
    n_heads, n_tot = k_ref.shape[0], k_ref.shape[1]
    tq = q_ref.shape[1]
    dh = DA_HEAD_DIM
    lane = lax.broadcasted_iota(jnp.int32, (1, DA_VDIM), 1)
    comp0 = lane < dh
    low_half = lax.rem(lane, dh) < dh // 2

    def head_norm(xv, gain):
        x2 = xv * xv
        s0 = jnp.sum(jnp.where(comp0, x2, 0.0), axis=-1, keepdims=True)
        s1 = jnp.sum(jnp.where(comp0, 0.0, x2), axis=-1, keepdims=True)
        ms = jnp.where(comp0, s0, s1) * (1.0 / dh)
        return xv * lax.rsqrt(ms + NORM_EPS) * gain

    def rope(xv, cos, sin_signed):
        partner = jnp.where(low_half, pltpu.roll(xv, DA_VDIM - dh // 2, axis=1),
                            pltpu.roll(xv, dh // 2, axis=1))
        return xv * cos + partner * sin_signed

    @pl.when(qi == 0)
    def _():
        kg = kg_ref[...]
        for g in range(n_heads):
            kn_ref[g, 0:n_ctx, :] = head_norm(k_ref[g, 0:n_ctx, :].astype(F32), kg).astype(BF16)
            for c0 in range(n_ctx, n_tot, tq):
                kk = head_norm(k_ref[g, c0:c0 + tq, :].astype(F32), kg)
                kn_ref[g, c0:c0 + tq, :] = rope(kk, cosk_ref[c0 - n_ctx:c0 - n_ctx + tq, :],
                                                sink_ref[c0 - n_ctx:c0 - n_ctx + tq, :]).astype(BF16)

    lam_v = lam_ref[...]
    lam = (jnp.exp(jnp.sum(lam_v[0:1] * lam_v[1:2], axis=-1, keepdims=True))
           - jnp.exp(jnp.sum(lam_v[2:3] * lam_v[3:4], axis=-1, keepdims=True)) + lambda_init)
    dims = (((1,), (1,)), ((), ()))
    rb = ATTN_ROW_BLOCK
    units = [(g, r * rb) for g in range(n_heads) for r in range(tq // rb)]

    def scores(g, r0):
        qn = rope(head_norm(q_ref[g, r0:r0 + rb, :].astype(F32), qg_ref[...]),
                  cosq_ref[r0:r0 + rb, :], sinq_ref[r0:r0 + rb, :]) * (dh ** -0.5 * LOG2_E)
        qq = jnp.concatenate([jnp.where(comp0, qn, 0.0), jnp.where(comp0, 0.0, qn)], axis=0).astype(BF16)
        return lax.dot_general(qq, kn_ref[g], dims, preferred_element_type=F32)

    def softmax_mix(s):
        s0, s1 = s[0:rb], s[rb:2 * rb]
        e0 = jnp.exp2(s0 - jnp.max(s0, axis=-1, keepdims=True))
        e1 = jnp.exp2(s1 - jnp.max(s1, axis=-1, keepdims=True))
        l0 = jnp.sum(e0, axis=-1, keepdims=True)
        l1 = jnp.sum(e1, axis=-1, keepdims=True)
        return e0.astype(BF16) - (lam * l0 / l1).astype(BF16) * e1.astype(BF16), l0

    def values(g, r0, a, l0):
        o = jnp.dot(a, v_ref[g], preferred_element_type=F32) * (1.0 / l0)
        o_ref[r0:r0 + rb, g * DA_VDIM:(g + 1) * DA_VDIM] = (
            _rms(o) * sg_ref[...] * (1.0 - lambda_init)).astype(BF16)

    s_of, a_of = {}, {}
    for i in range(len(units) + 2):
        if i < len(units):
            s_of[i] = scores(*units[i])
        if 0 <= i - 1 < len(units):
            a_of[i - 1] = softmax_mix(s_of.pop(i - 1))
        if 0 <= i - 2 < len(units):
            values(*units[i - 2], *a_of.pop(i - 2))


def _rope_tables(n_tokens):
    rows = n_tokens // GRID_W
    row = jnp.repeat(jnp.arange(rows, dtype=F32), GRID_W)
    col = jnp.tile(jnp.arange(GRID_W, dtype=F32), rows)
    n_freq = DA_HEAD_DIM // 4
    inv_freq = jnp.exp(-math.log(ROPE_BASE) * jnp.arange(n_freq, dtype=F32) / n_freq)
    ang = jnp.concatenate([row[:, None] * inv_freq, col[:, None] * inv_freq], axis=-1)
    cos, sin = jnp.cos(ang), jnp.sin(ang)
    return jnp.tile(cos, (1, 4)), jnp.tile(jnp.concatenate([-sin, sin], axis=-1), (1, 2))


def _attention(q, k, v, n_ctx, q_gain, k_gain, lam_vecs, sub_gain, lambda_init):
    nb, nh, n_tot, dv = k.shape
    n_lat = n_tot - n_ctx
    tq = Q_TILE
    cos_t, sin_t = _rope_tables(n_lat)
    gain2 = lambda gv: jnp.tile(gv, 2).reshape(1, dv)
    lam_pad = jnp.zeros((SUBLANES, dv), F32).at[0:4, 0:DA_HEAD_DIM].set(lam_vecs)
    kern = functools.partial(_attn_kernel, n_ctx=n_ctx, lambda_init=lambda_init)
    const = lambda shape: pl.BlockSpec(shape, lambda b, h, i: (0, 0))
    hg = HEADS_PER_STEP
    return pl.pallas_call(
        kern,
        out_shape=jax.ShapeDtypeStruct((nb, n_lat, nh * dv), BF16),
        grid=(nb, nh // hg, n_lat // tq),
        in_specs=[pl.BlockSpec((None, hg, tq, dv), lambda b, h, i: (b, h, i, 0)),
                  pl.BlockSpec((None, hg, n_tot, dv), lambda b, h, i: (b, h, 0, 0)),
                  pl.BlockSpec((None, hg, n_tot, dv), lambda b, h, i: (b, h, 0, 0)),
                  const((1, dv)), const((1, dv)),
                  pl.BlockSpec((tq, dv), lambda b, h, i: (i, 0)),
                  pl.BlockSpec((tq, dv), lambda b, h, i: (i, 0)),
                  const((n_lat, dv)), const((n_lat, dv)),
                  const((SUBLANES, dv)), const((1, dv))],
        out_specs=pl.BlockSpec((None, tq, hg * dv), lambda b, h, i: (b, i, h)),
        scratch_shapes=[pltpu.VMEM((hg, n_tot, dv), BF16)],
        compiler_params=_cparams(("parallel", "parallel", "arbitrary")),
        name="attention",
    )(q, k, v, gain2(q_gain), gain2(k_gain), cos_t, sin_t, cos_t, sin_t, lam_pad, sub_gain.reshape(1, dv))


def _oproj_kernel(o_ref, w_ref, x_ref, mod_ref, out_ref):
    b = pl.program_id(0)
    y = jnp.dot(o_ref[...], w_ref[...], preferred_element_type=F32)
    out_ref[...] = x_ref[...] + mod_ref[pl.ds(b, 1), :] * y


def _oproj(o, w_o, xs, n_ctx, mod_gt):
    nb, n_lat, d = o.shape
    r = ROW_TILE
    off = n_ctx // r
    return pl.pallas_call(
        _oproj_kernel,
        out_shape=jax.ShapeDtypeStruct((nb, n_lat, d), F32),
        grid=(nb, n_lat // r),
        in_specs=[pl.BlockSpec((None, r, d), lambda b, j: (b, j, 0)),
                  pl.BlockSpec((d, d), lambda b, j: (0, 0)),
                  pl.BlockSpec((None, r, d), lambda b, j: (b, j + off, 0)),
                  pl.BlockSpec((MOD_ROWS, d), lambda b, j: (0, 0))],
        out_specs=pl.BlockSpec((None, r, d), lambda b, j: (b, j, 0)),
        compiler_params=_cparams(("parallel", "parallel")),
        name="oproj",
    )(o, w_o.astype(BF16), xs, mod_gt)


def _attn_layer(xs, n_ctx, mod, g_mix, w_qkv, w_o, q_gain, k_gain, lq1, lk1, lq2, lk2, sub_gain, lambda_init):
    d = xs.shape[-1]
    q, k, v = _qkv(xs, n_ctx, mod[:, 0:2 * d], g_mix, w_qkv)
    o = _attention(q, k, v, n_ctx, q_gain, k_gain, jnp.stack([lq1, lk1, lq2, lk2]), sub_gain, lambda_init)
    return _oproj(o, w_o, xs, n_ctx, mod[:, 2 * d:3 * d])


def kernel(x, c, ctx, c_ctx, w_ada, b_ada, g_mix, g_ffn, s5_a_re, s5_a_im, s5_log_dt, s5_b_re, s5_b_im, s5_c_re, s5_c_im, s5_d, s5_w_glu, s5_b_glu, da_w_qkv, da_w_o, da_q_gain, da_k_gain, da_lam_q1, da_lam_k1, da_lam_q2, da_lam_k2, da_sub_gain, moe_w_router, moe_b_router, moe_w_gu, moe_b_gu, moe_w_down, moe_b_down):
    nb, n_lat, d = x.shape
    n_ctx = ctx.shape[1]
    depth = w_ada.shape[0]
    assert depth == 2 and nb % SUBLANES == 0 and nb + 1 <= MOD_ROWS
    cc = jnp.concatenate([c, c_ctx[None], jnp.zeros((MOD_ROWS - nb - 1, d), F32)], axis=0)
    mod = _ada(cc, w_ada, b_ada)
    xs = _s5_layer(ctx, x, mod[0], g_mix[0], s5_a_re[0], s5_a_im[0], s5_log_dt[0], s5_b_re[0], s5_b_im[0],
                   s5_c_re[0], s5_c_im[0], s5_d[0], s5_w_glu[0], s5_b_glu[0])
    xs = _moe_layer(xs, n_ctx, mod[0][:, 3 * d:5 * d], mod[0][:, 5 * d:6 * d], g_ffn[0], 0,
                    moe_w_router, moe_b_router, moe_w_gu, moe_b_gu, moe_w_down, moe_b_down)
    lambda_init = 0.8 - 0.6 * math.exp(-0.3 * 1)
    x1 = _attn_layer(xs, n_ctx, mod[1], g_mix[1], da_w_qkv[0], da_w_o[0], da_q_gain[0], da_k_gain[0],
                     da_lam_q1[0], da_lam_k1[0], da_lam_q2[0], da_lam_k2[0], da_sub_gain[0], lambda_init)
    return _moe_layer(x1, 0, mod[1][:, 3 * d:5 * d], mod[1][:, 5 * d:6 * d], g_ffn[1], 1,
                      moe_w_router, moe_b_router, moe_w_gu, moe_b_gu, moe_w_down, moe_b_down)
```

```python
import functools
import math

import jax
import jax.numpy as jnp
from jax import lax
from jax.experimental import pallas as pl
from jax.experimental.pallas import tpu as pltpu

NORM_EPS = 1e-6
N_MIXERS = 2
S5_GROUP = 16
S5_STATE = 64
DA_HEAD_DIM = 64
DA_VDIM = 2 * DA_HEAD_DIM
GRID_W = 64
ROPE_BASE = 10000.0
N_EXPERTS = 32
TOP_K = 4
SWIGLU_LIMIT = 7.0
SWIGLU_ALPHA = 1.702
LOG2_E = 1.4426950408889634

LANES = 128
SUBLANES = 8
VMEM_LIMIT_BYTES = 56 * 1024 * 1024

S5_SLAB_GROUPS = LANES // S5_GROUP
S5_SLAB_STATE = S5_SLAB_GROUPS * S5_STATE
S5_CHUNK = 64
S5_ROW_BLOCKS = 4
TM_TILE = 32
ROW_TILE = 256
ROUTER_TILE = 512
EXPERT_ROWS = 512
COMBINE_TILE = 256
Q_TILE = 512
ATTN_ROW_BLOCK = 128
HEADS_PER_STEP = 2
MOD_ROWS = 24

F32 = jnp.float32
BF16 = jnp.bfloat16


def _cparams(sem, vmem=VMEM_LIMIT_BYTES):
    return pltpu.CompilerParams(dimension_semantics=sem, vmem_limit_bytes=vmem)


def _rms(x):
    return x * lax.rsqrt(jnp.mean(x * x, axis=-1, keepdims=True) + NORM_EPS)


def _ada_kernel(c_ref, w_ref, b_ref, o_ref):
    cv = c_ref[...]
    s = cv * jax.nn.sigmoid(cv)
    o_ref[...] = jnp.dot(s, w_ref[...], precision=lax.Precision.HIGHEST,
                         preferred_element_type=F32) + b_ref[...]


def _ada(cc, w_ada, b_ada):
    depth, d, d6 = w_ada.shape
    nj = d6 // d
    return pl.pallas_call(
        _ada_kernel,
        out_shape=jax.ShapeDtypeStruct((depth, MOD_ROWS, d6), F32),
        grid=(depth, nj),
        in_specs=[pl.BlockSpec((MOD_ROWS, d), lambda i, j: (0, 0)),
                  pl.BlockSpec((None, d, d), lambda i, j: (i, 0, j)),
                  pl.BlockSpec((None, 1, d), lambda i, j: (i, 0, j))],
        out_specs=pl.BlockSpec((None, MOD_ROWS, d), lambda i, j: (i, 0, j)),
        compiler_params=_cparams(("parallel", "parallel")),
        name="ada",
    )(cc, w_ada, b_ada.reshape(depth, 1, d6))


def _prenorm_tm_kernel(ctx_ref, x_ref, mod_ref, g_ref, o_ref, *, n_ctx_tiles, nb, d):
    i = pl.program_id(0)
    tt = o_ref.shape[1] // nb
    g = g_ref[...]

    def emit(src_ref, mod_row):
        for b in range(nb):
            r = mod_row(b)
            sh = mod_ref[r:r + 1, 0:d]
            sc = mod_ref[r:r + 1, d:2 * d]
            h = (_rms(src_ref[b]) * g) * (1.0 + sc) + sh
            for s in range(d // LANES):
                o_ref[s, pl.ds(b, tt, stride=nb), :] = h[:, s * LANES:(s + 1) * LANES]

    @pl.when(i < n_ctx_tiles)
    def _():
        emit(ctx_ref, lambda b: nb)

    @pl.when(i >= n_ctx_tiles)
    def _():
        emit(x_ref, lambda b: b)


def _prenorm_tm(ctx, x, mod, g):
    nb, n_ctx, d = ctx.shape
    n_lat = x.shape[1]
    tt = TM_TILE
    nct, nlt = n_ctx // tt, n_lat // tt
    kern = functools.partial(_prenorm_tm_kernel, n_ctx_tiles=nct, nb=nb, d=d)
    return pl.pallas_call(
        kern,
        out_shape=jax.ShapeDtypeStruct((d // LANES, (n_ctx + n_lat) * nb, LANES), F32),
        grid=(nct + nlt,),
        in_specs=[pl.BlockSpec((nb, tt, d), lambda i: (0, jnp.minimum(i, nct - 1), 0)),
                  pl.BlockSpec((nb, tt, d), lambda i: (0, jnp.maximum(i - nct, 0), 0)),
                  pl.BlockSpec((MOD_ROWS, 2 * d), lambda i: (0, 0)),
                  pl.BlockSpec((1, d), lambda i: (0, 0))],
        out_specs=pl.BlockSpec((d // LANES, tt * nb, LANES), lambda i: (0, i, 0)),
        compiler_params=_cparams(("parallel",)),
        name="prenorm_tm",
    )(ctx, x, mod, g)


def _s5_disc_kernel(ar_ref, ai_ref, ldt_ref, br_ref, bi_ref, lr_ref, li_ref, bbr_ref, bbi_ref):
    ar, ai = ar_ref[...], ai_ref[...]
    dt = jnp.exp(ldt_ref[...])
    mag = jnp.exp(ar * dt)
    lr = mag * jnp.cos(ai * dt)
    li = mag * jnp.sin(ai * dt)
    nr, ni = lr - 1.0, li
    den = ar * ar + ai * ai
    qr = (nr * ar + ni * ai) / den
    qi = (ni * ar - nr * ai) / den
    br, bi = br_ref[...], bi_ref[...]
    lr_ref[...] = lr
    li_ref[...] = li
    bbr_ref[...] = qr * br - qi * bi
    bbi_ref[...] = qr * bi + qi * br


def _s5_disc(a_re, a_im, log_dt, b_re, b_im):
    nd, g, p = a_re.shape
    cg = b_re.shape[-1]
    gp = g * p
    col = lambda a: a.reshape(nd, gp, 1)
    ldt = jnp.broadcast_to(log_dt[:, :, None], (nd, g, p))
    spec1 = pl.BlockSpec((None, gp, 1), lambda i: (i, 0, 0))
    specb = pl.BlockSpec((None, gp, cg), lambda i: (i, 0, 0))
    lr, li, bbr, bbi = pl.pallas_call(
        _s5_disc_kernel,
        out_shape=[jax.ShapeDtypeStruct((nd, gp, 1), F32)] * 2 + [jax.ShapeDtypeStruct((nd, gp, cg), F32)] * 2,
        grid=(nd,),
        in_specs=[spec1, spec1, spec1, specb, specb],
        out_specs=[spec1, spec1, specb, specb],
        compiler_params=_cparams(("parallel",)),
        name="s5_disc",
    )(col(a_re), col(a_im), col(ldt), b_re.reshape(nd, gp, cg), b_im.reshape(nd, gp, cg))
    return (lr.reshape(nd, g, p), li.reshape(nd, g, p),
            bbr.reshape(nd, g, p, cg), bbi.reshape(nd, g, p, cg))


def _s5_slab_params(lam_re, lam_im, bb_re, bb_im, c_re, c_im):
    nd, g, p = lam_re.shape
    cg = bb_re.shape[-1]
    ns, sg = g // S5_SLAB_GROUPS, S5_SLAB_GROUPS
    eye = jnp.eye(sg, dtype=F32)

    def b_mat(bb):
        bb = bb.reshape(nd, ns, sg, p, cg)
        return jnp.einsum('dsjpc,jk->dsjckp', bb, eye).reshape(nd, ns, sg * cg, sg * p)

    def c_mat(cm):
        cm = cm.reshape(nd, ns, sg, cg, p)
        return jnp.einsum('dsjcp,jk->dskpjc', cm, eye).reshape(nd, ns, sg * p, sg * cg)

    bm = jnp.concatenate([b_mat(bb_re), b_mat(bb_im)], axis=-1).astype(BF16)
    lam = lambda a: a.reshape(nd, ns, 1, sg * p)
    return bm, c_mat(c_re).astype(BF16), c_mat(c_im).astype(BF16), lam(lam_re), lam(lam_im)


def _s5_scan_kernel(u_ref, bm_ref, cr_ref, ci_ref, lr_ref, li_ref, y_ref, x0_ref, x1_ref, h0_ref, h1_ref, st_ref):
    d = pl.program_id(0)
    c = pl.program_id(2)
    t_len, nb = u_ref.shape[0], u_ref.shape[1] * u_ref.shape[2]
    ns = S5_SLAB_STATE
    xs, hs = (x0_ref, x1_ref), (h0_ref, h1_ref)

    @pl.when(c == 0)
    def _():
        for ref in xs + hs:
            ref[...] = jnp.zeros_like(ref)

    def stages(in_ref, bu_ref, h_ref, out_ref):
        lr = jnp.broadcast_to(lr_ref[...], (SUBLANES, ns))
        li = jnp.broadcast_to(li_ref[...], (SUBLANES, ns))
        fresh = c <= 1
        state = [(jnp.where(fresh, 0.0, st_ref[0, o]), jnp.where(fresh, 0.0, st_ref[1, o]))
                 for o in range(nb // SUBLANES)]
        tb = t_len // S5_ROW_BLOCKS
        for j in range(S5_ROW_BLOCKS):
            rsl = slice(j * tb * nb, (j + 1) * tb * nb)
            u = u_ref[j * tb:(j + 1) * tb].reshape(tb * nb, LANES).astype(BF16)
            in_ref[rsl, :] = jnp.dot(u, bm_ref[...], preferred_element_type=F32)

            for s in range(j * tb, (j + 1) * tb):
                t = jnp.where(d == 0, s, t_len - 1 - s)
                for o in range(nb // SUBLANES):
                    hr, hi = state[o]
                    row = pl.multiple_of(t * nb + o * SUBLANES, SUBLANES)
                    br = bu_ref[pl.ds(row, SUBLANES), 0:ns]
                    bi = bu_ref[pl.ds(row, SUBLANES), ns:2 * ns]
                    nr = lr * hr - li * hi + br
                    ni = lr * hi + li * hr + bi
                    h_ref[pl.ds(row, SUBLANES), 0:ns] = nr
                    h_ref[pl.ds(row, SUBLANES), ns:2 * ns] = ni
                    state[o] = (nr, ni)

            y = (jnp.dot(out_ref[rsl, 0:ns].astype(BF16), cr_ref[...], preferred_element_type=F32)
                 - jnp.dot(out_ref[rsl, ns:2 * ns].astype(BF16), ci_ref[...], preferred_element_type=F32))
            y_ref[j * tb:(j + 1) * tb] = y.reshape((tb,) + y_ref.shape[1:])
        for o in range(nb // SUBLANES):
            st_ref[0, o] = state[o][0]
            st_ref[1, o] = state[o][1]

    phase = lax.rem(c, 2)
    for r in range(2):
        @pl.when(phase == r)
        def _(r=r):
            stages(xs[r], xs[1 - r], hs[1 - r], hs[r])


def _s5_scan(u_t, bm, cr, ci, lam_r, lam_i, nb, n_ctx, n_tot):
    nslab = u_t.shape[0]
    nbo = nb // SUBLANES
    t_len = S5_CHUNK
    ncc, nc = n_ctx // t_len, n_tot // t_len
    u5 = u_t.reshape(nslab, n_tot, nbo, SUBLANES, LANES)

    def chunk(d, c):
        c = jnp.clip(c, 0, nc - 1)
        back = jnp.where(c < ncc, ncc - 1 - c, ncc + nc - 1 - c)
        return jnp.where(d == 0, c, back)

    pspec = lambda r, k: pl.BlockSpec((None, None, r, k), lambda d, s, c: (d, s, 0, 0))
    rows = t_len * nb
    return pl.pallas_call(
        _s5_scan_kernel,
        out_shape=jax.ShapeDtypeStruct((2, nslab, n_tot, nbo, SUBLANES, LANES), F32),
        grid=(2, nslab, nc + 2),
        in_specs=[pl.BlockSpec((None, t_len, nbo, SUBLANES, LANES), lambda d, s, c: (s, chunk(d, c), 0, 0, 0)),
                  pspec(LANES, 2 * S5_SLAB_STATE), pspec(S5_SLAB_STATE, LANES), pspec(S5_SLAB_STATE, LANES),
                  pspec(1, S5_SLAB_STATE), pspec(1, S5_SLAB_STATE)],
        out_specs=pl.BlockSpec((None, None, t_len, nbo, SUBLANES, LANES),
                               lambda d, s, c: (d, s, chunk(d, c - 2), 0, 0, 0)),
        scratch_shapes=[pltpu.VMEM((rows, 2 * S5_SLAB_STATE), F32)] * 4
                       + [pltpu.VMEM((2, nbo, SUBLANES, S5_SLAB_STATE), F32)],
        compiler_params=_cparams(("parallel", "parallel", "arbitrary")),
        name="s5_scan",
    )(u5, bm, cr, ci, lam_r, lam_i)


def _s5_out_kernel(y_ref, u_ref, dsk_ref, w_ref, bg_ref, ctx_ref, x_ref, mod_ref, o_ref, tr_ref,
                   *, n_ctx_tiles, nb, d):
    i = pl.program_id(0)
    nslab = d // LANES
    rows = u_ref.shape[1]
    tt = rows // nb
    y = (y_ref[0] + y_ref[1]).reshape(nslab, rows, LANES)
    v = jax.nn.gelu(y + dsk_ref[...] * u_ref[...])
    lhs = jnp.concatenate([v[s] for s in range(nslab)], axis=-1).astype(BF16)
    z = jnp.dot(lhs, w_ref[...], preferred_element_type=F32) + bg_ref[...]
    m = z[:, 0:d] * jax.nn.sigmoid(z[:, d:2 * d])
    for s in range(nslab):
        tr_ref[s] = m[:, s * LANES:(s + 1) * LANES]

    def emit(src_ref, mod_row):
        for b in range(nb):
            r = mod_row(b)
            gt = mod_ref[r:r + 1, :]
            mb = jnp.concatenate([tr_ref[s, pl.ds(b, tt, stride=nb), :] for s in range(nslab)], axis=-1)
            o_ref[b] = src_ref[b] + gt * mb

    @pl.when(i < n_ctx_tiles)
    def _():
        emit(ctx_ref, lambda b: nb)

    @pl.when(i >= n_ctx_tiles)
    def _():
        emit(x_ref, lambda b: b)


def _s5_out(y_t, u_t, d_skip, w_glu, b_glu, ctx, x, gt1):
    nb, n_ctx, d = ctx.shape
    n_lat = x.shape[1]
    n_tot = n_ctx + n_lat
    nslab, nbo = d // LANES, nb // SUBLANES
    tt = TM_TILE
    nct, nlt = n_ctx // tt, n_lat // tt
    kern = functools.partial(_s5_out_kernel, n_ctx_tiles=nct, nb=nb, d=d)
    return pl.pallas_call(
        kern,
        out_shape=jax.ShapeDtypeStruct((nb, n_tot, d), F32),
        grid=(nct + nlt,),
        in_specs=[pl.BlockSpec((2, nslab, tt, nbo, SUBLANES, LANES), lambda i: (0, 0, i, 0, 0, 0)),
                  pl.BlockSpec((nslab, tt * nb, LANES), lambda i: (0, i, 0)),
                  pl.BlockSpec((nslab, 1, LANES), lambda i: (0, 0, 0)),
                  pl.BlockSpec((d, 2 * d), lambda i: (0, 0)),
                  pl.BlockSpec((1, 2 * d), lambda i: (0, 0)),
                  pl.BlockSpec((nb, tt, d), lambda i: (0, jnp.minimum(i, nct - 1), 0)),
                  pl.BlockSpec((nb, tt, d), lambda i: (0, jnp.maximum(i - nct, 0), 0)),
                  pl.BlockSpec((MOD_ROWS, d), lambda i: (0, 0))],
        out_specs=pl.BlockSpec((nb, tt, d), lambda i: (0, i, 0)),
        scratch_shapes=[pltpu.VMEM((nslab, tt * nb, LANES), F32)],
        compiler_params=_cparams(("parallel",)),
        name="s5_out",
    )(y_t, u_t, d_skip.reshape(nslab, 1, LANES), w_glu.astype(BF16), b_glu.reshape(1, 2 * d), ctx, x, gt1)


def _s5_layer(xs_ctx, xs_lat, mod, g_mix, a_re, a_im, log_dt, b_re, b_im, c_re, c_im, d_skip, w_glu, b_glu):
    nb, n_ctx, d = xs_ctx.shape
    n_tot = n_ctx + xs_lat.shape[1]
    u_t = _prenorm_tm(xs_ctx, xs_lat, mod[:, 0:2 * d], g_mix.reshape(1, d))
    lam_re, lam_im, bb_re, bb_im = _s5_disc(a_re, a_im, log_dt, b_re, b_im)
    bm, cr, ci, lam_r, lam_i = _s5_slab_params(lam_re, lam_im, bb_re, bb_im, c_re, c_im)
    y_t = _s5_scan(u_t, bm, cr, ci, lam_r, lam_i, nb, n_ctx, n_tot)
    return _s5_out(y_t, u_t, d_skip, w_glu, b_glu, xs_ctx, xs_lat, mod[:, 2 * d:3 * d])


NEG_PAD = -1e30
NEG_MASK = -3e38


def _split_bf16(a):
    hi = a.astype(BF16)
    return hi, (a - hi.astype(F32)).astype(BF16)


def _router_kernel(x_ref, mod_ref, g_ref, w_ref, bias_ref, h_ref, sel_ref, selt_ref, meta_ref, cnt_ref,
                   carry_ref, *, n_ctx_tiles, nb, d):
    b = pl.program_id(0)
    j = pl.program_id(1)
    rows = x_ref.shape[0]

    @pl.when((b == 0) & (j == 0))
    def _():
        carry_ref[...] = jnp.zeros_like(carry_ref)

    row = jnp.where(j < n_ctx_tiles, nb, b)
    mrow = mod_ref[pl.ds(row, 1), :]
    h = (_rms(x_ref[...]) * g_ref[...]) * (1.0 + mrow[:, d:2 * d]) + mrow[:, 0:d]
    h_ref[...] = h.astype(BF16)

    hi, lo = _split_bf16(h)
    whi, wlo = _split_bf16(w_ref[...])
    dot = functools.partial(jnp.dot, preferred_element_type=F32)
    logits = dot(hi, whi) + dot(lo, whi) + dot(hi, wlo) + bias_ref[...]

    lane = lax.broadcasted_iota(jnp.int32, (rows, LANES), 1)
    lane_f = lane.astype(F32)
    work = logits
    tops, idxs, hots = [], [], []
    for _ in range(TOP_K):
        m = jnp.max(work, axis=-1, keepdims=True)
        idx = jnp.min(jnp.where(work == m, lane_f, float(LANES)), axis=-1, keepdims=True)
        hot = lane_f == idx
        work = jnp.where(hot, NEG_MASK, work)
        tops.append(m)
        idxs.append(idx)
        hots.append(hot)
    exps = [jnp.exp(m - tops[0]) for m in tops]
    denom = exps[0] + exps[1] + exps[2] + exps[3]

    multi = jnp.zeros((rows, LANES), F32)
    for hot in hots:
        multi = multi + jnp.where(hot, 1.0, 0.0)
    r_i = lax.broadcasted_iota(jnp.int32, (rows, rows), 0)
    c_i = lax.broadcasted_iota(jnp.int32, (rows, rows), 1)
    tri = jnp.where(c_i < r_i, 1.0, 0.0).astype(BF16)
    in_tile = dot(tri, multi.astype(BF16))
    cnt_tile = jnp.sum(multi, axis=0, keepdims=True)
    e_r = lax.broadcasted_iota(jnp.int32, (LANES, LANES), 0)
    e_c = lax.broadcasted_iota(jnp.int32, (LANES, LANES), 1)
    below = jnp.where(e_r < e_c, 1.0, 0.0).astype(BF16)
    prefix = dot(jnp.broadcast_to(cnt_tile, (SUBLANES, LANES)).astype(BF16), below)[0:1, :]
    slot = in_tile + prefix

    sel = jnp.zeros((rows, LANES), F32)
    for k in range(TOP_K):
        pos = jnp.sum(jnp.where(hots[k], slot, 0.0), axis=-1, keepdims=True)
        sel = jnp.where(lane == k, idxs[k].astype(F32), sel)
        sel = jnp.where(lane == TOP_K + k, exps[k] / denom, sel)
        sel = jnp.where(lane == 2 * TOP_K + k, pos, sel)
    sel_ref[...] = sel
    selt_ref[...] = sel.T[2 * TOP_K:2 * TOP_K + SUBLANES, :].astype(jnp.int32)

    carry = carry_ref[0:1, :]
    sub = lax.broadcasted_iota(jnp.int32, (SUBLANES, LANES), 0)
    meta = jnp.where(sub == 0, cnt_tile, jnp.where(sub == 1, carry, jnp.where(sub == 2, prefix, 0.0)))
    meta_ref[...] = meta.astype(jnp.int32)
    carry_ref[0:1, :] = carry + cnt_tile
    cnt_ref[...] = jnp.broadcast_to(carry + cnt_tile, cnt_ref.shape).astype(jnp.int32)


def _router(xs, n_ctx, mod_shsc, g, w_router, b_router):
    nb, t_len, d = xs.shape
    r = ROW_TILE
    nj = t_len // r
    ne = w_router.shape[1]
    w_pad = jnp.zeros((d, LANES), F32).at[:, :ne].set(w_router)
    b_pad = jnp.full((1, LANES), NEG_PAD, F32).at[0, :ne].set(b_router)
    kern = functools.partial(_router_kernel, n_ctx_tiles=n_ctx // r, nb=nb, d=d)
    tile = lambda b, j: (b * nj + j, 0)
    return pl.pallas_call(
        kern,
        out_shape=[jax.ShapeDtypeStruct((nb * t_len, d), BF16),
                   jax.ShapeDtypeStruct((nb * t_len, LANES), F32),
                   jax.ShapeDtypeStruct((nb * nj * SUBLANES, r), jnp.int32),
                   jax.ShapeDtypeStruct((nb * nj, SUBLANES, LANES), jnp.int32),
                   jax.ShapeDtypeStruct((SUBLANES, LANES), jnp.int32)],
        grid=(nb, nj),
        in_specs=[pl.BlockSpec((None, r, d), lambda b, j: (b, j, 0)),
                  pl.BlockSpec((MOD_ROWS, 2 * d), lambda b, j: (0, 0)),
                  pl.BlockSpec((1, d), lambda b, j: (0, 0)),
                  pl.BlockSpec((d, LANES), lambda b, j: (0, 0)),
                  pl.BlockSpec((1, LANES), lambda b, j: (0, 0))],
        out_specs=[pl.BlockSpec((r, d), tile),
                   pl.BlockSpec((r, LANES), tile),
                   pl.BlockSpec((SUBLANES, r), tile),
                   pl.BlockSpec((None, SUBLANES, LANES), lambda b, j: (b * nj + j, 0, 0)),
                   pl.BlockSpec((SUBLANES, LANES), lambda b, j: (0, 0))],
        scratch_shapes=[pltpu.VMEM((SUBLANES, LANES), F32)],
        compiler_params=_cparams(("arbitrary", "arbitrary")),
        name="router",
    )(xs, mod_shsc, g.reshape(1, d), w_pad, b_pad)


def _plan_kernel(cnt_ref, ps_ref, be_ref, first_ref, nused_ref, *, ne, rb):
    n_blocks = be_ref.shape[0]

    def fill(i, carry):
        be_ref[i] = ne - 1
        first_ref[i] = 0
        return carry
    lax.fori_loop(0, n_blocks, fill, 0)

    def per_expert(e, acc):
        nblk = (cnt_ref[0, e] + (rb - 1)) // rb
        ps_ref[e] = acc
        b0 = acc // rb

        def mark(i, carry):
            be_ref[b0 + i] = e
            first_ref[b0 + i] = jnp.where(i == 0, 1, 0)
            return carry
        lax.fori_loop(0, nblk, mark, 0)
        return acc + nblk * rb

    total = lax.fori_loop(0, ne, per_expert, 0)

    def tail(e, carry):
        ps_ref[e] = total
        return carry
    lax.fori_loop(ne, ps_ref.shape[0], tail, 0)
    nused_ref[0] = total // rb


def _plan(cnt, ne, n_blocks):
    smem = pl.BlockSpec(memory_space=pltpu.SMEM)
    return pl.pallas_call(
        functools.partial(_plan_kernel, ne=ne, rb=EXPERT_ROWS),
        out_shape=[jax.ShapeDtypeStruct((LANES,), jnp.int32),
                   jax.ShapeDtypeStruct((n_blocks,), jnp.int32),
                   jax.ShapeDtypeStruct((n_blocks,), jnp.int32),
                   jax.ShapeDtypeStruct((1,), jnp.int32)],
        in_specs=[smem],
        out_specs=[smem, smem, smem, smem],
        name="plan",
    )(cnt)


_SEG_SIZES = tuple(ROW_TILE >> s for s in range(ROW_TILE.bit_length()))
TILE_SLOTS = ROW_TILE * TOP_K
TOKEN_ROWS = SUBLANES


def _segment_copies(n, src, dst, make_copy, wait=False):
    for p in _SEG_SIZES:
        take = (n & p) != 0

        @pl.when(take)
        def _():
            cp = make_copy(pl.multiple_of(src * TOKEN_ROWS, TOKEN_ROWS),
                           pl.multiple_of(dst * TOKEN_ROWS, TOKEN_ROWS), p * TOKEN_ROWS)
            cp.wait() if wait else cp.start()
        step = jnp.where(take, p, 0)
        src = src + step
        dst = dst + step


def _tile_major(x):
    return [x[:, s * LANES:(s + 1) * LANES] for s in range(x.shape[1] // LANES)]


def _dispatch_kernel(meta_ref, metap_ref, ps_ref, cnt_ref, h_ref, selt_ref, xbuf_ref, xs_ref, zero_ref, sem,
                     zsem, *, ne, rb):
    rows = h_ref.shape[0]
    slots = xs_ref.shape[1] // TOKEN_ROWS

    @pl.when(pl.program_id(0) == 0)
    def _():
        zero_ref[...] = jnp.zeros_like(zero_ref)

        def pad_copy(_, dst, p):
            return pltpu.make_async_copy(zero_ref.at[pl.ds(0, p), :], xbuf_ref.at[pl.ds(dst, p), :], zsem)

        for wait in (False, True):
            def pad_rows(e, carry, wait=wait):
                cnt = cnt_ref[0, e]
                _segment_copies((-cnt) & (rb - 1), 0, ps_ref[e] + cnt, pad_copy, wait)
                return carry
            lax.fori_loop(0, ne, pad_rows, 0)

    pos = selt_ref[0:TOP_K, :]
    s_iota = lax.broadcasted_iota(jnp.int32, (slots, rows), 0)
    perm = jnp.zeros((slots, rows), F32)
    for k in range(TOP_K):
        perm = perm + jnp.where(s_iota == pos[k:k + 1, :], 1.0, 0.0)
    step = pl.program_id(0)
    slot = lax.rem(step, 2)
    sorted_rows = jnp.dot(perm.astype(BF16), h_ref[...], preferred_element_type=F32)
    for s, slab in enumerate(_tile_major(sorted_rows)):
        xs_ref[slot, pl.ds(s, slots, stride=TOKEN_ROWS), :] = slab

    def segments(mref, which, wait):
        def seg_copy(src, dst, p):
            return pltpu.make_async_copy(xs_ref.at[which, pl.ds(src, p), :], xbuf_ref.at[pl.ds(dst, p), :],
                                         sem.at[which])

        def per_expert(e, carry):
            _segment_copies(mref[0, e], mref[2, e], ps_ref[e] + mref[1, e], seg_copy, wait)
            return carry
        lax.fori_loop(0, ne, per_expert, 0)

    segments(meta_ref, slot, False)

    @pl.when(step > 0)
    def _():
        segments(metap_ref, 1 - slot, True)

    @pl.when(step == pl.num_programs(0) - 1)
    def _():
        segments(meta_ref, slot, True)


def _dispatch(h2, selt, meta, ps, cnt, ne, n_buf):
    n_tok, d = h2.shape
    r = ROW_TILE
    smem = pl.BlockSpec(memory_space=pltpu.SMEM)
    return pl.pallas_call(
        functools.partial(_dispatch_kernel, ne=ne, rb=EXPERT_ROWS),
        out_shape=jax.ShapeDtypeStruct((n_buf * TOKEN_ROWS, LANES), F32),
        grid=(n_tok // r,),
        in_specs=[pl.BlockSpec((None, SUBLANES, LANES), lambda i: (i, 0, 0), memory_space=pltpu.SMEM),
                  pl.BlockSpec((None, SUBLANES, LANES), lambda i: (jnp.maximum(i - 1, 0), 0, 0),
                               memory_space=pltpu.SMEM),
                  smem, smem,
                  pl.BlockSpec((r, d), lambda i: (i, 0)),
                  pl.BlockSpec((SUBLANES, r), lambda i: (i, 0))],
        out_specs=pl.BlockSpec(memory_space=pl.ANY),
        scratch_shapes=[pltpu.VMEM((2, TILE_SLOTS * TOKEN_ROWS, LANES), F32),
                        pltpu.VMEM((_SEG_SIZES[0] * TOKEN_ROWS, LANES), F32),
                        pltpu.SemaphoreType.DMA((2,)), pltpu.SemaphoreType.DMA(())],
        compiler_params=_cparams(("arbitrary",)),
        name="dispatch",
    )(meta, meta, ps, cnt, h2, selt)


def _expert_kernel(be_ref, first_ref, nused_ref, x_ref, wgu_ref, bgu_ref, wd_ref, bd_ref, y_ref,
                   wgu_bf, wd_bf):
    i = pl.program_id(0)
    dff = wd_ref.shape[0]

    @pl.when(i < nused_ref[0])
    def _():
        @pl.when(first_ref[i] == 1)
        def _():
            wgu_bf[...] = wgu_ref[...].astype(BF16)
            wd_bf[...] = wd_ref[...].astype(BF16)

        rows = x_ref.shape[0] // TOKEN_ROWS
        x = jnp.concatenate([x_ref[pl.ds(s, rows, stride=TOKEN_ROWS), :] for s in range(TOKEN_ROWS)], axis=-1)
        gu = jnp.dot(x.astype(BF16), wgu_bf[...], preferred_element_type=F32) + bgu_ref[...]
        gate = jnp.minimum(gu[:, 0:dff], SWIGLU_LIMIT)
        up = jnp.clip(gu[:, dff:2 * dff], -SWIGLU_LIMIT, SWIGLU_LIMIT)
        act = (up + 1.0) * gate * jax.nn.sigmoid(SWIGLU_ALPHA * gate)
        y = jnp.dot(act.astype(BF16), wd_bf[...], preferred_element_type=F32) + bd_ref[...]
        for s, slab in enumerate(_tile_major(y)):
            y_ref[pl.ds(s, rows, stride=TOKEN_ROWS), :] = slab

    @pl.when(i >= nused_ref[0])
    def _():
        y_ref[...] = jnp.zeros_like(y_ref)


def _experts(xbuf, block_e, first, n_used, layer, w_gu, b_gu, w_down, b_down):
    depth, ne, d, dff2 = w_gu.shape
    assert d == TOKEN_ROWS * LANES
    dff = dff2 // 2
    rb = EXPERT_ROWS * TOKEN_ROWS
    blk = lambda i, nu: jnp.minimum(i, nu[0] - 1)
    wmap = lambda i, be, fi, nu: (layer, be[blk(i, nu)], 0, 0)
    grid_spec = pltpu.PrefetchScalarGridSpec(
        num_scalar_prefetch=3,
        grid=(xbuf.shape[0] // rb,),
        in_specs=[pl.BlockSpec((rb, LANES), lambda i, be, fi, nu: (blk(i, nu), 0)),
                  pl.BlockSpec((None, None, d, dff2), wmap),
                  pl.BlockSpec((None, None, 1, dff2), wmap),
                  pl.BlockSpec((None, None, dff, d), wmap),
                  pl.BlockSpec((None, None, 1, d), wmap)],
        out_specs=pl.BlockSpec((rb, LANES), lambda i, be, fi, nu: (i, 0)),
        scratch_shapes=[pltpu.VMEM((d, dff2), BF16), pltpu.VMEM((dff, d), BF16)])
    return pl.pallas_call(
        _expert_kernel,
        out_shape=jax.ShapeDtypeStruct(xbuf.shape, F32),
        grid_spec=grid_spec,
        compiler_params=_cparams(("arbitrary",)),
        name="experts",
    )(block_e, first, n_used, xbuf, w_gu, b_gu.reshape(depth, ne, 1, dff2), w_down,
      b_down.reshape(depth, ne, 1, d))


def _combine_kernel(meta_ref, metan_ref, ps_ref, sel_ref, x_ref, mod_ref, ybuf_ref, o_ref, ys_ref, sem,
                    *, n_ctx_tiles, nb, ne):
    b = pl.program_id(0)
    j = pl.program_id(1)
    nj = pl.num_programs(1)
    step = b * nj + j
    n_steps = pl.num_programs(0) * nj
    slot = lax.rem(step, 2)
    rows = x_ref.shape[0]
    slots = ys_ref.shape[1] // TOKEN_ROWS

    def segments(mref, to_slot, wait):
        def seg_copy(src, dst, p):
            return pltpu.make_async_copy(ybuf_ref.at[pl.ds(dst, p), :], ys_ref.at[to_slot, pl.ds(src, p), :],
                                         sem.at[to_slot])

        def per_expert(e, carry):
            _segment_copies(mref[0, e], mref[2, e], ps_ref[e] + mref[1, e], seg_copy, wait)
            return carry
        lax.fori_loop(0, ne, per_expert, 0)

    @pl.when(step == 0)
    def _():
        segments(meta_ref, 0, False)

    @pl.when(step + 1 < n_steps)
    def _():
        segments(metan_ref, 1 - slot, False)

    segments(meta_ref, slot, True)

    sel = sel_ref[...]
    l_iota = lax.broadcasted_iota(jnp.int32, (rows, slots), 1).astype(F32)
    w = jnp.zeros((rows, slots), F32)
    for k in range(TOP_K):
        w = w + jnp.where(l_iota == sel[:, 2 * TOP_K + k:2 * TOP_K + k + 1], sel[:, TOP_K + k:TOP_K + k + 1], 0.0)
    whi, wlo = _split_bf16(w)
    ys = jnp.concatenate([ys_ref[slot, pl.ds(s, slots, stride=TOKEN_ROWS), :] for s in range(TOKEN_ROWS)],
                         axis=-1).astype(BF16)
    acc =jnp.dot(whi, ys, preferred_element_type=F32) + jnp.dot(wlo, ys, preferred_element_type=F32)
    row = jnp.where(j < n_ctx_tiles, nb, b)
    o_ref[...] = x_ref[...] + mod_ref[pl.ds(row, 1), :] * acc


def _combine(ybuf, meta, ps, sel, xs, n_ctx, mod_gt, ne):
    nb, t_len, d = xs.shape
    r = ROW_TILE
    nj = t_len // r
    kern = functools.partial(_combine_kernel, n_ctx_tiles=n_ctx // r, nb=nb, ne=ne)
    last = nb * nj - 1
    mspec = lambda f: pl.BlockSpec((None, SUBLANES, LANES), f, memory_space=pltpu.SMEM)
    return pl.pallas_call(
        kern,
        out_shape=jax.ShapeDtypeStruct((nb, t_len, d), F32),
        grid=(nb, nj),
        in_specs=[mspec(lambda b, j: (b * nj + j, 0, 0)),
                  mspec(lambda b, j: (jnp.minimum(b * nj + j + 1, last), 0, 0)),
                  pl.BlockSpec(memory_space=pltpu.SMEM),
                  pl.BlockSpec((r, LANES), lambda b, j: (b * nj + j, 0)),
                  pl.BlockSpec((None, r, d), lambda b, j: (b, j, 0)),
                  pl.BlockSpec((MOD_ROWS, d), lambda b, j: (0, 0)),
                  pl.BlockSpec(memory_space=pl.ANY)],
        out_specs=pl.BlockSpec((None, r, d), lambda b, j: (b, j, 0)),
        scratch_shapes=[pltpu.VMEM((2, TILE_SLOTS * TOKEN_ROWS, LANES), F32), pltpu.SemaphoreType.DMA((2,))],
        compiler_params=_cparams(("arbitrary", "arbitrary")),
        name="combine",
    )(meta, meta, ps, sel, xs, mod_gt, ybuf)


def _moe_layer(xs, n_ctx, mod_shsc, mod_gt, g, layer, w_router, b_router, w_gu, b_gu, w_down, b_down):
    nb, t_len, d = xs.shape
    ne = w_router.shape[-1]
    n_tiles = nb * t_len // ROW_TILE
    n_blocks = n_tiles * TILE_SLOTS // EXPERT_ROWS + ne
    h2, sel, selt, meta, cnt = _router(xs, n_ctx, mod_shsc, g, w_router[layer], b_router[layer])
    ps, block_e, first, n_used = _plan(cnt, ne, n_blocks)
    xbuf = _dispatch(h2, selt, meta, ps, cnt, ne, n_blocks * EXPERT_ROWS)
    ybuf = _experts(xbuf, block_e, first, n_used, layer, w_gu, b_gu, w_down, b_down)
    return _combine(ybuf, meta, ps, sel, xs, n_ctx, mod_gt, ne)


def _qkv_kernel(x_ref, mod_ref, g_ref, w_ref, q_ref, k_ref, v_ref, *, n_ctx_tiles, nb, d):
    b = pl.program_id(0)
    j = pl.program_id(1)
    row = jnp.where(j < n_ctx_tiles, nb, b)
    mrow = mod_ref[pl.ds(row, 1), :]
    h = (_rms(x_ref[...]) * g_ref[...]) * (1.0 + mrow[:, d:2 * d]) + mrow[:, 0:d]
    qkv = jnp.dot(h.astype(BF16), w_ref[...], preferred_element_type=F32)
    for hd in range(d // DA_VDIM):
        lo = hd * DA_VDIM
        q_ref[hd] = qkv[:, lo:lo + DA_VDIM].astype(BF16)
        k_ref[hd] = qkv[:, d + lo:d + lo + DA_VDIM].astype(BF16)
        v_ref[hd] = qkv[:, 2 * d + lo:2 * d + lo + DA_VDIM].astype(BF16)


def _qkv(xs, n_ctx, mod_shsc, g, w_qkv):
    nb, t_len, d = xs.shape
    r = ROW_TILE
    nh = d // DA_VDIM
    kern = functools.partial(_qkv_kernel, n_ctx_tiles=n_ctx // r, nb=nb, d=d)
    nct = n_ctx // r
    hspec = pl.BlockSpec((None, nh, r, DA_VDIM), lambda b, j: (b, 0, j, 0))
    qspec = pl.BlockSpec((None, nh, r, DA_VDIM), lambda b, j: (b, 0, jnp.maximum(j - nct, 0), 0))
    kv_shape = jax.ShapeDtypeStruct((nb, nh, t_len, DA_VDIM), BF16)
    return pl.pallas_call(
        kern,
        out_shape=[jax.ShapeDtypeStruct((nb, nh, t_len - n_ctx, DA_VDIM), BF16), kv_shape, kv_shape],
        grid=(nb, t_len // r),
        in_specs=[pl.BlockSpec((None, r, d), lambda b, j: (b, j, 0)),
                  pl.BlockSpec((MOD_ROWS, 2 * d), lambda b, j: (0, 0)),
                  pl.BlockSpec((1, d), lambda b, j: (0, 0)),
                  pl.BlockSpec((d, 3 * d), lambda b, j: (0, 0))],
        out_specs=[qspec, hspec, hspec],
        compiler_params=_cparams(("parallel", "arbitrary")),
        name="qkv",
    )(xs, mod_shsc, g.reshape(1, d), w_qkv.astype(BF16))


def _attn_kernel(q_ref, k_ref, v_ref, qg_ref, kg_ref, cosq_ref, sinq_ref, cosk_ref, sink_ref, lam_ref,
                 sg_ref, o_ref, kn_ref, *, n_ctx, lambda_init):
    qi = pl.program_id(2)
    n_heads, n_tot = k_ref.shape[0], k_ref.shape[1]
    tq = q_ref.shape[1]
    dh = DA_HEAD_DIM
    lane = lax.broadcasted_iota(jnp.int32, (1, DA_VDIM), 1)
    comp0 = lane < dh
    low_half = lax.rem(lane, dh) < dh // 2

    def head_norm(xv, gain):
        x2 = xv * xv
        s0 = jnp.sum(jnp.where(comp0, x2, 0.0), axis=-1, keepdims=True)
        s1 = jnp.sum(jnp.where(comp0, 0.0, x2), axis=-1, keepdims=True)
        ms = jnp.where(comp0, s0, s1) * (1.0 / dh)
        return xv * lax.rsqrt(ms + NORM_EPS) * gain

    def rope(xv, cos, sin_signed):
        partner = jnp.where(low_half, pltpu.roll(xv, DA_VDIM - dh // 2, axis=1),
                            pltpu.roll(xv, dh // 2, axis=1))
        return xv * cos + partner * sin_signed

    @pl.when(qi == 0)
    def _():
        kg = kg_ref[...]
        for g in range(n_heads):
            kn_ref[g, 0:n_ctx, :] = head_norm(k_ref[g, 0:n_ctx, :].astype(F32), kg).astype(BF16)
            for c0 in range(n_ctx, n_tot, tq):
                kk = head_norm(k_ref[g, c0:c0 + tq, :].astype(F32), kg)
                kn_ref[g, c0:c0 + tq, :] = rope(kk, cosk_ref[c0 - n_ctx:c0 - n_ctx + tq, :],
                                                sink_ref[c0 - n_ctx:c0 - n_ctx + tq, :]).astype(BF16)

    lam_v = lam_ref[...]
    lam = (jnp.exp(jnp.sum(lam_v[0:1] * lam_v[1:2], axis=-1, keepdims=True))
           - jnp.exp(jnp.sum(lam_v[2:3] * lam_v[3:4], axis=-1, keepdims=True)) + lambda_init)
    dims = (((1,), (1,)), ((), ()))
    rb = ATTN_ROW_BLOCK
    units = [(g, r * rb) for g in range(n_heads) for r in range(tq // rb)]

    def scores(g, r0):
        qn = rope(head_norm(q_ref[g, r0:r0 + rb, :].astype(F32), qg_ref[...]),
                  cosq_ref[r0:r0 + rb, :], sinq_ref[r0:r0 + rb, :]) * (dh ** -0.5 * LOG2_E)
        qq = jnp.concatenate([jnp.where(comp0, qn, 0.0), jnp.where(comp0, 0.0, qn)], axis=0).astype(BF16)
        return lax.dot_general(qq, kn_ref[g], dims, preferred_element_type=F32)

    def softmax_mix(s):
        s0, s1 = s[0:rb], s[rb:2 * rb]
        e0 = jnp.exp2(s0 - jnp.max(s0, axis=-1, keepdims=True))
        e1 = jnp.exp2(s1 - jnp.max(s1, axis=-1, keepdims=True))
        l0 = jnp.sum(e0, axis=-1, keepdims=True)
        l1 = jnp.sum(e1, axis=-1, keepdims=True)
        return e0.astype(BF16) - (lam * l0 / l1).astype(BF16) * e1.astype(BF16), l0

    def values(g, r0, a, l0):
        o = jnp.dot(a, v_ref[g], preferred_element_type=F32) * (1.0 / l0)
        o_ref[r0:r0 + rb, g * DA_VDIM:(g + 1) * DA_VDIM] = (
            _rms(o) * sg_ref[...] * (1.0 - lambda_init)).astype(BF16)

    s_of, a_of = {}, {}
    for i in range(len(units) + 2):
        if i < len(units):
            s_of[i] = scores(*units[i])
        if 0 <= i - 1 < len(units):
            a_of[i - 1] = softmax_mix(s_of.pop(i - 1))
        if 0 <= i - 2 < len(units):
            values(*units[i - 2], *a_of.pop(i - 2))


def _rope_tables(n_tokens):
    rows = n_tokens // GRID_W
    row = jnp.repeat(jnp.arange(rows, dtype=F32), GRID_W)
    col = jnp.tile(jnp.arange(GRID_W, dtype=F32), rows)
    n_freq = DA_HEAD_DIM // 4
    inv_freq = jnp.exp(-math.log(ROPE_BASE) * jnp.arange(n_freq, dtype=F32) / n_freq)
    ang = jnp.concatenate([row[:, None] * inv_freq, col[:, None] * inv_freq], axis=-1)
    cos, sin = jnp.cos(ang), jnp.sin(ang)
    return jnp.tile(cos, (1, 4)), jnp.tile(jnp.concatenate([-sin, sin], axis=-1), (1, 2))


def _attention(q, k, v, n_ctx, q_gain, k_gain, lam_vecs, sub_gain, lambda_init):
    nb, nh, n_tot, dv = k.shape
    n_lat = n_tot - n_ctx
    tq = Q_TILE
    cos_t, sin_t = _rope_tables(n_lat)
    gain2 = lambda gv: jnp.tile(gv, 2).reshape(1, dv)
    lam_pad = jnp.zeros((SUBLANES, dv), F32).at[0:4, 0:DA_HEAD_DIM].set(lam_vecs)
    kern = functools.partial(_attn_kernel, n_ctx=n_ctx, lambda_init=lambda_init)
    const = lambda shape: pl.BlockSpec(shape, lambda b, h, i: (0, 0))
    hg = HEADS_PER_STEP
    return pl.pallas_call(
        kern,
        out_shape=jax.ShapeDtypeStruct((nb, n_lat, nh * dv), BF16),
        grid=(nb, nh // hg, n_lat // tq),
        in_specs=[pl.BlockSpec((None, hg, tq, dv), lambda b, h, i: (b, h, i, 0)),
                  pl.BlockSpec((None, hg, n_tot, dv), lambda b, h, i: (b, h, 0, 0)),
                  pl.BlockSpec((None, hg, n_tot, dv), lambda b, h, i: (b, h, 0, 0)),
                  const((1, dv)), const((1, dv)),
                  pl.BlockSpec((tq, dv), lambda b, h, i: (i, 0)),
                  pl.BlockSpec((tq, dv), lambda b, h, i: (i, 0)),
                  const((n_lat, dv)), const((n_lat, dv)),
                  const((SUBLANES, dv)), const((1, dv))],
        out_specs=pl.BlockSpec((None, tq, hg * dv), lambda b, h, i: (b, i, h)),
        scratch_shapes=[pltpu.VMEM((hg, n_tot, dv), BF16)],
        compiler_params=_cparams(("parallel", "parallel", "arbitrary")),
        name="attention",
    )(q, k, v, gain2(q_gain), gain2(k_gain), cos_t, sin_t, cos_t, sin_t, lam_pad, sub_gain.reshape(1, dv))


def _oproj_kernel(o_ref, w_ref, x_ref, mod_ref, out_ref):
    b = pl.program_id(0)
    y = jnp.dot(o_ref[...], w_ref[...], preferred_element_type=F32)
    out_ref[...] = x_ref[...] + mod_ref[pl.ds(b, 1), :] * y


def _oproj(o, w_o, xs, n_ctx, mod_gt):
    nb, n_lat, d = o.shape
    r = ROW_TILE
    off = n_ctx // r
    return pl.pallas_call(
        _oproj_kernel,
        out_shape=jax.ShapeDtypeStruct((nb, n_lat, d), F32),
        grid=(nb, n_lat // r),
        in_specs=[pl.BlockSpec((None, r, d), lambda b, j: (b, j, 0)),
                  pl.BlockSpec((d, d), lambda b, j: (0, 0)),
                  pl.BlockSpec((None, r, d), lambda b, j: (b, j + off, 0)),
                  pl.BlockSpec((MOD_ROWS, d), lambda b, j: (0, 0))],
        out_specs=pl.BlockSpec((None, r, d), lambda b, j: (b, j, 0)),
        compiler_params=_cparams(("parallel", "parallel")),
        name="oproj",
    )(o, w_o.astype(BF16), xs, mod_gt)


def _attn_layer(xs, n_ctx, mod, g_mix, w_qkv, w_o, q_gain, k_gain, lq1, lk1, lq2, lk2, sub_gain, lambda_init):
    d = xs.shape[-1]
    q, k, v = _qkv(xs, n_ctx, mod[:, 0:2 * d], g_mix, w_qkv)
    o = _attention(q, k, v, n_ctx, q_gain, k_gain, jnp.stack([lq1, lk1, lq2, lk2]), sub_gain, lambda_init)
    return _oproj(o, w_o, xs, n_ctx, mod[:, 2 * d:3 * d])


def kernel(x, c, ctx, c_ctx, w_ada, b_ada, g_mix, g_ffn, s5_a_re, s5_a_im, s5_log_dt, s5_b_re, s5_b_im, s5_c_re, s5_c_im, s5_d, s5_w_glu, s5_b_glu, da_w_qkv, da_w_o, da_q_gain, da_k_gain, da_lam_q1, da_lam_k1, da_lam_q2, da_lam_k2, da_sub_gain, moe_w_router, moe_b_router, moe_w_gu, moe_b_gu, moe_w_down, moe_b_down):
    nb, n_lat, d = x.shape
    n_ctx = ctx.shape[1]
    depth = w_ada.shape[0]
    assert depth == 2 and nb % SUBLANES == 0 and nb + 1 <= MOD_ROWS
    cc = jnp.concatenate([c, c_ctx[None], jnp.zeros((MOD_ROWS - nb - 1, d), F32)], axis=0)
    mod = _ada(cc, w_ada, b_ada)
    xs = _s5_layer(ctx, x, mod[0], g_mix[0], s5_a_re[0], s5_a_im[0], s5_log_dt[0], s5_b_re[0], s5_b_im[0],
                   s5_c_re[0], s5_c_im[0], s5_d[0], s5_w_glu[0], s5_b_glu[0])
    xs = _moe_layer(xs, n_ctx, mod[0][:, 3 * d:5 * d], mod[0][:, 5 * d:6 * d], g_ffn[0], 0,
                    moe_w_router, moe_b_router, moe_w_gu, moe_b_gu, moe_w_down, moe_b_down)
    lambda_init = 0.8 - 0.6 * math.exp(-0.3 * 1)
    x1 = _attn_layer(xs, n_ctx, mod[1], g_mix[1], da_w_qkv[0], da_w_o[0], da_q_gain[0], da_k_gain[0],
                     da_lam_q1[0], da_lam_k1[0], da_lam_q2[0], da_lam_k2[0], da_sub_gain[0], lambda_init)
    return _moe_layer(x1, 0, mod[1][:, 3 * d:5 * d], mod[1][:, 5 * d:6 * d], g_ffn[1], 1,
                      moe_w_router, moe_b_router, moe_w_gu, moe_b_gu, moe_w_down, moe_b_down)
```

```python
import functools
import math

import jax
import jax.numpy as jnp
from jax import lax
from jax.experimental import pallas as pl
from jax.experimental.pallas import tpu as pltpu

NORM_EPS = 1e-6
N_MIXERS = 2
S5_GROUP = 16
S5_STATE = 64
DA_HEAD_DIM = 64
DA_VDIM = 2 * DA_HEAD_DIM
GRID_W = 64
ROPE_BASE = 10000.0
N_EXPERTS = 32
TOP_K = 4
SWIGLU_LIMIT = 7.0
SWIGLU_ALPHA = 1.702
LOG2_E = 1.4426950408889634

LANES = 128
SUBLANES = 8
VMEM_LIMIT_BYTES = 56 * 1024 * 1024

S5_SLAB_GROUPS = LANES // S5_GROUP
S5_SLAB_STATE = S5_SLAB_GROUPS * S5_STATE
S5_CHUNK = 64
S5_ROW_BLOCKS = 4
TM_TILE = 32
ROW_TILE = 256
MOE_TILE = 512
ROUTER_TILE = 512
EXPERT_ROWS = 512
COMBINE_TILE = 256
Q_TILE = 512
ATTN_ROW_BLOCK = 128
HEADS_PER_STEP = 2
MOD_ROWS = 24

F32 = jnp.float32
BF16 = jnp.bfloat16


def _cparams(sem, vmem=VMEM_LIMIT_BYTES):
    return pltpu.CompilerParams(dimension_semantics=sem, vmem_limit_bytes=vmem)


def _rms(x):
    return x * lax.rsqrt(jnp.mean(x * x, axis=-1, keepdims=True) + NORM_EPS)


def _ada_kernel(c_ref, w_ref, b_ref, o_ref):
    cv = c_ref[...]
    s = cv * jax.nn.sigmoid(cv)
    o_ref[...] = jnp.dot(s, w_ref[...], precision=lax.Precision.HIGHEST,
                         preferred_element_type=F32) + b_ref[...]


def _ada(cc, w_ada, b_ada):
    depth, d, d6 = w_ada.shape
    nj = d6 // d
    return pl.pallas_call(
        _ada_kernel,
        out_shape=jax.ShapeDtypeStruct((depth, MOD_ROWS, d6), F32),
        grid=(depth, nj),
        in_specs=[pl.BlockSpec((MOD_ROWS, d), lambda i, j: (0, 0)),
                  pl.BlockSpec((None, d, d), lambda i, j: (i, 0, j)),
                  pl.BlockSpec((None, 1, d), lambda i, j: (i, 0, j))],
        out_specs=pl.BlockSpec((None, MOD_ROWS, d), lambda i, j: (i, 0, j)),
        compiler_params=_cparams(("parallel", "parallel")),
        name="ada",
    )(cc, w_ada, b_ada.reshape(depth, 1, d6))


def _prenorm_tm_kernel(ctx_ref, x_ref, mod_ref, g_ref, o_ref, *, n_ctx_tiles, nb, d):
    i = pl.program_id(0)
    tt = o_ref.shape[1] // nb
    g = g_ref[...]

    def emit(src_ref, mod_row):
        for b in range(nb):
            r = mod_row(b)
            sh = mod_ref[r:r + 1, 0:d]
            sc = mod_ref[r:r + 1, d:2 * d]
            h = (_rms(src_ref[b]) * g) * (1.0 + sc) + sh
            for s in range(d // LANES):
                o_ref[s, pl.ds(b, tt, stride=nb), :] = h[:, s * LANES:(s + 1) * LANES]

    @pl.when(i < n_ctx_tiles)
    def _():
        emit(ctx_ref, lambda b: nb)

    @pl.when(i >= n_ctx_tiles)
    def _():
        emit(x_ref, lambda b: b)


def _prenorm_tm(ctx, x, mod, g):
    nb, n_ctx, d = ctx.shape
    n_lat = x.shape[1]
    tt = TM_TILE
    nct, nlt = n_ctx // tt, n_lat // tt
    kern = functools.partial(_prenorm_tm_kernel, n_ctx_tiles=nct, nb=nb, d=d)
    return pl.pallas_call(
        kern,
        out_shape=jax.ShapeDtypeStruct((d // LANES, (n_ctx + n_lat) * nb, LANES), F32),
        grid=(nct + nlt,),
        in_specs=[pl.BlockSpec((nb, tt, d), lambda i: (0, jnp.minimum(i, nct - 1), 0)),
                  pl.BlockSpec((nb, tt, d), lambda i: (0, jnp.maximum(i - nct, 0), 0)),
                  pl.BlockSpec((MOD_ROWS, 2 * d), lambda i: (0, 0)),
                  pl.BlockSpec((1, d), lambda i: (0, 0))],
        out_specs=pl.BlockSpec((d // LANES, tt * nb, LANES), lambda i: (0, i, 0)),
        compiler_params=_cparams(("parallel",)),
        name="prenorm_tm",
    )(ctx, x, mod, g)


def _s5_disc_kernel(ar_ref, ai_ref, ldt_ref, br_ref, bi_ref, lr_ref, li_ref, bbr_ref, bbi_ref):
    ar, ai = ar_ref[...], ai_ref[...]
    dt = jnp.exp(ldt_ref[...])
    mag = jnp.exp(ar * dt)
    lr = mag * jnp.cos(ai * dt)
    li = mag * jnp.sin(ai * dt)
    nr, ni = lr - 1.0, li
    den = ar * ar + ai * ai
    qr = (nr * ar + ni * ai) / den
    qi = (ni * ar - nr * ai) / den
    br, bi = br_ref[...], bi_ref[...]
    lr_ref[...] = lr
    li_ref[...] = li
    bbr_ref[...] = qr * br - qi * bi
    bbi_ref[...] = qr * bi + qi * br


def _s5_disc(a_re, a_im, log_dt, b_re, b_im):
    nd, g, p = a_re.shape
    cg = b_re.shape[-1]
    gp = g * p
    col = lambda a: a.reshape(nd, gp, 1)
    ldt = jnp.broadcast_to(log_dt[:, :, None], (nd, g, p))
    spec1 = pl.BlockSpec((None, gp, 1), lambda i: (i, 0, 0))
    specb = pl.BlockSpec((None, gp, cg), lambda i: (i, 0, 0))
    lr, li, bbr, bbi = pl.pallas_call(
        _s5_disc_kernel,
        out_shape=[jax.ShapeDtypeStruct((nd, gp, 1), F32)] * 2 + [jax.ShapeDtypeStruct((nd, gp, cg), F32)] * 2,
        grid=(nd,),
        in_specs=[spec1, spec1, spec1, specb, specb],
        out_specs=[spec1, spec1, specb, specb],
        compiler_params=_cparams(("parallel",)),
        name="s5_disc",
    )(col(a_re), col(a_im), col(ldt), b_re.reshape(nd, gp, cg), b_im.reshape(nd, gp, cg))
    return (lr.reshape(nd, g, p), li.reshape(nd, g, p),
            bbr.reshape(nd, g, p, cg), bbi.reshape(nd, g, p, cg))


def _s5_slab_params(lam_re, lam_im, bb_re, bb_im, c_re, c_im):
    nd, g, p = lam_re.shape
    cg = bb_re.shape[-1]
    ns, sg = g // S5_SLAB_GROUPS, S5_SLAB_GROUPS
    eye = jnp.eye(sg, dtype=F32)

    def b_mat(bb):
        bb = bb.reshape(nd, ns, sg, p, cg)
        return jnp.einsum('dsjpc,jk->dsjckp', bb, eye).reshape(nd, ns, sg * cg, sg * p)

    def c_mat(cm):
        cm = cm.reshape(nd, ns, sg, cg, p)
        return jnp.einsum('dsjcp,jk->dskpjc', cm, eye).reshape(nd, ns, sg * p, sg * cg)

    bm = jnp.concatenate([b_mat(bb_re), b_mat(bb_im)], axis=-1).astype(BF16)
    lam = lambda a: a.reshape(nd, ns, 1, sg * p)
    return bm, c_mat(c_re).astype(BF16), c_mat(c_im).astype(BF16), lam(lam_re), lam(lam_im)


def _s5_scan_kernel(u_ref, bm_ref, cr_ref, ci_ref, lr_ref, li_ref, y_ref, x0_ref, x1_ref, h0_ref, h1_ref, st_ref):
    d = pl.program_id(0)
    c = pl.program_id(2)
    t_len, nb = u_ref.shape[0], u_ref.shape[1] * u_ref.shape[2]
    ns = S5_SLAB_STATE
    xs, hs = (x0_ref, x1_ref), (h0_ref, h1_ref)

    @pl.when(c == 0)
    def _():
        for ref in xs + hs:
            ref[...] = jnp.zeros_like(ref)

    def stages(in_ref, bu_ref, h_ref, out_ref):
        lr = jnp.broadcast_to(lr_ref[...], (SUBLANES, ns))
        li = jnp.broadcast_to(li_ref[...], (SUBLANES, ns))
        fresh = c <= 1
        state = [(jnp.where(fresh, 0.0, st_ref[0, o]), jnp.where(fresh, 0.0, st_ref[1, o]))
                 for o in range(nb // SUBLANES)]
        tb = t_len // S5_ROW_BLOCKS
        for j in range(S5_ROW_BLOCKS):
            rsl = slice(j * tb * nb, (j + 1) * tb * nb)
            u = u_ref[j * tb:(j + 1) * tb].reshape(tb * nb, LANES).astype(BF16)
            in_ref[rsl, :] = jnp.dot(u, bm_ref[...], preferred_element_type=F32)

            for s in range(j * tb, (j + 1) * tb):
                t = jnp.where(d == 0, s, t_len - 1 - s)
                for o in range(nb // SUBLANES):
                    hr, hi = state[o]
                    row = pl.multiple_of(t * nb + o * SUBLANES, SUBLANES)
                    br = bu_ref[pl.ds(row, SUBLANES), 0:ns]
                    bi = bu_ref[pl.ds(row, SUBLANES), ns:2 * ns]
                    nr = lr * hr - li * hi + br
                    ni = lr * hi + li * hr + bi
                    h_ref[pl.ds(row, SUBLANES), 0:ns] = nr
                    h_ref[pl.ds(row, SUBLANES), ns:2 * ns] = ni
                    state[o] = (nr, ni)

            y = (jnp.dot(out_ref[rsl, 0:ns].astype(BF16), cr_ref[...], preferred_element_type=F32)
                 - jnp.dot(out_ref[rsl, ns:2 * ns].astype(BF16), ci_ref[...], preferred_element_type=F32))
            y_ref[j * tb:(j + 1) * tb] = y.reshape((tb,) + y_ref.shape[1:])
        for o in range(nb // SUBLANES):
            st_ref[0, o] = state[o][0]
            st_ref[1, o] = state[o][1]

    phase = lax.rem(c, 2)
    for r in range(2):
        @pl.when(phase == r)
        def _(r=r):
            stages(xs[r], xs[1 - r], hs[1 - r], hs[r])


def _s5_scan(u_t, bm, cr, ci, lam_r, lam_i, nb, n_ctx, n_tot):
    nslab = u_t.shape[0]
    nbo = nb // SUBLANES
    t_len = S5_CHUNK
    ncc, nc = n_ctx // t_len, n_tot // t_len
    u5 = u_t.reshape(nslab, n_tot, nbo, SUBLANES, LANES)

    def chunk(d, c):
        c = jnp.clip(c, 0, nc - 1)
        back = jnp.where(c < ncc, ncc - 1 - c, ncc + nc - 1 - c)
        return jnp.where(d == 0, c, back)

    pspec = lambda r, k: pl.BlockSpec((None, None, r, k), lambda d, s, c: (d, s, 0, 0))
    rows = t_len * nb
    return pl.pallas_call(
        _s5_scan_kernel,
        out_shape=jax.ShapeDtypeStruct((2, nslab, n_tot, nbo, SUBLANES, LANES), F32),
        grid=(2, nslab, nc + 2),
        in_specs=[pl.BlockSpec((None, t_len, nbo, SUBLANES, LANES), lambda d, s, c: (s, chunk(d, c), 0, 0, 0)),
                  pspec(LANES, 2 * S5_SLAB_STATE), pspec(S5_SLAB_STATE, LANES), pspec(S5_SLAB_STATE, LANES),
                  pspec(1, S5_SLAB_STATE), pspec(1, S5_SLAB_STATE)],
        out_specs=pl.BlockSpec((None, None, t_len, nbo, SUBLANES, LANES),
                               lambda d, s, c: (d, s, chunk(d, c - 2), 0, 0, 0)),
        scratch_shapes=[pltpu.VMEM((rows, 2 * S5_SLAB_STATE), F32)] * 4
                       + [pltpu.VMEM((2, nbo, SUBLANES, S5_SLAB_STATE), F32)],
        compiler_params=_cparams(("parallel", "parallel", "arbitrary")),
        name="s5_scan",
    )(u5, bm, cr, ci, lam_r, lam_i)


def _s5_out_kernel(y_ref, u_ref, dsk_ref, w_ref, bg_ref, ctx_ref, x_ref, mod_ref, o_ref, tr_ref,
                   *, n_ctx_tiles, nb, d):
    i = pl.program_id(0)
    nslab = d // LANES
    rows = u_ref.shape[1]
    tt = rows // nb
    y = (y_ref[0] + y_ref[1]).reshape(nslab, rows, LANES)
    v = jax.nn.gelu(y + dsk_ref[...] * u_ref[...])
    lhs = jnp.concatenate([v[s] for s in range(nslab)], axis=-1).astype(BF16)
    z = jnp.dot(lhs, w_ref[...], preferred_element_type=F32) + bg_ref[...]
    m = z[:, 0:d] * jax.nn.sigmoid(z[:, d:2 * d])
    for s in range(nslab):
        tr_ref[s] = m[:, s * LANES:(s + 1) * LANES]

    def emit(src_ref, mod_row):
        for b in range(nb):
            r = mod_row(b)
            gt = mod_ref[r:r + 1, :]
            mb = jnp.concatenate([tr_ref[s, pl.ds(b, tt, stride=nb), :] for s in range(nslab)], axis=-1)
            o_ref[b] = src_ref[b] + gt * mb

    @pl.when(i < n_ctx_tiles)
    def _():
        emit(ctx_ref, lambda b: nb)

    @pl.when(i >= n_ctx_tiles)
    def _():
        emit(x_ref, lambda b: b)


def _s5_out(y_t, u_t, d_skip, w_glu, b_glu, ctx, x, gt1):
    nb, n_ctx, d = ctx.shape
    n_lat = x.shape[1]
    n_tot = n_ctx + n_lat
    nslab, nbo = d // LANES, nb // SUBLANES
    tt = TM_TILE
    nct, nlt = n_ctx // tt, n_lat // tt
    kern = functools.partial(_s5_out_kernel, n_ctx_tiles=nct, nb=nb, d=d)
    return pl.pallas_call(
        kern,
        out_shape=jax.ShapeDtypeStruct((nb, n_tot, d), F32),
        grid=(nct + nlt,),
        in_specs=[pl.BlockSpec((2, nslab, tt, nbo, SUBLANES, LANES), lambda i: (0, 0, i, 0, 0, 0)),
                  pl.BlockSpec((nslab, tt * nb, LANES), lambda i: (0, i, 0)),
                  pl.BlockSpec((nslab, 1, LANES), lambda i: (0, 0, 0)),
                  pl.BlockSpec((d, 2 * d), lambda i: (0, 0)),
                  pl.BlockSpec((1, 2 * d), lambda i: (0, 0)),
                  pl.BlockSpec((nb, tt, d), lambda i: (0, jnp.minimum(i, nct - 1), 0)),
                  pl.BlockSpec((nb, tt, d), lambda i: (0, jnp.maximum(i - nct, 0), 0)),
                  pl.BlockSpec((MOD_ROWS, d), lambda i: (0, 0))],
        out_specs=pl.BlockSpec((nb, tt, d), lambda i: (0, i, 0)),
        scratch_shapes=[pltpu.VMEM((nslab, tt * nb, LANES), F32)],
        compiler_params=_cparams(("parallel",)),
        name="s5_out",
    )(y_t, u_t, d_skip.reshape(nslab, 1, LANES), w_glu.astype(BF16), b_glu.reshape(1, 2 * d), ctx, x, gt1)


def _s5_layer(xs_ctx, xs_lat, mod, g_mix, a_re, a_im, log_dt, b_re, b_im, c_re, c_im, d_skip, w_glu, b_glu):
    nb, n_ctx, d = xs_ctx.shape
    n_tot = n_ctx + xs_lat.shape[1]
    u_t = _prenorm_tm(xs_ctx, xs_lat, mod[:, 0:2 * d], g_mix.reshape(1, d))
    lam_re, lam_im, bb_re, bb_im = _s5_disc(a_re, a_im, log_dt, b_re, b_im)
    bm, cr, ci, lam_r, lam_i = _s5_slab_params(lam_re, lam_im, bb_re, bb_im, c_re, c_im)
    y_t = _s5_scan(u_t, bm, cr, ci, lam_r, lam_i, nb, n_ctx, n_tot)
    return _s5_out(y_t, u_t, d_skip, w_glu, b_glu, xs_ctx, xs_lat, mod[:, 2 * d:3 * d])


NEG_PAD = -1e30
NEG_MASK = -3e38


def _split_bf16(a):
    hi = a.astype(BF16)
    return hi, (a - hi.astype(F32)).astype(BF16)


def _mod_row(piece, pieces_per_batch, n_ctx_pieces, nb):
    b = piece // pieces_per_batch
    return jnp.where(piece - b * pieces_per_batch < n_ctx_pieces, nb, b)


def _router_kernel(x_ref, mod_ref, g_ref, w_ref, bias_ref, h_ref, sel_ref, selt_ref, meta_ref, cnt_ref,
                   carry_ref, *, pieces_per_batch, n_ctx_pieces, nb, d):
    i = pl.program_id(0)
    rows = x_ref.shape[0]

    @pl.when(i == 0)
    def _():
        carry_ref[...] = jnp.zeros_like(carry_ref)

    parts = []
    for p in range(rows // ROW_TILE):
        row = _mod_row(i * (rows // ROW_TILE) + p, pieces_per_batch, n_ctx_pieces, nb)
        mrow = mod_ref[pl.ds(row, 1), :]
        xp = x_ref[p * ROW_TILE:(p + 1) * ROW_TILE, :]
        parts.append((_rms(xp) * g_ref[...]) * (1.0 + mrow[:, d:2 * d]) + mrow[:, 0:d])
    h = jnp.concatenate(parts, axis=0)
    h_ref[...] = h.astype(BF16)

    hi, lo = _split_bf16(h)
    whi, wlo = _split_bf16(w_ref[...])
    dot = functools.partial(jnp.dot, preferred_element_type=F32)
    logits = dot(hi, whi) + dot(lo, whi) + dot(hi, wlo) + bias_ref[...]

    lane = lax.broadcasted_iota(jnp.int32, (rows, LANES), 1)
    lane_f = lane.astype(F32)
    work = logits
    tops, idxs, hots = [], [], []
    for _ in range(TOP_K):
        m = jnp.max(work, axis=-1, keepdims=True)
        idx = jnp.min(jnp.where(work == m, lane_f, float(LANES)), axis=-1, keepdims=True)
        hot = lane_f == idx
        work = jnp.where(hot, NEG_MASK, work)
        tops.append(m)
        idxs.append(idx)
        hots.append(hot)
    exps = [jnp.exp(m - tops[0]) for m in tops]
    denom = exps[0] + exps[1] + exps[2] + exps[3]

    multi = jnp.zeros((rows, LANES), F32)
    for hot in hots:
        multi = multi + jnp.where(hot, 1.0, 0.0)
    r_i = lax.broadcasted_iota(jnp.int32, (rows, rows), 0)
    c_i = lax.broadcasted_iota(jnp.int32, (rows, rows), 1)
    tri = jnp.where(c_i < r_i, 1.0, 0.0).astype(BF16)
    in_tile = dot(tri, multi.astype(BF16))
    cnt_tile = jnp.sum(multi, axis=0, keepdims=True)
    e_r = lax.broadcasted_iota(jnp.int32, (LANES, LANES), 0)
    e_c = lax.broadcasted_iota(jnp.int32, (LANES, LANES), 1)
    below = jnp.where(e_r < e_c, 1.0, 0.0).astype(BF16)
    c_hi, c_lo = _split_bf16(jnp.broadcast_to(cnt_tile, (SUBLANES, LANES)))
    prefix = (dot(c_hi, below) + dot(c_lo, below))[0:1, :]
    slot = in_tile + prefix

    sel = jnp.zeros((rows, LANES), F32)
    for k in range(TOP_K):
        pos = jnp.sum(jnp.where(hots[k], slot, 0.0), axis=-1, keepdims=True)
        sel = jnp.where(lane == k, idxs[k].astype(F32), sel)
        sel = jnp.where(lane == TOP_K + k, exps[k] / denom, sel)
        sel = jnp.where(lane == 2 * TOP_K + k, pos, sel)
    sel_ref[...] = sel
    selt_ref[...] = sel.T[2 * TOP_K:2 * TOP_K + SUBLANES, :].astype(jnp.int32)

    carry = carry_ref[0:1, :]
    sub = lax.broadcasted_iota(jnp.int32, (SUBLANES, LANES), 0)
    meta = jnp.where(sub == 0, cnt_tile, jnp.where(sub == 1, carry, jnp.where(sub == 2, prefix, 0.0)))
    meta_ref[...] = meta.astype(jnp.int32)
    carry_ref[0:1, :] = carry + cnt_tile
    cnt_ref[...] = jnp.broadcast_to(carry + cnt_tile, cnt_ref.shape).astype(jnp.int32)


def _router(xs, n_ctx, mod_shsc, g, w_router, b_router):
    nb, t_len, d = xs.shape
    r = MOE_TILE
    n_tok = nb * t_len
    nt = n_tok // r
    ne = w_router.shape[1]
    w_pad = jnp.zeros((d, LANES), F32).at[:, :ne].set(w_router)
    b_pad = jnp.full((1, LANES), NEG_PAD, F32).at[0, :ne].set(b_router)
    kern = functools.partial(_router_kernel, pieces_per_batch=t_len // ROW_TILE,
                             n_ctx_pieces=n_ctx // ROW_TILE, nb=nb, d=d)
    tile = lambda i: (i, 0)
    const = lambda i: (0, 0)
    return pl.pallas_call(
        kern,
        out_shape=[jax.ShapeDtypeStruct((n_tok, d), BF16),
                   jax.ShapeDtypeStruct((n_tok, LANES), F32),
                   jax.ShapeDtypeStruct((nt * SUBLANES, r), jnp.int32),
                   jax.ShapeDtypeStruct((nt, SUBLANES, LANES), jnp.int32),
                   jax.ShapeDtypeStruct((SUBLANES, LANES), jnp.int32)],
        grid=(nt,),
        in_specs=[pl.BlockSpec((r, d), tile),
                  pl.BlockSpec((MOD_ROWS, 2 * d), const),
                  pl.BlockSpec((1, d), const),
                  pl.BlockSpec((d, LANES), const),
                  pl.BlockSpec((1, LANES), const)],
        out_specs=[pl.BlockSpec((r, d), tile),
                   pl.BlockSpec((r, LANES), tile),
                   pl.BlockSpec((SUBLANES, r), tile),
                   pl.BlockSpec((None, SUBLANES, LANES), lambda i: (i, 0, 0)),
                   pl.BlockSpec((SUBLANES, LANES), const)],
        scratch_shapes=[pltpu.VMEM((SUBLANES, LANES), F32)],
        compiler_params=_cparams(("arbitrary",)),
        name="router",
    )(xs.reshape(n_tok, d), mod_shsc, g.reshape(1, d), w_pad, b_pad)


def _plan_kernel(cnt_ref, ps_ref, be_ref, first_ref, nused_ref, *, ne, rb):
    n_blocks = be_ref.shape[0]

    def fill(i, carry):
        be_ref[i] = ne - 1
        first_ref[i] = 0
        return carry
    lax.fori_loop(0, n_blocks, fill, 0)

    def per_expert(e, acc):
        nblk = (cnt_ref[0, e] + (rb - 1)) // rb
        ps_ref[e] = acc
        b0 = acc // rb

        def mark(i, carry):
            be_ref[b0 + i] = e
            first_ref[b0 + i] = jnp.where(i == 0, 1, 0)
            return carry
        lax.fori_loop(0, nblk, mark, 0)
        return acc + nblk * rb

    total = lax.fori_loop(0, ne, per_expert, 0)

    def tail(e, carry):
        ps_ref[e] = total
        return carry
    lax.fori_loop(ne, ps_ref.shape[0], tail, 0)
    nused_ref[0] = total // rb


def _plan(cnt, ne, n_blocks):
    smem = pl.BlockSpec(memory_space=pltpu.SMEM)
    return pl.pallas_call(
        functools.partial(_plan_kernel, ne=ne, rb=EXPERT_ROWS),
        out_shape=[jax.ShapeDtypeStruct((LANES,), jnp.int32),
                   jax.ShapeDtypeStruct((n_blocks,), jnp.int32),
                   jax.ShapeDtypeStruct((n_blocks,), jnp.int32),
                   jax.ShapeDtypeStruct((1,), jnp.int32)],
        in_specs=[smem],
        out_specs=[smem, smem, smem, smem],
        name="plan",
    )(cnt)


_SEG_SIZES = tuple(MOE_TILE >> s for s in range(MOE_TILE.bit_length()))
TILE_SLOTS = MOE_TILE * TOP_K
TOKEN_ROWS = SUBLANES


def _segment_copies(n, src, dst, make_copy, wait=False):
    for p in _SEG_SIZES:
        take = (n & p) != 0

        @pl.when(take)
        def _():
            cp = make_copy(pl.multiple_of(src * TOKEN_ROWS, TOKEN_ROWS),
                           pl.multiple_of(dst * TOKEN_ROWS, TOKEN_ROWS), p * TOKEN_ROWS)
            cp.wait() if wait else cp.start()
        step = jnp.where(take, p, 0)
        src = src + step
        dst = dst + step


def _tile_major(x):
    return [x[:, s * LANES:(s + 1) * LANES] for s in range(x.shape[1] // LANES)]


def _dispatch_kernel(meta_ref, metap_ref, ps_ref, cnt_ref, h_ref, selt_ref, xbuf_ref, xs_ref, zero_ref, sem,
                     zsem, *, ne, rb):
    rows = h_ref.shape[0]
    slots = xs_ref.shape[1] // TOKEN_ROWS

    @pl.when(pl.program_id(0) == 0)
    def _():
        zero_ref[...] = jnp.zeros_like(zero_ref)

        def pad_copy(_, dst, p):
            return pltpu.make_async_copy(zero_ref.at[pl.ds(0, p), :], xbuf_ref.at[pl.ds(dst, p), :], zsem)

        for wait in (False, True):
            def pad_rows(e, carry, wait=wait):
                cnt = cnt_ref[0, e]
                _segment_copies((-cnt) & (rb - 1), 0, ps_ref[e] + cnt, pad_copy, wait)
                return carry
            lax.fori_loop(0, ne, pad_rows, 0)

    pos = selt_ref[0:TOP_K, :]
    s_iota = lax.broadcasted_iota(jnp.int32, (slots, rows), 0)
    perm = jnp.zeros((slots, rows), F32)
    for k in range(TOP_K):
        perm = perm + jnp.where(s_iota == pos[k:k + 1, :], 1.0, 0.0)
    step = pl.program_id(0)
    slot = lax.rem(step, 2)
    sorted_rows = jnp.dot(perm.astype(BF16), h_ref[...], preferred_element_type=F32)
    for s, slab in enumerate(_tile_major(sorted_rows)):
        xs_ref[slot, pl.ds(s, slots, stride=TOKEN_ROWS), :] = slab

    def segments(mref, which, wait):
        def seg_copy(src, dst, p):
            return pltpu.make_async_copy(xs_ref.at[which, pl.ds(src, p), :], xbuf_ref.at[pl.ds(dst, p), :],
                                         sem.at[which])

        def per_expert(e, carry):
            _segment_copies(mref[0, e], mref[2, e], ps_ref[e] + mref[1, e], seg_copy, wait)
            return carry
        lax.fori_loop(0, ne, per_expert, 0)

    segments(meta_ref, slot, False)

    @pl.when(step > 0)
    def _():
        segments(metap_ref, 1 - slot, True)

    @pl.when(step == pl.num_programs(0) - 1)
    def _():
        segments(meta_ref, slot, True)


def _dispatch(h2, selt, meta, ps, cnt, ne, n_buf):
    n_tok, d = h2.shape
    r = MOE_TILE
    smem = pl.BlockSpec(memory_space=pltpu.SMEM)
    return pl.pallas_call(
        functools.partial(_dispatch_kernel, ne=ne, rb=EXPERT_ROWS),
        out_shape=jax.ShapeDtypeStruct((n_buf * TOKEN_ROWS, LANES), F32),
        grid=(n_tok // r,),
        in_specs=[pl.BlockSpec((None, SUBLANES, LANES), lambda i: (i, 0, 0), memory_space=pltpu.SMEM),
                  pl.BlockSpec((None, SUBLANES, LANES), lambda i: (jnp.maximum(i - 1, 0), 0, 0),
                               memory_space=pltpu.SMEM),
                  smem, smem,
                  pl.BlockSpec((r, d), lambda i: (i, 0)),
                  pl.BlockSpec((SUBLANES, r), lambda i: (i, 0))],
        out_specs=pl.BlockSpec(memory_space=pl.ANY),
        scratch_shapes=[pltpu.VMEM((2, TILE_SLOTS * TOKEN_ROWS, LANES), F32),
                        pltpu.VMEM((_SEG_SIZES[0] * TOKEN_ROWS, LANES), F32),
                        pltpu.SemaphoreType.DMA((2,)), pltpu.SemaphoreType.DMA(())],
        compiler_params=_cparams(("arbitrary",)),
        name="dispatch",
    )(meta, meta, ps, cnt, h2, selt)


def _expert_kernel(be_ref, first_ref, nused_ref, x_ref, wgu_ref, bgu_ref, wd_ref, bd_ref, y_ref,
                   wgu_bf, wd_bf):
    i = pl.program_id(0)
    dff = wd_ref.shape[0]

    @pl.when(i < nused_ref[0])
    def _():
        @pl.when(first_ref[i] == 1)
        def _():
            wgu_bf[...] = wgu_ref[...].astype(BF16)
            wd_bf[...] = wd_ref[...].astype(BF16)

        rows = x_ref.shape[0] // TOKEN_ROWS
        x = jnp.concatenate([x_ref[pl.ds(s, rows, stride=TOKEN_ROWS), :] for s in range(TOKEN_ROWS)], axis=-1)
        gu = jnp.dot(x.astype(BF16), wgu_bf[...], preferred_element_type=F32) + bgu_ref[...]
        gate = jnp.minimum(gu[:, 0:dff], SWIGLU_LIMIT)
        up = jnp.clip(gu[:, dff:2 * dff], -SWIGLU_LIMIT, SWIGLU_LIMIT)
        act = (up + 1.0) * gate * jax.nn.sigmoid(SWIGLU_ALPHA * gate)
        y = jnp.dot(act.astype(BF16), wd_bf[...], preferred_element_type=F32) + bd_ref[...]
        for s, slab in enumerate(_tile_major(y)):
            y_ref[pl.ds(s, rows, stride=TOKEN_ROWS), :] = slab

    @pl.when(i >= nused_ref[0])
    def _():
        y_ref[...] = jnp.zeros_like(y_ref)


def _experts(xbuf, block_e, first, n_used, layer, w_gu, b_gu, w_down, b_down):
    depth, ne, d, dff2 = w_gu.shape
    assert d == TOKEN_ROWS * LANES
    dff = dff2 // 2
    rb = EXPERT_ROWS * TOKEN_ROWS
    blk = lambda i, nu: jnp.minimum(i, nu[0] - 1)
    wmap = lambda i, be, fi, nu: (layer, be[blk(i, nu)], 0, 0)
    grid_spec = pltpu.PrefetchScalarGridSpec(
        num_scalar_prefetch=3,
        grid=(xbuf.shape[0] // rb,),
        in_specs=[pl.BlockSpec((rb, LANES), lambda i, be, fi, nu: (blk(i, nu), 0)),
                  pl.BlockSpec((None, None, d, dff2), wmap),
                  pl.BlockSpec((None, None, 1, dff2), wmap),
                  pl.BlockSpec((None, None, dff, d), wmap),
                  pl.BlockSpec((None, None, 1, d), wmap)],
        out_specs=pl.BlockSpec((rb, LANES), lambda i, be, fi, nu: (i, 0)),
        scratch_shapes=[pltpu.VMEM((d, dff2), BF16), pltpu.VMEM((dff, d), BF16)])
    return pl.pallas_call(
        _expert_kernel,
        out_shape=jax.ShapeDtypeStruct(xbuf.shape, F32),
        grid_spec=grid_spec,
        compiler_params=_cparams(("arbitrary",)),
        name="experts",
    )(block_e, first, n_used, xbuf, w_gu, b_gu.reshape(depth, ne, 1, dff2), w_down,
      b_down.reshape(depth, ne, 1, d))


def _combine_kernel(meta_ref, metan_ref, ps_ref, sel_ref, x_ref, mod_ref, ybuf_ref, o_ref, ys_ref, sem,
                    *, pieces_per_batch, n_ctx_pieces, nb, ne):
    step = pl.program_id(0)
    n_steps = pl.num_programs(0)
    slot = lax.rem(step, 2)
    rows = x_ref.shape[0]
    slots = ys_ref.shape[1] // TOKEN_ROWS

    def segments(mref, to_slot, wait):
        def seg_copy(src, dst, p):
            return pltpu.make_async_copy(ybuf_ref.at[pl.ds(dst, p), :], ys_ref.at[to_slot, pl.ds(src, p), :],
                                         sem.at[to_slot])

        def per_expert(e, carry):
            _segment_copies(mref[0, e], mref[2, e], ps_ref[e] + mref[1, e], seg_copy, wait)
            return carry
        lax.fori_loop(0, ne, per_expert, 0)

    @pl.when(step == 0)
    def _():
        segments(meta_ref, 0, False)

    @pl.when(step + 1 < n_steps)
    def _():
        segments(metan_ref, 1 - slot, False)

    segments(meta_ref, slot, True)

    sel = sel_ref[...]
    l_iota = lax.broadcasted_iota(jnp.int32, (rows, slots), 1).astype(F32)
    w = jnp.zeros((rows, slots), F32)
    for k in range(TOP_K):
        w = w + jnp.where(l_iota == sel[:, 2 * TOP_K + k:2 * TOP_K + k + 1], sel[:, TOP_K + k:TOP_K + k + 1], 0.0)
    whi, wlo = _split_bf16(w)
    ys = jnp.concatenate([ys_ref[slot, pl.ds(s, slots, stride=TOKEN_ROWS), :] for s in range(TOKEN_ROWS)],
                         axis=-1).astype(BF16)
    acc = jnp.dot(whi, ys, preferred_element_type=F32) + jnp.dot(wlo, ys, preferred_element_type=F32)
    for p in range(rows // ROW_TILE):
        row = _mod_row(step * (rows // ROW_TILE) + p, pieces_per_batch, n_ctx_pieces, nb)
        psl = slice(p * ROW_TILE, (p + 1) * ROW_TILE)
        o_ref[psl, :] = x_ref[psl, :] + mod_ref[pl.ds(row, 1), :] * acc[psl, :]


def _combine(ybuf, meta, ps, sel, xs, n_ctx, mod_gt, ne):
    nb, t_len, d = xs.shape
    r = MOE_TILE
    n_tok = nb * t_len
    nt = n_tok // r
    kern = functools.partial(_combine_kernel, pieces_per_batch=t_len // ROW_TILE,
                             n_ctx_pieces=n_ctx // ROW_TILE, nb=nb, ne=ne)
    mspec = lambda f: pl.BlockSpec((None, SUBLANES, LANES), f, memory_space=pltpu.SMEM)
    out = pl.pallas_call(
        kern,
        out_shape=jax.ShapeDtypeStruct((n_tok, d), F32),
        grid=(nt,),
        in_specs=[mspec(lambda i: (i, 0, 0)),
                  mspec(lambda i: (jnp.minimum(i + 1, nt - 1), 0, 0)),
                  pl.BlockSpec(memory_space=pltpu.SMEM),
                  pl.BlockSpec((r, LANES), lambda i: (i, 0)),
                  pl.BlockSpec((r, d), lambda i: (i, 0)),
                  pl.BlockSpec((MOD_ROWS, d), lambda i: (0, 0)),
                  pl.BlockSpec(memory_space=pl.ANY)],
        out_specs=pl.BlockSpec((r, d), lambda i: (i, 0)),
        scratch_shapes=[pltpu.VMEM((2, TILE_SLOTS * TOKEN_ROWS, LANES), F32), pltpu.SemaphoreType.DMA((2,))],
        compiler_params=_cparams(("arbitrary",)),
        name="combine",
    )(meta, meta, ps, sel, xs.reshape(n_tok, d), mod_gt, ybuf)
    return out.reshape(nb, t_len, d)


def _moe_layer(xs, n_ctx, mod_shsc, mod_gt, g, layer, w_router, b_router, w_gu, b_gu, w_down, b_down):
    nb, t_len, d = xs.shape
    ne = w_router.shape[-1]
    n_tiles = nb * t_len // MOE_TILE
    n_blocks = n_tiles * TILE_SLOTS // EXPERT_ROWS + ne
    h2, sel, selt, meta, cnt = _router(xs, n_ctx, mod_shsc, g, w_router[layer], b_router[layer])
    ps, block_e, first, n_used = _plan(cnt, ne, n_blocks)
    xbuf = _dispatch(h2, selt, meta, ps, cnt, ne, n_blocks * EXPERT_ROWS)
    ybuf = _experts(xbuf, block_e, first, n_used, layer, w_gu, b_gu, w_down, b_down)
    return _combine(ybuf, meta, ps, sel, xs, n_ctx, mod_gt, ne)


def _qkv_kernel(x_ref, mod_ref, g_ref, w_ref, q_ref, k_ref, v_ref, *, n_ctx_tiles, nb, d):
    b = pl.program_id(0)
    j = pl.program_id(1)
    row = jnp.where(j < n_ctx_tiles, nb, b)
    mrow = mod_ref[pl.ds(row, 1), :]
    h = (_rms(x_ref[...]) * g_ref[...]) * (1.0 + mrow[:, d:2 * d]) + mrow[:, 0:d]
    qkv = jnp.dot(h.astype(BF16), w_ref[...], preferred_element_type=F32)
    for hd in range(d // DA_VDIM):
        lo = hd * DA_VDIM
        q_ref[hd] = qkv[:, lo:lo + DA_VDIM].astype(BF16)
        k_ref[hd] = qkv[:, d + lo:d + lo + DA_VDIM].astype(BF16)
        v_ref[hd] = qkv[:, 2 * d + lo:2 * d + lo + DA_VDIM].astype(BF16)


def _qkv(xs, n_ctx, mod_shsc, g, w_qkv):
    nb, t_len, d = xs.shape
    r = ROW_TILE
    nh = d // DA_VDIM
    kern = functools.partial(_qkv_kernel, n_ctx_tiles=n_ctx // r, nb=nb, d=d)
    nct = n_ctx // r
    hspec = pl.BlockSpec((None, nh, r, DA_VDIM), lambda b, j: (b, 0, j, 0))
    qspec = pl.BlockSpec((None, nh, r, DA_VDIM), lambda b, j: (b, 0, jnp.maximum(j - nct, 0), 0))
    kv_shape = jax.ShapeDtypeStruct((nb, nh, t_len, DA_VDIM), BF16)
    return pl.pallas_call(
        kern,
        out_shape=[jax.ShapeDtypeStruct((nb, nh, t_len - n_ctx, DA_VDIM), BF16), kv_shape, kv_shape],
        grid=(nb, t_len // r),
        in_specs=[pl.BlockSpec((None, r, d), lambda b, j: (b, j, 0)),
                  pl.BlockSpec((MOD_ROWS, 2 * d), lambda b, j: (0, 0)),
                  pl.BlockSpec((1, d), lambda b, j: (0, 0)),
                  pl.BlockSpec((d, 3 * d), lambda b, j: (0, 0))],
        out_specs=[qspec, hspec, hspec],
        compiler_params=_cparams(("parallel", "arbitrary")),
        name="qkv",
    )(xs, mod_shsc, g.reshape(1, d), w_qkv.astype(BF16))


def _attn_kernel(q_ref, k_ref, v_ref, qg_ref, kg_ref, cosq_ref, sinq_ref, cosk_ref, sink_ref, lam_ref,
                 sg_ref, o_ref, kn_ref, *, n_ctx, lambda_init):
    qi = pl.program_id(2)
    n_heads, n_tot = k_ref.shape[0], k_ref.shape[1]
    tq = q_ref.shape[1]
    dh = DA_HEAD_DIM
    lane = lax.broadcasted_iota(jnp.int32, (1, DA_VDIM), 1)
    comp0 = lane < dh
    low_half = lax.rem(lane, dh) < dh // 2

    def head_norm(xv, gain):
        x2 = xv * xv
        s0 = jnp.sum(jnp.where(comp0, x2, 0.0), axis=-1, keepdims=True)
        s1 = jnp.sum(jnp.where(comp0, 0.0, x2), axis=-1, keepdims=True)
        ms = jnp.where(comp0, s0, s1) * (1.0 / dh)
        return xv * lax.rsqrt(ms + NORM_EPS) * gain

    def rope(xv, cos, sin_signed):
        partner = jnp.where(low_half, pltpu.roll(xv, DA_VDIM - dh // 2, axis=1),
                            pltpu.roll(xv, dh // 2, axis=1))
        return xv * cos + partner * sin_signed

    @pl.when(qi == 0)
    def _():
        kg = kg_ref[...]
        for g in range(n_heads):
            kn_ref[g, 0:n_ctx, :] = head_norm(k_ref[g, 0:n_ctx, :].astype(F32), kg).astype(BF16)
            for c0 in range(n_ctx, n_tot, tq):
                kk = head_norm(k_ref[g, c0:c0 + tq, :].astype(F32), kg)
                kn_ref[g, c0:c0 + tq, :] = rope(kk, cosk_ref[c0 - n_ctx:c0 - n_ctx + tq, :],
                                                sink_ref[c0 - n_ctx:c0 - n_ctx + tq, :]).astype(BF16)

    lam_v = lam_ref[...]
    lam = (jnp.exp(jnp.sum(lam_v[0:1] * lam_v[1:2], axis=-1, keepdims=True))
           - jnp.exp(jnp.sum(lam_v[2:3] * lam_v[3:4], axis=-1, keepdims=True)) + lambda_init)
    dims = (((1,), (1,)), ((), ()))
    rb = ATTN_ROW_BLOCK
    units = [(g, r * rb) for g in range(n_heads) for r in range(tq // rb)]

    def scores(g, r0):
        qn = rope(head_norm(q_ref[g, r0:r0 + rb, :].astype(F32), qg_ref[...]),
                  cosq_ref[r0:r0 + rb, :], sinq_ref[r0:r0 + rb, :]) * (dh ** -0.5 * LOG2_E)
        qq = jnp.concatenate([jnp.where(comp0, qn, 0.0), jnp.where(comp0, 0.0, qn)], axis=0).astype(BF16)
        return lax.dot_general(qq, kn_ref[g], dims, preferred_element_type=F32)

    def softmax_mix(s):
        s0, s1 = s[0:rb], s[rb:2 * rb]
        e0 = jnp.exp2(s0 - jnp.max(s0, axis=-1, keepdims=True))
        e1 = jnp.exp2(s1 - jnp.max(s1, axis=-1, keepdims=True))
        l0 = jnp.sum(e0, axis=-1, keepdims=True)
        l1 = jnp.sum(e1, axis=-1, keepdims=True)
        return e0.astype(BF16) - (lam * l0 / l1).astype(BF16) * e1.astype(BF16), l0

    def values(g, r0, a, l0):
        o = jnp.dot(a, v_ref[g], preferred_element_type=F32) * (1.0 / l0)
        o_ref[r0:r0 + rb, g * DA_VDIM:(g + 1) * DA_VDIM] = (
            _rms(o) * sg_ref[...] * (1.0 - lambda_init)).astype(BF16)

    s_of, a_of = {}, {}
    for i in range(len(units) + 2):
        if i < len(units):
            s_of[i] = scores(*units[i])
        if 0 <= i - 1 < len(units):
            a_of[i - 1] = softmax_mix(s_of.pop(i - 1))
        if 0 <= i - 2 < len(units):
            values(*units[i - 2], *a_of.pop(i - 2))


def _rope_tables(n_tokens):
    rows = n_tokens // GRID_W
    row = jnp.repeat(jnp.arange(rows, dtype=F32), GRID_W)
    col = jnp.tile(jnp.arange(GRID_W, dtype=F32), rows)
    n_freq = DA_HEAD_DIM // 4
    inv_freq = jnp.exp(-math.log(ROPE_BASE) * jnp.arange(n_freq, dtype=F32) / n_freq)
    ang = jnp.concatenate([row[:, None] * inv_freq, col[:, None] * inv_freq], axis=-1)
    cos, sin = jnp.cos(ang), jnp.sin(ang)
    return jnp.tile(cos, (1, 4)), jnp.tile(jnp.concatenate([-sin, sin], axis=-1), (1, 2))


def _attention(q, k, v, n_ctx, q_gain, k_gain, lam_vecs, sub_gain, lambda_init):
    nb, nh, n_tot, dv = k.shape
    n_lat = n_tot - n_ctx
    tq = Q_TILE
    cos_t, sin_t = _rope_tables(n_lat)
    gain2 = lambda gv: jnp.tile(gv, 2).reshape(1, dv)
    lam_pad = jnp.zeros((SUBLANES, dv), F32).at[0:4, 0:DA_HEAD_DIM].set(lam_vecs)
    kern = functools.partial(_attn_kernel, n_ctx=n_ctx, lambda_init=lambda_init)
    const = lambda shape: pl.BlockSpec(shape, lambda b, h, i: (0, 0))
    hg = HEADS_PER_STEP
    return pl.pallas_call(
        kern,
        out_shape=jax.ShapeDtypeStruct((nb, n_lat, nh * dv), BF16),
        grid=(nb, nh // hg, n_lat // tq),
        in_specs=[pl.BlockSpec((None, hg, tq, dv), lambda b, h, i: (b, h, i, 0)),
                  pl.BlockSpec((None, hg, n_tot, dv), lambda b, h, i: (b, h, 0, 0)),
                  pl.BlockSpec((None, hg, n_tot, dv), lambda b, h, i: (b, h, 0, 0)),
                  const((1, dv)), const((1, dv)),
                  pl.BlockSpec((tq, dv), lambda b, h, i: (i, 0)),
                  pl.BlockSpec((tq, dv), lambda b, h, i: (i, 0)),
                  const((n_lat, dv)), const((n_lat, dv)),
                  const((SUBLANES, dv)), const((1, dv))],
        out_specs=pl.BlockSpec((None, tq, hg * dv), lambda b, h, i: (b, i, h)),
        scratch_shapes=[pltpu.VMEM((hg, n_tot, dv), BF16)],
        compiler_params=_cparams(("parallel", "parallel", "arbitrary")),
        name="attention",
    )(q, k, v, gain2(q_gain), gain2(k_gain), cos_t, sin_t, cos_t, sin_t, lam_pad, sub_gain.reshape(1, dv))


def _oproj_kernel(o_ref, w_ref, x_ref, mod_ref, out_ref):
    b = pl.program_id(0)
    y = jnp.dot(o_ref[...], w_ref[...], preferred_element_type=F32)
    out_ref[...] = x_ref[...] + mod_ref[pl.ds(b, 1), :] * y


def _oproj(o, w_o, xs, n_ctx, mod_gt):
    nb, n_lat, d = o.shape
    r = ROW_TILE
    off = n_ctx // r
    return pl.pallas_call(
        _oproj_kernel,
        out_shape=jax.ShapeDtypeStruct((nb, n_lat, d), F32),
        grid=(nb, n_lat // r),
        in_specs=[pl.BlockSpec((None, r, d), lambda b, j: (b, j, 0)),
                  pl.BlockSpec((d, d), lambda b, j: (0, 0)),
                  pl.BlockSpec((None, r, d), lambda b, j: (b, j + off, 0)),
                  pl.BlockSpec((MOD_ROWS, d), lambda b, j: (0, 0))],
        out_specs=pl.BlockSpec((None, r, d), lambda b, j: (b, j, 0)),
        compiler_params=_cparams(("parallel", "parallel")),
        name="oproj",
    )(o, w_o.astype(BF16), xs, mod_gt)


def _attn_layer(xs, n_ctx, mod, g_mix, w_qkv, w_o, q_gain, k_gain, lq1, lk1, lq2, lk2, sub_gain, lambda_init):
    d = xs.shape[-1]
    q, k, v = _qkv(xs, n_ctx, mod[:, 0:2 * d], g_mix, w_qkv)
    o = _attention(q, k, v, n_ctx, q_gain, k_gain, jnp.stack([lq1, lk1, lq2, lk2]), sub_gain, lambda_init)
    return _oproj(o, w_o, xs, n_ctx, mod[:, 2 * d:3 * d])


def kernel(x, c, ctx, c_ctx, w_ada, b_ada, g_mix, g_ffn, s5_a_re, s5_a_im, s5_log_dt, s5_b_re, s5_b_im, s5_c_re, s5_c_im, s5_d, s5_w_glu, s5_b_glu, da_w_qkv, da_w_o, da_q_gain, da_k_gain, da_lam_q1, da_lam_k1, da_lam_q2, da_lam_k2, da_sub_gain, moe_w_router, moe_b_router, moe_w_gu, moe_b_gu, moe_w_down, moe_b_down):
    nb, n_lat, d = x.shape
    n_ctx = ctx.shape[1]
    depth = w_ada.shape[0]
    assert depth == 2 and nb % SUBLANES == 0 and nb + 1 <= MOD_ROWS
    cc = jnp.concatenate([c, c_ctx[None], jnp.zeros((MOD_ROWS - nb - 1, d), F32)], axis=0)
    mod = _ada(cc, w_ada, b_ada)
    xs = _s5_layer(ctx, x, mod[0], g_mix[0], s5_a_re[0], s5_a_im[0], s5_log_dt[0], s5_b_re[0], s5_b_im[0],
                   s5_c_re[0], s5_c_im[0], s5_d[0], s5_w_glu[0], s5_b_glu[0])
    xs = _moe_layer(xs, n_ctx, mod[0][:, 3 * d:5 * d], mod[0][:, 5 * d:6 * d], g_ffn[0], 0,
                    moe_w_router, moe_b_router, moe_w_gu, moe_b_gu, moe_w_down, moe_b_down)
    lambda_init = 0.8 - 0.6 * math.exp(-0.3 * 1)
    x1 = _attn_layer(xs, n_ctx, mod[1], g_mix[1], da_w_qkv[0], da_w_o[0], da_q_gain[0], da_k_gain[0],
                     da_lam_q1[0], da_lam_k1[0], da_lam_q2[0], da_lam_k2[0], da_sub_gain[0], lambda_init)
    return _moe_layer(x1, 0, mod[1][:, 3 * d:5 * d], mod[1][:, 5 * d:6 * d], g_ffn[1], 1,
                      moe_w_router, moe_b_router, moe_w_gu, moe_b_gu, moe_w_down, moe_b_down)
```

```python
import functools
import math

import jax
import jax.numpy as jnp
from jax import lax
from jax.experimental import pallas as pl
from jax.experimental.pallas import tpu as pltpu

NORM_EPS = 1e-6
N_MIXERS = 2
S5_GROUP = 16
S5_STATE = 64
DA_HEAD_DIM = 64
DA_VDIM = 2 * DA_HEAD_DIM
GRID_W = 64
ROPE_BASE = 10000.0
N_EXPERTS = 32
TOP_K = 4
SWIGLU_LIMIT = 7.0
SWIGLU_ALPHA = 1.702
LOG2_E = 1.4426950408889634

LANES = 128
SUBLANES = 8
VMEM_LIMIT_BYTES = 56 * 1024 * 1024

S5_SLAB_GROUPS = LANES // S5_GROUP
S5_SLAB_STATE = S5_SLAB_GROUPS * S5_STATE
S5_CHUNK = 64
S5_ROW_BLOCKS = 4
TM_TILE = 32
ROW_TILE = 256
MOE_TILE = 512
ROUTER_TILE = 512
EXPERT_ROWS = 512
COMBINE_TILE = 256
Q_TILE = 512
ATTN_ROW_BLOCK = 256
HEADS_PER_STEP = 2
MOD_ROWS = 24

F32 = jnp.float32
BF16 = jnp.bfloat16


def _cparams(sem, vmem=VMEM_LIMIT_BYTES):
    return pltpu.CompilerParams(dimension_semantics=sem, vmem_limit_bytes=vmem)


def _rms(x):
    return x * lax.rsqrt(jnp.mean(x * x, axis=-1, keepdims=True) + NORM_EPS)


def _ada_kernel(c_ref, w_ref, b_ref, o_ref):
    cv = c_ref[...]
    s = cv * jax.nn.sigmoid(cv)
    o_ref[...] = jnp.dot(s, w_ref[...], precision=lax.Precision.HIGHEST,
                         preferred_element_type=F32) + b_ref[...]


def _ada(cc, w_ada, b_ada):
    depth, d, d6 = w_ada.shape
    nj = d6 // d
    return pl.pallas_call(
        _ada_kernel,
        out_shape=jax.ShapeDtypeStruct((depth, MOD_ROWS, d6), F32),
        grid=(depth, nj),
        in_specs=[pl.BlockSpec((MOD_ROWS, d), lambda i, j: (0, 0)),
                  pl.BlockSpec((None, d, d), lambda i, j: (i, 0, j)),
                  pl.BlockSpec((None, 1, d), lambda i, j: (i, 0, j))],
        out_specs=pl.BlockSpec((None, MOD_ROWS, d), lambda i, j: (i, 0, j)),
        compiler_params=_cparams(("parallel", "parallel")),
        name="ada",
    )(cc, w_ada, b_ada.reshape(depth, 1, d6))


def _prenorm_tm_kernel(ctx_ref, x_ref, mod_ref, g_ref, o_ref, *, n_ctx_tiles, nb, d):
    i = pl.program_id(0)
    tt = o_ref.shape[1] // nb
    g = g_ref[...]

    def emit(src_ref, mod_row):
        for b in range(nb):
            r = mod_row(b)
            sh = mod_ref[r:r + 1, 0:d]
            sc = mod_ref[r:r + 1, d:2 * d]
            h = (_rms(src_ref[b]) * g) * (1.0 + sc) + sh
            for s in range(d // LANES):
                o_ref[s, pl.ds(b, tt, stride=nb), :] = h[:, s * LANES:(s + 1) * LANES]

    @pl.when(i < n_ctx_tiles)
    def _():
        emit(ctx_ref, lambda b: nb)

    @pl.when(i >= n_ctx_tiles)
    def _():
        emit(x_ref, lambda b: b)


def _prenorm_tm(ctx, x, mod, g):
    nb, n_ctx, d = ctx.shape
    n_lat = x.shape[1]
    tt = TM_TILE
    nct, nlt = n_ctx // tt, n_lat // tt
    kern = functools.partial(_prenorm_tm_kernel, n_ctx_tiles=nct, nb=nb, d=d)
    return pl.pallas_call(
        kern,
        out_shape=jax.ShapeDtypeStruct((d // LANES, (n_ctx + n_lat) * nb, LANES), F32),
        grid=(nct + nlt,),
        in_specs=[pl.BlockSpec((nb, tt, d), lambda i: (0, jnp.minimum(i, nct - 1), 0)),
                  pl.BlockSpec((nb, tt, d), lambda i: (0, jnp.maximum(i - nct, 0), 0)),
                  pl.BlockSpec((MOD_ROWS, 2 * d), lambda i: (0, 0)),
                  pl.BlockSpec((1, d), lambda i: (0, 0))],
        out_specs=pl.BlockSpec((d // LANES, tt * nb, LANES), lambda i: (0, i, 0)),
        compiler_params=_cparams(("parallel",)),
        name="prenorm_tm",
    )(ctx, x, mod, g)


def _s5_disc_kernel(ar_ref, ai_ref, ldt_ref, br_ref, bi_ref, lr_ref, li_ref, bbr_ref, bbi_ref):
    ar, ai = ar_ref[...], ai_ref[...]
    dt = jnp.exp(ldt_ref[...])
    mag = jnp.exp(ar * dt)
    lr = mag * jnp.cos(ai * dt)
    li = mag * jnp.sin(ai * dt)
    nr, ni = lr - 1.0, li
    den = ar * ar + ai * ai
    qr = (nr * ar + ni * ai) / den
    qi = (ni * ar - nr * ai) / den
    br, bi = br_ref[...], bi_ref[...]
    lr_ref[...] = lr
    li_ref[...] = li
    bbr_ref[...] = qr * br - qi * bi
    bbi_ref[...] = qr * bi + qi * br


def _s5_disc(a_re, a_im, log_dt, b_re, b_im):
    nd, g, p = a_re.shape
    cg = b_re.shape[-1]
    gp = g * p
    col = lambda a: a.reshape(nd, gp, 1)
    ldt = jnp.broadcast_to(log_dt[:, :, None], (nd, g, p))
    spec1 = pl.BlockSpec((None, gp, 1), lambda i: (i, 0, 0))
    specb = pl.BlockSpec((None, gp, cg), lambda i: (i, 0, 0))
    lr, li, bbr, bbi = pl.pallas_call(
        _s5_disc_kernel,
        out_shape=[jax.ShapeDtypeStruct((nd, gp, 1), F32)] * 2 + [jax.ShapeDtypeStruct((nd, gp, cg), F32)] * 2,
        grid=(nd,),
        in_specs=[spec1, spec1, spec1, specb, specb],
        out_specs=[spec1, spec1, specb, specb],
        compiler_params=_cparams(("parallel",)),
        name="s5_disc",
    )(col(a_re), col(a_im), col(ldt), b_re.reshape(nd, gp, cg), b_im.reshape(nd, gp, cg))
    return (lr.reshape(nd, g, p), li.reshape(nd, g, p),
            bbr.reshape(nd, g, p, cg), bbi.reshape(nd, g, p, cg))


def _s5_slab_params(lam_re, lam_im, bb_re, bb_im, c_re, c_im):
    nd, g, p = lam_re.shape
    cg = bb_re.shape[-1]
    ns, sg = g // S5_SLAB_GROUPS, S5_SLAB_GROUPS
    eye = jnp.eye(sg, dtype=F32)

    def b_mat(bb):
        bb = bb.reshape(nd, ns, sg, p, cg)
        return jnp.einsum('dsjpc,jk->dsjckp', bb, eye).reshape(nd, ns, sg * cg, sg * p)

    def c_mat(cm):
        cm = cm.reshape(nd, ns, sg, cg, p)
        return jnp.einsum('dsjcp,jk->dskpjc', cm, eye).reshape(nd, ns, sg * p, sg * cg)

    bm = jnp.concatenate([b_mat(bb_re), b_mat(bb_im)], axis=-1).astype(BF16)
    lam = lambda a: a.reshape(nd, ns, 1, sg * p)
    return bm, c_mat(c_re).astype(BF16), c_mat(c_im).astype(BF16), lam(lam_re), lam(lam_im)


def _s5_scan_kernel(u_ref, bm_ref, cr_ref, ci_ref, lr_ref, li_ref, y_ref, x0_ref, x1_ref, h0_ref, h1_ref, st_ref):
    d = pl.program_id(0)
    c = pl.program_id(2)
    t_len, nb = u_ref.shape[0], u_ref.shape[1] * u_ref.shape[2]
    ns = S5_SLAB_STATE
    xs, hs = (x0_ref, x1_ref), (h0_ref, h1_ref)

    @pl.when(c == 0)
    def _():
        for ref in xs + hs:
            ref[...] = jnp.zeros_like(ref)

    def stages(in_ref, bu_ref, h_ref, out_ref):
        lr = jnp.broadcast_to(lr_ref[...], (SUBLANES, ns))
        li = jnp.broadcast_to(li_ref[...], (SUBLANES, ns))
        fresh = c <= 1
        state = [(jnp.where(fresh, 0.0, st_ref[0, o]), jnp.where(fresh, 0.0, st_ref[1, o]))
                 for o in range(nb // SUBLANES)]
        tb = t_len // S5_ROW_BLOCKS
        for j in range(S5_ROW_BLOCKS):
            rsl = slice(j * tb * nb, (j + 1) * tb * nb)
            u = u_ref[j * tb:(j + 1) * tb].reshape(tb * nb, LANES).astype(BF16)
            in_ref[rsl, :] = jnp.dot(u, bm_ref[...], preferred_element_type=F32)

            for s in range(j * tb, (j + 1) * tb):
                t = jnp.where(d == 0, s, t_len - 1 - s)
                row = pl.multiple_of(t * nb, nb)
                for o in range(nb // SUBLANES):
                    hr, hi = state[o]
                    br = bu_ref[pl.ds(row + o * SUBLANES, SUBLANES), 0:ns]
                    bi = bu_ref[pl.ds(row + o * SUBLANES, SUBLANES), ns:2 * ns]
                    state[o] = (lr * hr - li * hi + br, lr * hi + li * hr + bi)
                h_ref[pl.ds(row, nb), 0:ns] = jnp.concatenate([st[0] for st in state], axis=0).astype(BF16)
                h_ref[pl.ds(row, nb), ns:2 * ns] = jnp.concatenate([st[1] for st in state], axis=0).astype(BF16)

            y = (jnp.dot(out_ref[rsl, 0:ns], cr_ref[...], preferred_element_type=F32)
                 - jnp.dot(out_ref[rsl, ns:2 * ns], ci_ref[...], preferred_element_type=F32))
            y_ref[j * tb:(j + 1) * tb] = y.reshape((tb,) + y_ref.shape[1:])
        for o in range(nb // SUBLANES):
            st_ref[0, o] = state[o][0]
            st_ref[1, o] = state[o][1]

    phase = lax.rem(c, 2)
    for r in range(2):
        @pl.when(phase == r)
        def _(r=r):
            stages(xs[r], xs[1 - r], hs[1 - r], hs[r])


def _s5_scan(u_t, bm, cr, ci, lam_r, lam_i, nb, n_ctx, n_tot):
    nslab = u_t.shape[0]
    nbo = nb // SUBLANES
    t_len = S5_CHUNK
    ncc, nc = n_ctx // t_len, n_tot // t_len
    u5 = u_t.reshape(nslab, n_tot, nbo, SUBLANES, LANES)

    def chunk(d, c):
        c = jnp.clip(c, 0, nc - 1)
        back = jnp.where(c < ncc, ncc - 1 - c, ncc + nc - 1 - c)
        return jnp.where(d == 0, c, back)

    pspec = lambda r, k: pl.BlockSpec((None, None, r, k), lambda d, s, c: (d, s, 0, 0))
    rows = t_len * nb
    return pl.pallas_call(
        _s5_scan_kernel,
        out_shape=jax.ShapeDtypeStruct((2, nslab, n_tot, nbo, SUBLANES, LANES), F32),
        grid=(2, nslab, nc + 2),
        in_specs=[pl.BlockSpec((None, t_len, nbo, SUBLANES, LANES), lambda d, s, c: (s, chunk(d, c), 0, 0, 0)),
                  pspec(LANES, 2 * S5_SLAB_STATE), pspec(S5_SLAB_STATE, LANES), pspec(S5_SLAB_STATE, LANES),
                  pspec(1, S5_SLAB_STATE), pspec(1, S5_SLAB_STATE)],
        out_specs=pl.BlockSpec((None, None, t_len, nbo, SUBLANES, LANES),
                               lambda d, s, c: (d, s, chunk(d, c - 2), 0, 0, 0)),
        scratch_shapes=[pltpu.VMEM((rows, 2 * S5_SLAB_STATE), F32)] * 2
                       + [pltpu.VMEM((rows, 2 * S5_SLAB_STATE), BF16)] * 2
                       + [pltpu.VMEM((2, nbo, SUBLANES, S5_SLAB_STATE), F32)],
        compiler_params=_cparams(("parallel", "parallel", "arbitrary")),
        name="s5_scan",
    )(u5, bm, cr, ci, lam_r, lam_i)


def _s5_out_kernel(y_ref, u_ref, dsk_ref, w_ref, bg_ref, ctx_ref, x_ref, mod_ref, o_ref, tr_ref,
                   *, n_ctx_tiles, nb, d):
    i = pl.program_id(0)
    nslab = d // LANES
    rows = u_ref.shape[1]
    tt = rows // nb
    y = (y_ref[0] + y_ref[1]).reshape(nslab, rows, LANES)
    v = jax.nn.gelu(y + dsk_ref[...] * u_ref[...])
    lhs = jnp.concatenate([v[s] for s in range(nslab)], axis=-1).astype(BF16)
    z = jnp.dot(lhs, w_ref[...], preferred_element_type=F32) + bg_ref[...]
    m = z[:, 0:d] * jax.nn.sigmoid(z[:, d:2 * d])
    for s in range(nslab):
        tr_ref[s] = m[:, s * LANES:(s + 1) * LANES]

    def emit(src_ref, mod_row):
        for b in range(nb):
            r = mod_row(b)
            gt = mod_ref[r:r + 1, :]
            mb = jnp.concatenate([tr_ref[s, pl.ds(b, tt, stride=nb), :] for s in range(nslab)], axis=-1)
            o_ref[b] = src_ref[b] + gt * mb

    @pl.when(i < n_ctx_tiles)
    def _():
        emit(ctx_ref, lambda b: nb)

    @pl.when(i >= n_ctx_tiles)
    def _():
        emit(x_ref, lambda b: b)


def _s5_out(y_t, u_t, d_skip, w_glu, b_glu, ctx, x, gt1):
    nb, n_ctx, d = ctx.shape
    n_lat = x.shape[1]
    n_tot = n_ctx + n_lat
    nslab, nbo = d // LANES, nb // SUBLANES
    tt = TM_TILE
    nct, nlt = n_ctx // tt, n_lat // tt
    kern = functools.partial(_s5_out_kernel, n_ctx_tiles=nct, nb=nb, d=d)
    return pl.pallas_call(
        kern,
        out_shape=jax.ShapeDtypeStruct((nb, n_tot, d), F32),
        grid=(nct + nlt,),
        in_specs=[pl.BlockSpec((2, nslab, tt, nbo, SUBLANES, LANES), lambda i: (0, 0, i, 0, 0, 0)),
                  pl.BlockSpec((nslab, tt * nb, LANES), lambda i: (0, i, 0)),
                  pl.BlockSpec((nslab, 1, LANES), lambda i: (0, 0, 0)),
                  pl.BlockSpec((d, 2 * d), lambda i: (0, 0)),
                  pl.BlockSpec((1, 2 * d), lambda i: (0, 0)),
                  pl.BlockSpec((nb, tt, d), lambda i: (0, jnp.minimum(i, nct - 1), 0)),
                  pl.BlockSpec((nb, tt, d), lambda i: (0, jnp.maximum(i - nct, 0), 0)),
                  pl.BlockSpec((MOD_ROWS, d), lambda i: (0, 0))],
        out_specs=pl.BlockSpec((nb, tt, d), lambda i: (0, i, 0)),
        scratch_shapes=[pltpu.VMEM((nslab, tt * nb, LANES), F32)],
        compiler_params=_cparams(("parallel",)),
        name="s5_out",
    )(y_t, u_t, d_skip.reshape(nslab, 1, LANES), w_glu.astype(BF16), b_glu.reshape(1, 2 * d), ctx, x, gt1)


def _s5_layer(xs_ctx, xs_lat, mod, g_mix, a_re, a_im, log_dt, b_re, b_im, c_re, c_im, d_skip, w_glu, b_glu):
    nb, n_ctx, d = xs_ctx.shape
    n_tot = n_ctx + xs_lat.shape[1]
    u_t = _prenorm_tm(xs_ctx, xs_lat, mod[:, 0:2 * d], g_mix.reshape(1, d))
    lam_re, lam_im, bb_re, bb_im = _s5_disc(a_re, a_im, log_dt, b_re, b_im)
    bm, cr, ci, lam_r, lam_i = _s5_slab_params(lam_re, lam_im, bb_re, bb_im, c_re, c_im)
    y_t = _s5_scan(u_t, bm, cr, ci, lam_r, lam_i, nb, n_ctx, n_tot)
    return _s5_out(y_t, u_t, d_skip, w_glu, b_glu, xs_ctx, xs_lat, mod[:, 2 * d:3 * d])


NEG_PAD = -1e30
NEG_MASK = -3e38


def _split_bf16(a):
    hi = a.astype(BF16)
    return hi, (a - hi.astype(F32)).astype(BF16)


def _mod_row(piece, pieces_per_batch, n_ctx_pieces, nb):
    b = piece // pieces_per_batch
    return jnp.where(piece - b * pieces_per_batch < n_ctx_pieces, nb, b)


def _router_kernel(x_ref, mod_ref, g_ref, w_ref, bias_ref, h_ref, sel_ref, selt_ref, meta_ref, cnt_ref,
                   carry_ref, *, pieces_per_batch, n_ctx_pieces, nb, d):
    i = pl.program_id(0)
    rows = x_ref.shape[0]

    @pl.when(i == 0)
    def _():
        carry_ref[...] = jnp.zeros_like(carry_ref)

    parts = []
    for p in range(rows // ROW_TILE):
        row = _mod_row(i * (rows // ROW_TILE) + p, pieces_per_batch, n_ctx_pieces, nb)
        mrow = mod_ref[pl.ds(row, 1), :]
        xp = x_ref[p * ROW_TILE:(p + 1) * ROW_TILE, :]
        parts.append((_rms(xp) * g_ref[...]) * (1.0 + mrow[:, d:2 * d]) + mrow[:, 0:d])
    h = jnp.concatenate(parts, axis=0)
    h_ref[...] = h.astype(BF16)

    hi, lo = _split_bf16(h)
    whi, wlo = _split_bf16(w_ref[...])
    dot = functools.partial(jnp.dot, preferred_element_type=F32)
    logits = dot(hi, whi) + dot(lo, whi) + dot(hi, wlo) + bias_ref[...]

    lane = lax.broadcasted_iota(jnp.int32, (rows, LANES), 1)
    lane_f = lane.astype(F32)
    work = logits
    tops, idxs, hots = [], [], []
    for _ in range(TOP_K):
        m = jnp.max(work, axis=-1, keepdims=True)
        idx = jnp.min(jnp.where(work == m, lane_f, float(LANES)), axis=-1, keepdims=True)
        hot = lane_f == idx
        work = jnp.where(hot, NEG_MASK, work)
        tops.append(m)
        idxs.append(idx)
        hots.append(hot)
    exps = [jnp.exp(m - tops[0]) for m in tops]
    denom = exps[0] + exps[1] + exps[2] + exps[3]

    multi = jnp.zeros((rows, LANES), F32)
    for hot in hots:
        multi = multi + jnp.where(hot, 1.0, 0.0)
    r_i = lax.broadcasted_iota(jnp.int32, (rows, rows), 0)
    c_i = lax.broadcasted_iota(jnp.int32, (rows, rows), 1)
    tri = jnp.where(c_i < r_i, 1.0, 0.0).astype(BF16)
    in_tile = dot(tri, multi.astype(BF16))
    cnt_tile = jnp.sum(multi, axis=0, keepdims=True)
    e_r = lax.broadcasted_iota(jnp.int32, (LANES, LANES), 0)
    e_c = lax.broadcasted_iota(jnp.int32, (LANES, LANES), 1)
    below = jnp.where(e_r < e_c, 1.0, 0.0).astype(BF16)
    c_hi, c_lo = _split_bf16(jnp.broadcast_to(cnt_tile, (SUBLANES, LANES)))
    prefix = (dot(c_hi, below) + dot(c_lo, below))[0:1, :]
    slot = in_tile + prefix

    sel = jnp.zeros((rows, LANES), F32)
    for k in range(TOP_K):
        pos = jnp.sum(jnp.where(hots[k], slot, 0.0), axis=-1, keepdims=True)
        sel = jnp.where(lane == k, idxs[k].astype(F32), sel)
        sel = jnp.where(lane == TOP_K + k, exps[k] / denom, sel)
        sel = jnp.where(lane == 2 * TOP_K + k, pos, sel)
    sel_ref[...] = sel
    selt_ref[...] = sel.T[2 * TOP_K:2 * TOP_K + SUBLANES, :].astype(jnp.int32)

    carry = carry_ref[0:1, :]
    sub = lax.broadcasted_iota(jnp.int32, (SUBLANES, LANES), 0)
    meta = jnp.where(sub == 0, cnt_tile, jnp.where(sub == 1, carry, jnp.where(sub == 2, prefix, 0.0)))
    meta_ref[...] = meta.astype(jnp.int32)
    carry_ref[0:1, :] = carry + cnt_tile
    cnt_ref[...] = jnp.broadcast_to(carry + cnt_tile, cnt_ref.shape).astype(jnp.int32)


def _router(xs, n_ctx, mod_shsc, g, w_router, b_router):
    nb, t_len, d = xs.shape
    r = MOE_TILE
    n_tok = nb * t_len
    nt = n_tok // r
    ne = w_router.shape[1]
    w_pad = jnp.zeros((d, LANES), F32).at[:, :ne].set(w_router)
    b_pad = jnp.full((1, LANES), NEG_PAD, F32).at[0, :ne].set(b_router)
    kern = functools.partial(_router_kernel, pieces_per_batch=t_len // ROW_TILE,
                             n_ctx_pieces=n_ctx // ROW_TILE, nb=nb, d=d)
    tile = lambda i: (i, 0)
    const = lambda i: (0, 0)
    return pl.pallas_call(
        kern,
        out_shape=[jax.ShapeDtypeStruct((n_tok, d), BF16),
                   jax.ShapeDtypeStruct((n_tok, LANES), F32),
                   jax.ShapeDtypeStruct((nt * SUBLANES, r), jnp.int32),
                   jax.ShapeDtypeStruct((nt, SUBLANES, LANES), jnp.int32),
                   jax.ShapeDtypeStruct((SUBLANES, LANES), jnp.int32)],
        grid=(nt,),
        in_specs=[pl.BlockSpec((r, d), tile),
                  pl.BlockSpec((MOD_ROWS, 2 * d), const),
                  pl.BlockSpec((1, d), const),
                  pl.BlockSpec((d, LANES), const),
                  pl.BlockSpec((1, LANES), const)],
        out_specs=[pl.BlockSpec((r, d), tile),
                   pl.BlockSpec((r, LANES), tile),
                   pl.BlockSpec((SUBLANES, r), tile),
                   pl.BlockSpec((None, SUBLANES, LANES), lambda i: (i, 0, 0)),
                   pl.BlockSpec((SUBLANES, LANES), const)],
        scratch_shapes=[pltpu.VMEM((SUBLANES, LANES), F32)],
        compiler_params=_cparams(("arbitrary",)),
        name="router",
    )(xs.reshape(n_tok, d), mod_shsc, g.reshape(1, d), w_pad, b_pad)


def _plan_kernel(cnt_ref, ps_ref, be_ref, first_ref, nused_ref, *, ne, rb):
    n_blocks = be_ref.shape[0]

    def fill(i, carry):
        be_ref[i] = ne - 1
        first_ref[i] = 0
        return carry
    lax.fori_loop(0, n_blocks, fill, 0)

    def per_expert(e, acc):
        nblk = (cnt_ref[0, e] + (rb - 1)) // rb
        ps_ref[e] = acc
        b0 = acc // rb

        def mark(i, carry):
            be_ref[b0 + i] = e
            first_ref[b0 + i] = jnp.where(i == 0, 1, 0)
            return carry
        lax.fori_loop(0, nblk, mark, 0)
        return acc + nblk * rb

    total = lax.fori_loop(0, ne, per_expert, 0)

    def tail(e, carry):
        ps_ref[e] = total
        return carry
    lax.fori_loop(ne, ps_ref.shape[0], tail, 0)
    nused_ref[0] = total // rb


def _plan(cnt, ne, n_blocks):
    smem = pl.BlockSpec(memory_space=pltpu.SMEM)
    return pl.pallas_call(
        functools.partial(_plan_kernel, ne=ne, rb=EXPERT_ROWS),
        out_shape=[jax.ShapeDtypeStruct((LANES,), jnp.int32),
                   jax.ShapeDtypeStruct((n_blocks,), jnp.int32),
                   jax.ShapeDtypeStruct((n_blocks,), jnp.int32),
                   jax.ShapeDtypeStruct((1,), jnp.int32)],
        in_specs=[smem],
        out_specs=[smem, smem, smem, smem],
        name="plan",
    )(cnt)


_SEG_SIZES = tuple(MOE_TILE >> s for s in range(MOE_TILE.bit_length()))
TILE_SLOTS = MOE_TILE * TOP_K
TOKEN_ROWS = SUBLANES


def _segment_copies(n, src, dst, make_copy, wait=False):
    for p in _SEG_SIZES:
        take = (n & p) != 0

        @pl.when(take)
        def _():
            cp = make_copy(pl.multiple_of(src * TOKEN_ROWS, TOKEN_ROWS),
                           pl.multiple_of(dst * TOKEN_ROWS, TOKEN_ROWS), p * TOKEN_ROWS)
            cp.wait() if wait else cp.start()
        step = jnp.where(take, p, 0)
        src = src + step
        dst = dst + step


def _tile_major(x):
    return [x[:, s * LANES:(s + 1) * LANES] for s in range(x.shape[1] // LANES)]


def _dispatch_kernel(meta_ref, metap_ref, ps_ref, cnt_ref, h_ref, selt_ref, xbuf_ref, xs_ref, zero_ref, sem,
                     zsem, *, ne, rb):
    rows = h_ref.shape[0]
    slots = xs_ref.shape[1] // TOKEN_ROWS

    @pl.when(pl.program_id(0) == 0)
    def _():
        zero_ref[...] = jnp.zeros_like(zero_ref)

        def pad_copy(_, dst, p):
            return pltpu.make_async_copy(zero_ref.at[pl.ds(0, p), :], xbuf_ref.at[pl.ds(dst, p), :], zsem)

        for wait in (False, True):
            def pad_rows(e, carry, wait=wait):
                cnt = cnt_ref[0, e]
                _segment_copies((-cnt) & (rb - 1), 0, ps_ref[e] + cnt, pad_copy, wait)
                return carry
            lax.fori_loop(0, ne, pad_rows, 0)

    pos = selt_ref[0:TOP_K, :]
    s_iota = lax.broadcasted_iota(jnp.int32, (slots, rows), 0)
    perm = jnp.zeros((slots, rows), F32)
    for k in range(TOP_K):
        perm = perm + jnp.where(s_iota == pos[k:k + 1, :], 1.0, 0.0)
    step = pl.program_id(0)
    slot = lax.rem(step, 2)
    sorted_rows = jnp.dot(perm.astype(BF16), h_ref[...], preferred_element_type=F32)
    for s, slab in enumerate(_tile_major(sorted_rows)):
        xs_ref[slot, pl.ds(s, slots, stride=TOKEN_ROWS), :] = slab

    def segments(mref, which, wait):
        def seg_copy(src, dst, p):
            return pltpu.make_async_copy(xs_ref.at[which, pl.ds(src, p), :], xbuf_ref.at[pl.ds(dst, p), :],
                                         sem.at[which])

        def per_expert(e, carry):
            _segment_copies(mref[0, e], mref[2, e], ps_ref[e] + mref[1, e], seg_copy, wait)
            return carry
        lax.fori_loop(0, ne, per_expert, 0)

    segments(meta_ref, slot, False)

    @pl.when(step > 0)
    def _():
        segments(metap_ref, 1 - slot, True)

    @pl.when(step == pl.num_programs(0) - 1)
    def _():
        segments(meta_ref, slot, True)


def _dispatch(h2, selt, meta, ps, cnt, ne, n_buf):
    n_tok, d = h2.shape
    r = MOE_TILE
    smem = pl.BlockSpec(memory_space=pltpu.SMEM)
    return pl.pallas_call(
        functools.partial(_dispatch_kernel, ne=ne, rb=EXPERT_ROWS),
        out_shape=jax.ShapeDtypeStruct((n_buf * TOKEN_ROWS, LANES), F32),
        grid=(n_tok // r,),
        in_specs=[pl.BlockSpec((None, SUBLANES, LANES), lambda i: (i, 0, 0), memory_space=pltpu.SMEM),
                  pl.BlockSpec((None, SUBLANES, LANES), lambda i: (jnp.maximum(i - 1, 0), 0, 0),
                               memory_space=pltpu.SMEM),
                  smem, smem,
                  pl.BlockSpec((r, d), lambda i: (i, 0)),
                  pl.BlockSpec((SUBLANES, r), lambda i: (i, 0))],
        out_specs=pl.BlockSpec(memory_space=pl.ANY),
        scratch_shapes=[pltpu.VMEM((2, TILE_SLOTS * TOKEN_ROWS, LANES), F32),
                        pltpu.VMEM((_SEG_SIZES[0] * TOKEN_ROWS, LANES), F32),
                        pltpu.SemaphoreType.DMA((2,)), pltpu.SemaphoreType.DMA(())],
        compiler_params=_cparams(("arbitrary",)),
        name="dispatch",
    )(meta, meta, ps, cnt, h2, selt)


def _expert_kernel(be_ref, first_ref, nused_ref, x_ref, wgu_ref, bgu_ref, wd_ref, bd_ref, y_ref,
                   wgu_bf, wd_bf):
    i = pl.program_id(0)
    dff = wd_ref.shape[0]

    @pl.when(i < nused_ref[0])
    def _():
        @pl.when(first_ref[i] == 1)
        def _():
            wgu_bf[...] = wgu_ref[...].astype(BF16)
            wd_bf[...] = wd_ref[...].astype(BF16)

        rows = x_ref.shape[0] // TOKEN_ROWS
        x = jnp.concatenate([x_ref[pl.ds(s, rows, stride=TOKEN_ROWS), :] for s in range(TOKEN_ROWS)], axis=-1)
        gu = jnp.dot(x.astype(BF16), wgu_bf[...], preferred_element_type=F32) + bgu_ref[...]
        gate = jnp.minimum(gu[:, 0:dff], SWIGLU_LIMIT)
        up = jnp.clip(gu[:, dff:2 * dff], -SWIGLU_LIMIT, SWIGLU_LIMIT)
        act = (up + 1.0) * gate * jax.nn.sigmoid(SWIGLU_ALPHA * gate)
        y = jnp.dot(act.astype(BF16), wd_bf[...], preferred_element_type=F32) + bd_ref[...]
        for s, slab in enumerate(_tile_major(y)):
            y_ref[pl.ds(s, rows, stride=TOKEN_ROWS), :] = slab

    @pl.when(i >= nused_ref[0])
    def _():
        y_ref[...] = jnp.zeros_like(y_ref)


def _experts(xbuf, block_e, first, n_used, layer, w_gu, b_gu, w_down, b_down):
    depth, ne, d, dff2 = w_gu.shape
    assert d == TOKEN_ROWS * LANES
    dff = dff2 // 2
    rb = EXPERT_ROWS * TOKEN_ROWS
    blk = lambda i, nu: jnp.minimum(i, nu[0] - 1)
    wmap = lambda i, be, fi, nu: (layer, be[blk(i, nu)], 0, 0)
    grid_spec = pltpu.PrefetchScalarGridSpec(
        num_scalar_prefetch=3,
        grid=(xbuf.shape[0] // rb,),
        in_specs=[pl.BlockSpec((rb, LANES), lambda i, be, fi, nu: (blk(i, nu), 0)),
                  pl.BlockSpec((None, None, d, dff2), wmap),
                  pl.BlockSpec((None, None, 1, dff2), wmap),
                  pl.BlockSpec((None, None, dff, d), wmap),
                  pl.BlockSpec((None, None, 1, d), wmap)],
        out_specs=pl.BlockSpec((rb, LANES), lambda i, be, fi, nu: (i, 0)),
        scratch_shapes=[pltpu.VMEM((d, dff2), BF16), pltpu.VMEM((dff, d), BF16)])
    return pl.pallas_call(
        _expert_kernel,
        out_shape=jax.ShapeDtypeStruct(xbuf.shape, F32),
        grid_spec=grid_spec,
        compiler_params=_cparams(("arbitrary",)),
        name="experts",
    )(block_e, first, n_used, xbuf, w_gu, b_gu.reshape(depth, ne, 1, dff2), w_down,
      b_down.reshape(depth, ne, 1, d))


def _combine_kernel(meta_ref, metan_ref, ps_ref, sel_ref, x_ref, mod_ref, ybuf_ref, o_ref, ys_ref, sem,
                    *, pieces_per_batch, n_ctx_pieces, nb, ne):
    step = pl.program_id(0)
    n_steps = pl.num_programs(0)
    slot = lax.rem(step, 2)
    rows = x_ref.shape[0]
    slots = ys_ref.shape[1] // TOKEN_ROWS

    def segments(mref, to_slot, wait):
        def seg_copy(src, dst, p):
            return pltpu.make_async_copy(ybuf_ref.at[pl.ds(dst, p), :], ys_ref.at[to_slot, pl.ds(src, p), :],
                                         sem.at[to_slot])

        def per_expert(e, carry):
            _segment_copies(mref[0, e], mref[2, e], ps_ref[e] + mref[1, e], seg_copy, wait)
            return carry
        lax.fori_loop(0, ne, per_expert, 0)

    @pl.when(step == 0)
    def _():
        segments(meta_ref, 0, False)

    @pl.when(step + 1 < n_steps)
    def _():
        segments(metan_ref, 1 - slot, False)

    segments(meta_ref, slot, True)

    sel = sel_ref[...]
    l_iota = lax.broadcasted_iota(jnp.int32, (rows, slots), 1).astype(F32)
    w = jnp.zeros((rows, slots), F32)
    for k in range(TOP_K):
        w = w + jnp.where(l_iota == sel[:, 2 * TOP_K + k:2 * TOP_K + k + 1], sel[:, TOP_K + k:TOP_K + k + 1], 0.0)
    ys = jnp.concatenate([ys_ref[slot, pl.ds(s, slots, stride=TOKEN_ROWS), :] for s in range(TOKEN_ROWS)],
                         axis=-1).astype(BF16)
    acc = jnp.dot(w.astype(BF16), ys, preferred_element_type=F32)
    for p in range(rows // ROW_TILE):
        row = _mod_row(step * (rows // ROW_TILE) + p, pieces_per_batch, n_ctx_pieces, nb)
        psl = slice(p * ROW_TILE, (p + 1) * ROW_TILE)
        o_ref[psl, :] = x_ref[psl, :] + mod_ref[pl.ds(row, 1), :] * acc[psl, :]


def _combine(ybuf, meta, ps, sel, xs, n_ctx, mod_gt, ne):
    nb, t_len, d = xs.shape
    r = MOE_TILE
    n_tok = nb * t_len
    nt = n_tok // r
    kern = functools.partial(_combine_kernel, pieces_per_batch=t_len // ROW_TILE,
                             n_ctx_pieces=n_ctx // ROW_TILE, nb=nb, ne=ne)
    mspec = lambda f: pl.BlockSpec((None, SUBLANES, LANES), f, memory_space=pltpu.SMEM)
    out = pl.pallas_call(
        kern,
        out_shape=jax.ShapeDtypeStruct((n_tok, d), F32),
        grid=(nt,),
        in_specs=[mspec(lambda i: (i, 0, 0)),
                  mspec(lambda i: (jnp.minimum(i + 1, nt - 1), 0, 0)),
                  pl.BlockSpec(memory_space=pltpu.SMEM),
                  pl.BlockSpec((r, LANES), lambda i: (i, 0)),
                  pl.BlockSpec((r, d), lambda i: (i, 0)),
                  pl.BlockSpec((MOD_ROWS, d), lambda i: (0, 0)),
                  pl.BlockSpec(memory_space=pl.ANY)],
        out_specs=pl.BlockSpec((r, d), lambda i: (i, 0)),
        scratch_shapes=[pltpu.VMEM((2, TILE_SLOTS * TOKEN_ROWS, LANES), F32), pltpu.SemaphoreType.DMA((2,))],
        compiler_params=_cparams(("arbitrary",)),
        name="combine",
    )(meta, meta, ps, sel, xs.reshape(n_tok, d), mod_gt, ybuf)
    return out.reshape(nb, t_len, d)


def _moe_layer(xs, n_ctx, mod_shsc, mod_gt, g, layer, w_router, b_router, w_gu, b_gu, w_down, b_down):
    nb, t_len, d = xs.shape
    ne = w_router.shape[-1]
    n_tiles = nb * t_len // MOE_TILE
    n_blocks = n_tiles * TILE_SLOTS // EXPERT_ROWS + ne
    h2, sel, selt, meta, cnt = _router(xs, n_ctx, mod_shsc, g, w_router[layer], b_router[layer])
    ps, block_e, first, n_used = _plan(cnt, ne, n_blocks)
    xbuf = _dispatch(h2, selt, meta, ps, cnt, ne, n_blocks * EXPERT_ROWS)
    ybuf = _experts(xbuf, block_e, first, n_used, layer, w_gu, b_gu, w_down, b_down)
    return _combine(ybuf, meta, ps, sel, xs, n_ctx, mod_gt, ne)


def _qkv_kernel(x_ref, mod_ref, g_ref, w_ref, q_ref, k_ref, v_ref, *, n_ctx_tiles, nb, d):
    b = pl.program_id(0)
    j = pl.program_id(1)
    row = jnp.where(j < n_ctx_tiles, nb, b)
    mrow = mod_ref[pl.ds(row, 1), :]
    h = (_rms(x_ref[...]) * g_ref[...]) * (1.0 + mrow[:, d:2 * d]) + mrow[:, 0:d]
    qkv = jnp.dot(h.astype(BF16), w_ref[...], preferred_element_type=F32)
    for hd in range(d // DA_VDIM):
        lo = hd * DA_VDIM
        q_ref[hd] = qkv[:, lo:lo + DA_VDIM].astype(BF16)
        k_ref[hd] = qkv[:, d + lo:d + lo + DA_VDIM].astype(BF16)
        v_ref[hd] = qkv[:, 2 * d + lo:2 * d + lo + DA_VDIM].astype(BF16)


def _qkv(xs, n_ctx, mod_shsc, g, w_qkv):
    nb, t_len, d = xs.shape
    r = ROW_TILE
    nh = d // DA_VDIM
    kern = functools.partial(_qkv_kernel, n_ctx_tiles=n_ctx // r, nb=nb, d=d)
    nct = n_ctx // r
    hspec = pl.BlockSpec((None, nh, r, DA_VDIM), lambda b, j: (b, 0, j, 0))
    qspec = pl.BlockSpec((None, nh, r, DA_VDIM), lambda b, j: (b, 0, jnp.maximum(j - nct, 0), 0))
    kv_shape = jax.ShapeDtypeStruct((nb, nh, t_len, DA_VDIM), BF16)
    return pl.pallas_call(
        kern,
        out_shape=[jax.ShapeDtypeStruct((nb, nh, t_len - n_ctx, DA_VDIM), BF16), kv_shape, kv_shape],
        grid=(nb, t_len // r),
        in_specs=[pl.BlockSpec((None, r, d), lambda b, j: (b, j, 0)),
                  pl.BlockSpec((MOD_ROWS, 2 * d), lambda b, j: (0, 0)),
                  pl.BlockSpec((1, d), lambda b, j: (0, 0)),
                  pl.BlockSpec((d, 3 * d), lambda b, j: (0, 0))],
        out_specs=[qspec, hspec, hspec],
        compiler_params=_cparams(("parallel", "arbitrary")),
        name="qkv",
    )(xs, mod_shsc, g.reshape(1, d), w_qkv.astype(BF16))


def _attn_kernel(q_ref, k_ref, v_ref, qg_ref, kg_ref, cosq_ref, sinq_ref, cosk_ref, sink_ref, lam_ref,
                 sg_ref, o_ref, kn_ref, *, n_ctx, lambda_init):
    qi = pl.program_id(2)
    n_heads, n_tot = k_ref.shape[0], k_ref.shape[1]
    tq = q_ref.shape[1]
    dh = DA_HEAD_DIM
    lane = lax.broadcasted_iota(jnp.int32, (1, DA_VDIM), 1)
    comp0 = lane < dh
    low_half = lax.rem(lane, dh) < dh // 2

    def head_norm(xv, gain):
        x2 = xv * xv
        s0 = jnp.sum(jnp.where(comp0, x2, 0.0), axis=-1, keepdims=True)
        s1 = jnp.sum(jnp.where(comp0, 0.0, x2), axis=-1, keepdims=True)
        ms = jnp.where(comp0, s0, s1) * (1.0 / dh)
        return xv * lax.rsqrt(ms + NORM_EPS) * gain

    def rope(xv, cos, sin_signed):
        partner = jnp.where(low_half, pltpu.roll(xv, DA_VDIM - dh // 2, axis=1),
                            pltpu.roll(xv, dh // 2, axis=1))
        return xv * cos + partner * sin_signed

    @pl.when(qi == 0)
    def _():
        kg = kg_ref[...]
        for g in range(n_heads):
            kn_ref[g, 0:n_ctx, :] = head_norm(k_ref[g, 0:n_ctx, :].astype(F32), kg).astype(BF16)
            for c0 in range(n_ctx, n_tot, tq):
                kk = head_norm(k_ref[g, c0:c0 + tq, :].astype(F32), kg)
                kn_ref[g, c0:c0 + tq, :] = rope(kk, cosk_ref[c0 - n_ctx:c0 - n_ctx + tq, :],
                                                sink_ref[c0 - n_ctx:c0 - n_ctx + tq, :]).astype(BF16)

    lam_v = lam_ref[...]
    lam = (jnp.exp(jnp.sum(lam_v[0:1] * lam_v[1:2], axis=-1, keepdims=True))
           - jnp.exp(jnp.sum(lam_v[2:3] * lam_v[3:4], axis=-1, keepdims=True)) + lambda_init)
    dims = (((1,), (1,)), ((), ()))
    rb = ATTN_ROW_BLOCK
    units = [(g, r * rb) for g in range(n_heads) for r in range(tq // rb)]

    def scores(g, r0):
        qn = rope(head_norm(q_ref[g, r0:r0 + rb, :].astype(F32), qg_ref[...]),
                  cosq_ref[r0:r0 + rb, :], sinq_ref[r0:r0 + rb, :]) * (dh ** -0.5 * LOG2_E)
        qq = jnp.concatenate([jnp.where(comp0, qn, 0.0), jnp.where(comp0, 0.0, qn)], axis=0).astype(BF16)
        return lax.dot_general(qq, kn_ref[g], dims, preferred_element_type=F32)

    def softmax_mix(s):
        s0, s1 = s[0:rb], s[rb:2 * rb]
        e0 = jnp.exp2(s0 - jnp.max(s0, axis=-1, keepdims=True))
        e1 = jnp.exp2(s1 - jnp.max(s1, axis=-1, keepdims=True))
        l0 = jnp.sum(e0, axis=-1, keepdims=True)
        l1 = jnp.sum(e1, axis=-1, keepdims=True)
        return e0.astype(BF16) - (lam * l0 / l1).astype(BF16) * e1.astype(BF16), l0

    def values(g, r0, a, l0):
        o = jnp.dot(a, v_ref[g], preferred_element_type=F32) * (1.0 / l0)
        o_ref[r0:r0 + rb, g * DA_VDIM:(g + 1) * DA_VDIM] = (
            _rms(o) * sg_ref[...] * (1.0 - lambda_init)).astype(BF16)

    s_of, a_of = {}, {}
    for i in range(len(units) + 2):
        if i < len(units):
            s_of[i] = scores(*units[i])
        if 0 <= i - 1 < len(units):
            a_of[i - 1] = softmax_mix(s_of.pop(i - 1))
        if 0 <= i - 2 < len(units):
            values(*units[i - 2], *a_of.pop(i - 2))


def _rope_tables(n_tokens):
    rows = n_tokens // GRID_W
    row = jnp.repeat(jnp.arange(rows, dtype=F32), GRID_W)
    col = jnp.tile(jnp.arange(GRID_W, dtype=F32), rows)
    n_freq = DA_HEAD_DIM // 4
    inv_freq = jnp.exp(-math.log(ROPE_BASE) * jnp.arange(n_freq, dtype=F32) / n_freq)
    ang = jnp.concatenate([row[:, None] * inv_freq, col[:, None] * inv_freq], axis=-1)
    cos, sin = jnp.cos(ang), jnp.sin(ang)
    return jnp.tile(cos, (1, 4)), jnp.tile(jnp.concatenate([-sin, sin], axis=-1), (1, 2))


def _attention(q, k, v, n_ctx, q_gain, k_gain, lam_vecs, sub_gain, lambda_init):
    nb, nh, n_tot, dv = k.shape
    n_lat = n_tot - n_ctx
    tq = Q_TILE
    cos_t, sin_t = _rope_tables(n_lat)
    gain2 = lambda gv: jnp.tile(gv, 2).reshape(1, dv)
    lam_pad = jnp.zeros((SUBLANES, dv), F32).at[0:4, 0:DA_HEAD_DIM].set(lam_vecs)
    kern = functools.partial(_attn_kernel, n_ctx=n_ctx, lambda_init=lambda_init)
    const = lambda shape: pl.BlockSpec(shape, lambda b, h, i: (0, 0))
    hg = HEADS_PER_STEP
    return pl.pallas_call(
        kern,
        out_shape=jax.ShapeDtypeStruct((nb, n_lat, nh * dv), BF16),
        grid=(nb, nh // hg, n_lat // tq),
        in_specs=[pl.BlockSpec((None, hg, tq, dv), lambda b, h, i: (b, h, i, 0)),
                  pl.BlockSpec((None, hg, n_tot, dv), lambda b, h, i: (b, h, 0, 0)),
                  pl.BlockSpec((None, hg, n_tot, dv), lambda b, h, i: (b, h, 0, 0)),
                  const((1, dv)), const((1, dv)),
                  pl.BlockSpec((tq, dv), lambda b, h, i: (i, 0)),
                  pl.BlockSpec((tq, dv), lambda b, h, i: (i, 0)),
                  const((n_lat, dv)), const((n_lat, dv)),
                  const((SUBLANES, dv)), const((1, dv))],
        out_specs=pl.BlockSpec((None, tq, hg * dv), lambda b, h, i: (b, i, h)),
        scratch_shapes=[pltpu.VMEM((hg, n_tot, dv), BF16)],
        compiler_params=_cparams(("parallel", "parallel", "arbitrary")),
        name="attention",
    )(q, k, v, gain2(q_gain), gain2(k_gain), cos_t, sin_t, cos_t, sin_t, lam_pad, sub_gain.reshape(1, dv))


def _oproj_kernel(o_ref, w_ref, x_ref, mod_ref, out_ref):
    b = pl.program_id(0)
    y = jnp.dot(o_ref[...], w_ref[...], preferred_element_type=F32)
    out_ref[...] = x_ref[...] + mod_ref[pl.ds(b, 1), :] * y


def _oproj(o, w_o, xs, n_ctx, mod_gt):
    nb, n_lat, d = o.shape
    r = ROW_TILE
    off = n_ctx // r
    return pl.pallas_call(
        _oproj_kernel,
        out_shape=jax.ShapeDtypeStruct((nb, n_lat, d), F32),
        grid=(nb, n_lat // r),
        in_specs=[pl.BlockSpec((None, r, d), lambda b, j: (b, j, 0)),
                  pl.BlockSpec((d, d), lambda b, j: (0, 0)),
                  pl.BlockSpec((None, r, d), lambda b, j: (b, j + off, 0)),
                  pl.BlockSpec((MOD_ROWS, d), lambda b, j: (0, 0))],
        out_specs=pl.BlockSpec((None, r, d), lambda b, j: (b, j, 0)),
        compiler_params=_cparams(("parallel", "parallel")),
        name="oproj",
    )(o, w_o.astype(BF16), xs, mod_gt)


def _attn_layer(xs, n_ctx, mod, g_mix, w_qkv, w_o, q_gain, k_gain, lq1, lk1, lq2, lk2, sub_gain, lambda_init):
    d = xs.shape[-1]
    q, k, v = _qkv(xs, n_ctx, mod[:, 0:2 * d], g_mix, w_qkv)
    o = _attention(q, k, v, n_ctx, q_gain, k_gain, jnp.stack([lq1, lk1, lq2, lk2]), sub_gain, lambda_init)
    return _oproj(o, w_o, xs, n_ctx, mod[:, 2 * d:3 * d])


def kernel(x, c, ctx, c_ctx, w_ada, b_ada, g_mix, g_ffn, s5_a_re, s5_a_im, s5_log_dt, s5_b_re, s5_b_im, s5_c_re, s5_c_im, s5_d, s5_w_glu, s5_b_glu, da_w_qkv, da_w_o, da_q_gain, da_k_gain, da_lam_q1, da_lam_k1, da_lam_q2, da_lam_k2, da_sub_gain, moe_w_router, moe_b_router, moe_w_gu, moe_b_gu, moe_w_down, moe_b_down):
    nb, n_lat, d = x.shape
    n_ctx = ctx.shape[1]
    depth = w_ada.shape[0]
    assert depth == 2 and nb % SUBLANES == 0 and nb + 1 <= MOD_ROWS
    cc = jnp.concatenate([c, c_ctx[None], jnp.zeros((MOD_ROWS - nb - 1, d), F32)], axis=0)
    mod = _ada(cc, w_ada, b_ada)
    xs = _s5_layer(ctx, x, mod[0], g_mix[0], s5_a_re[0], s5_a_im[0], s5_log_dt[0], s5_b_re[0], s5_b_im[0],
                   s5_c_re[0], s5_c_im[0], s5_d[0], s5_w_glu[0], s5_b_glu[0])
    xs = _moe_layer(xs, n_ctx, mod[0][:, 3 * d:5 * d], mod[0][:, 5 * d:6 * d], g_ffn[0], 0,
                    moe_w_router, moe_b_router, moe_w_gu, moe_b_gu, moe_w_down, moe_b_down)
    lambda_init = 0.8 - 0.6 * math.exp(-0.3 * 1)
    x1 = _attn_layer(xs, n_ctx, mod[1], g_mix[1], da_w_qkv[0], da_w_o[0], da_q_gain[0], da_k_gain[0],
                     da_lam_q1[0], da_lam_k1[0], da_lam_q2[0], da_lam_k2[0], da_sub_gain[0], lambda_init)
    return _moe_layer(x1, 0, mod[1][:, 3 * d:5 * d], mod[1][:, 5 * d:6 * d], g_ffn[1], 1,
                      moe_w_router, moe_b_router, moe_w_gu, moe_b_gu, moe_w_down, moe_b_down)
```

```python
import functools
import math

import jax
import jax.numpy as jnp
from jax import lax
from jax.experimental import pallas as pl
from jax.experimental.pallas import tpu as pltpu

NORM_EPS = 1e-6
N_MIXERS = 2
S5_GROUP = 16
S5_STATE = 64
DA_HEAD_DIM = 64
DA_VDIM = 2 * DA_HEAD_DIM
GRID_W = 64
ROPE_BASE = 10000.0
N_EXPERTS = 32
TOP_K = 4
SWIGLU_LIMIT = 7.0
SWIGLU_ALPHA = 1.702
LOG2_E = 1.4426950408889634

LANES = 128
SUBLANES = 8
VMEM_LIMIT_BYTES = 56 * 1024 * 1024

S5_SLAB_GROUPS = LANES // S5_GROUP
S5_SLAB_STATE = S5_SLAB_GROUPS * S5_STATE
S5_CHUNK = 64
S5_ROW_BLOCKS = 4
TM_TILE = 32
ROW_TILE = 256
MOE_TILE = 512
ROUTER_TILE = 512
EXPERT_ROWS = 512
COMBINE_TILE = 256
Q_TILE = 512
ATTN_ROW_BLOCK = 128
HEADS_PER_STEP = 2
MOD_ROWS = 24

F32 = jnp.float32
BF16 = jnp.bfloat16


def _cparams(sem, vmem=VMEM_LIMIT_BYTES):
    return pltpu.CompilerParams(dimension_semantics=sem, vmem_limit_bytes=vmem)


def _rms(x):
    return x * lax.rsqrt(jnp.mean(x * x, axis=-1, keepdims=True) + NORM_EPS)


def _ada_kernel(c_ref, w_ref, b_ref, o_ref):
    cv = c_ref[...]
    s = cv * jax.nn.sigmoid(cv)
    o_ref[...] = jnp.dot(s, w_ref[...], precision=lax.Precision.HIGHEST,
                         preferred_element_type=F32) + b_ref[...]


def _ada(cc, w_ada, b_ada):
    depth, d, d6 = w_ada.shape
    nj = d6 // d
    return pl.pallas_call(
        _ada_kernel,
        out_shape=jax.ShapeDtypeStruct((depth, MOD_ROWS, d6), F32),
        grid=(depth, nj),
        in_specs=[pl.BlockSpec((MOD_ROWS, d), lambda i, j: (0, 0)),
                  pl.BlockSpec((None, d, d), lambda i, j: (i, 0, j)),
                  pl.BlockSpec((None, 1, d), lambda i, j: (i, 0, j))],
        out_specs=pl.BlockSpec((None, MOD_ROWS, d), lambda i, j: (i, 0, j)),
        compiler_params=_cparams(("parallel", "parallel")),
        name="ada",
    )(cc, w_ada, b_ada.reshape(depth, 1, d6))


def _prenorm_tm_kernel(ctx_ref, x_ref, mod_ref, g_ref, o_ref, *, n_ctx_tiles, nb, d):
    i = pl.program_id(0)
    tt = o_ref.shape[1] // nb
    g = g_ref[...]

    def emit(src_ref, mod_row):
        for b in range(nb):
            r = mod_row(b)
            sh = mod_ref[r:r + 1, 0:d]
            sc = mod_ref[r:r + 1, d:2 * d]
            h = (_rms(src_ref[b]) * g) * (1.0 + sc) + sh
            for s in range(d // LANES):
                o_ref[s, pl.ds(b, tt, stride=nb), :] = h[:, s * LANES:(s + 1) * LANES]

    @pl.when(i < n_ctx_tiles)
    def _():
        emit(ctx_ref, lambda b: nb)

    @pl.when(i >= n_ctx_tiles)
    def _():
        emit(x_ref, lambda b: b)


def _prenorm_tm(ctx, x, mod, g):
    nb, n_ctx, d = ctx.shape
    n_lat = x.shape[1]
    tt = TM_TILE
    nct, nlt = n_ctx // tt, n_lat // tt
    kern = functools.partial(_prenorm_tm_kernel, n_ctx_tiles=nct, nb=nb, d=d)
    return pl.pallas_call(
        kern,
        out_shape=jax.ShapeDtypeStruct((d // LANES, (n_ctx + n_lat) * nb, LANES), F32),
        grid=(nct + nlt,),
        in_specs=[pl.BlockSpec((nb, tt, d), lambda i: (0, jnp.minimum(i, nct - 1), 0)),
                  pl.BlockSpec((nb, tt, d), lambda i: (0, jnp.maximum(i - nct, 0), 0)),
                  pl.BlockSpec((MOD_ROWS, 2 * d), lambda i: (0, 0)),
                  pl.BlockSpec((1, d), lambda i: (0, 0))],
        out_specs=pl.BlockSpec((d // LANES, tt * nb, LANES), lambda i: (0, i, 0)),
        compiler_params=_cparams(("parallel",)),
        name="prenorm_tm",
    )(ctx, x, mod, g)


def _s5_disc_kernel(ar_ref, ai_ref, ldt_ref, br_ref, bi_ref, lr_ref, li_ref, bbr_ref, bbi_ref):
    ar, ai = ar_ref[...], ai_ref[...]
    dt = jnp.exp(ldt_ref[...])
    mag = jnp.exp(ar * dt)
    lr = mag * jnp.cos(ai * dt)
    li = mag * jnp.sin(ai * dt)
    nr, ni = lr - 1.0, li
    den = ar * ar + ai * ai
    qr = (nr * ar + ni * ai) / den
    qi = (ni * ar - nr * ai) / den
    br, bi = br_ref[...], bi_ref[...]
    lr_ref[...] = lr
    li_ref[...] = li
    bbr_ref[...] = qr * br - qi * bi
    bbi_ref[...] = qr * bi + qi * br


def _s5_disc(a_re, a_im, log_dt, b_re, b_im):
    nd, g, p = a_re.shape
    cg = b_re.shape[-1]
    gp = g * p
    col = lambda a: a.reshape(nd, gp, 1)
    ldt = jnp.broadcast_to(log_dt[:, :, None], (nd, g, p))
    spec1 = pl.BlockSpec((None, gp, 1), lambda i: (i, 0, 0))
    specb = pl.BlockSpec((None, gp, cg), lambda i: (i, 0, 0))
    lr, li, bbr, bbi = pl.pallas_call(
        _s5_disc_kernel,
        out_shape=[jax.ShapeDtypeStruct((nd, gp, 1), F32)] * 2 + [jax.ShapeDtypeStruct((nd, gp, cg), F32)] * 2,
        grid=(nd,),
        in_specs=[spec1, spec1, spec1, specb, specb],
        out_specs=[spec1, spec1, specb, specb],
        compiler_params=_cparams(("parallel",)),
        name="s5_disc",
    )(col(a_re), col(a_im), col(ldt), b_re.reshape(nd, gp, cg), b_im.reshape(nd, gp, cg))
    return (lr.reshape(nd, g, p), li.reshape(nd, g, p),
            bbr.reshape(nd, g, p, cg), bbi.reshape(nd, g, p, cg))


def _s5_slab_params(lam_re, lam_im, bb_re, bb_im, c_re, c_im):
    nd, g, p = lam_re.shape
    cg = bb_re.shape[-1]
    ns, sg = g // S5_SLAB_GROUPS, S5_SLAB_GROUPS
    eye = jnp.eye(sg, dtype=F32)

    def b_mat(bb):
        bb = bb.reshape(nd, ns, sg, p, cg)
        return jnp.einsum('dsjpc,jk->dsjckp', bb, eye).reshape(nd, ns, sg * cg, sg * p)

    def c_mat(cm):
        cm = cm.reshape(nd, ns, sg, cg, p)
        return jnp.einsum('dsjcp,jk->dskpjc', cm, eye).reshape(nd, ns, sg * p, sg * cg)

    bm = jnp.concatenate([b_mat(bb_re), b_mat(bb_im)], axis=-1).astype(BF16)
    lam = lambda a: a.reshape(nd, ns, 1, sg * p)
    return bm, c_mat(c_re).astype(BF16), c_mat(c_im).astype(BF16), lam(lam_re), lam(lam_im)


def _s5_scan_kernel(u_ref, bm_ref, cr_ref, ci_ref, lr_ref, li_ref, y_ref, x0_ref, x1_ref, h0_ref, h1_ref, st_ref):
    d = pl.program_id(0)
    c = pl.program_id(2)
    t_len, nb = u_ref.shape[0], u_ref.shape[1] * u_ref.shape[2]
    ns = S5_SLAB_STATE
    xs, hs = (x0_ref, x1_ref), (h0_ref, h1_ref)

    @pl.when(c == 0)
    def _():
        for ref in xs + hs:
            ref[...] = jnp.zeros_like(ref)

    def stages(in_ref, bu_ref, h_ref, out_ref):
        lr = jnp.broadcast_to(lr_ref[...], (SUBLANES, ns))
        li = jnp.broadcast_to(li_ref[...], (SUBLANES, ns))
        fresh = c <= 1
        state = [(jnp.where(fresh, 0.0, st_ref[0, o]), jnp.where(fresh, 0.0, st_ref[1, o]))
                 for o in range(nb // SUBLANES)]
        tb = t_len // S5_ROW_BLOCKS
        for j in range(S5_ROW_BLOCKS):
            rsl = slice(j * tb * nb, (j + 1) * tb * nb)
            u = u_ref[j * tb:(j + 1) * tb].reshape(tb * nb, LANES).astype(BF16)
            in_ref[rsl, :] = jnp.dot(u, bm_ref[...], preferred_element_type=F32)

            for s in range(j * tb, (j + 1) * tb):
                t = jnp.where(d == 0, s, t_len - 1 - s)
                row = pl.multiple_of(t * nb, nb)
                for o in range(nb // SUBLANES):
                    hr, hi = state[o]
                    br = bu_ref[pl.ds(row + o * SUBLANES, SUBLANES), 0:ns]
                    bi = bu_ref[pl.ds(row + o * SUBLANES, SUBLANES), ns:2 * ns]
                    state[o] = (lr * hr - li * hi + br, lr * hi + li * hr + bi)
                h_ref[pl.ds(row, nb), 0:ns] = jnp.concatenate([st[0] for st in state], axis=0).astype(BF16)
                h_ref[pl.ds(row, nb), ns:2 * ns] = jnp.concatenate([st[1] for st in state], axis=0).astype(BF16)

            y = (jnp.dot(out_ref[rsl, 0:ns], cr_ref[...], preferred_element_type=F32)
                 - jnp.dot(out_ref[rsl, ns:2 * ns], ci_ref[...], preferred_element_type=F32))
            y_ref[j * tb:(j + 1) * tb] = y.reshape((tb,) + y_ref.shape[1:])
        for o in range(nb // SUBLANES):
            st_ref[0, o] = state[o][0]
            st_ref[1, o] = state[o][1]

    phase = lax.rem(c, 2)
    for r in range(2):
        @pl.when(phase == r)
        def _(r=r):
            stages(xs[r], xs[1 - r], hs[1 - r], hs[r])


def _s5_scan(u_t, bm, cr, ci, lam_r, lam_i, nb, n_ctx, n_tot):
    nslab = u_t.shape[0]
    nbo = nb // SUBLANES
    t_len = S5_CHUNK
    ncc, nc = n_ctx // t_len, n_tot // t_len
    u5 = u_t.reshape(nslab, n_tot, nbo, SUBLANES, LANES)

    def chunk(d, c):
        c = jnp.clip(c, 0, nc - 1)
        back = jnp.where(c < ncc, ncc - 1 - c, ncc + nc - 1 - c)
        return jnp.where(d == 0, c, back)

    pspec = lambda r, k: pl.BlockSpec((None, None, r, k), lambda d, s, c: (d, s, 0, 0))
    rows = t_len * nb
    return pl.pallas_call(
        _s5_scan_kernel,
        out_shape=jax.ShapeDtypeStruct((2, nslab, n_tot, nbo, SUBLANES, LANES), F32),
        grid=(2, nslab, nc + 2),
        in_specs=[pl.BlockSpec((None, t_len, nbo, SUBLANES, LANES), lambda d, s, c: (s, chunk(d, c), 0, 0, 0)),
                  pspec(LANES, 2 * S5_SLAB_STATE), pspec(S5_SLAB_STATE, LANES), pspec(S5_SLAB_STATE, LANES),
                  pspec(1, S5_SLAB_STATE), pspec(1, S5_SLAB_STATE)],
        out_specs=pl.BlockSpec((None, None, t_len, nbo, SUBLANES, LANES),
                               lambda d, s, c: (d, s, chunk(d, c - 2), 0, 0, 0)),
        scratch_shapes=[pltpu.VMEM((rows, 2 * S5_SLAB_STATE), F32)] * 2
                       + [pltpu.VMEM((rows, 2 * S5_SLAB_STATE), BF16)] * 2
                       + [pltpu.VMEM((2, nbo, SUBLANES, S5_SLAB_STATE), F32)],
        compiler_params=_cparams(("parallel", "parallel", "arbitrary")),
        name="s5_scan",
    )(u5, bm, cr, ci, lam_r, lam_i)


def _s5_out_kernel(y_ref, u_ref, dsk_ref, w_ref, bg_ref, ctx_ref, x_ref, mod_ref, o_ref, tr_ref,
                   *, n_ctx_tiles, nb, d):
    i = pl.program_id(0)
    nslab = d // LANES
    rows = u_ref.shape[1]
    tt = rows // nb
    y = (y_ref[0] + y_ref[1]).reshape(nslab, rows, LANES)
    v = jax.nn.gelu(y + dsk_ref[...] * u_ref[...])
    lhs = jnp.concatenate([v[s] for s in range(nslab)], axis=-1).astype(BF16)
    z = jnp.dot(lhs, w_ref[...], preferred_element_type=F32) + bg_ref[...]
    m = z[:, 0:d] * jax.nn.sigmoid(z[:, d:2 * d])
    for s in range(nslab):
        tr_ref[s] = m[:, s * LANES:(s + 1) * LANES]

    def emit(src_ref, mod_row):
        for b in range(nb):
            r = mod_row(b)
            gt = mod_ref[r:r + 1, :]
            mb = jnp.concatenate([tr_ref[s, pl.ds(b, tt, stride=nb), :] for s in range(nslab)], axis=-1)
            o_ref[b] = src_ref[b] + gt * mb

    @pl.when(i < n_ctx_tiles)
    def _():
        emit(ctx_ref, lambda b: nb)

    @pl.when(i >= n_ctx_tiles)
    def _():
        emit(x_ref, lambda b: b)


def _s5_out(y_t, u_t, d_skip, w_glu, b_glu, ctx, x, gt1):
    nb, n_ctx, d = ctx.shape
    n_lat = x.shape[1]
    n_tot = n_ctx + n_lat
    nslab, nbo = d // LANES, nb // SUBLANES
    tt = TM_TILE
    nct, nlt = n_ctx // tt, n_lat // tt
    kern = functools.partial(_s5_out_kernel, n_ctx_tiles=nct, nb=nb, d=d)
    return pl.pallas_call(
        kern,
        out_shape=jax.ShapeDtypeStruct((nb, n_tot, d), F32),
        grid=(nct + nlt,),
        in_specs=[pl.BlockSpec((2, nslab, tt, nbo, SUBLANES, LANES), lambda i: (0, 0, i, 0, 0, 0)),
                  pl.BlockSpec((nslab, tt * nb, LANES), lambda i: (0, i, 0)),
                  pl.BlockSpec((nslab, 1, LANES), lambda i: (0, 0, 0)),
                  pl.BlockSpec((d, 2 * d), lambda i: (0, 0)),
                  pl.BlockSpec((1, 2 * d), lambda i: (0, 0)),
                  pl.BlockSpec((nb, tt, d), lambda i: (0, jnp.minimum(i, nct - 1), 0)),
                  pl.BlockSpec((nb, tt, d), lambda i: (0, jnp.maximum(i - nct, 0), 0)),
                  pl.BlockSpec((MOD_ROWS, d), lambda i: (0, 0))],
        out_specs=pl.BlockSpec((nb, tt, d), lambda i: (0, i, 0)),
        scratch_shapes=[pltpu.VMEM((nslab, tt * nb, LANES), F32)],
        compiler_params=_cparams(("parallel",)),
        name="s5_out",
    )(y_t, u_t, d_skip.reshape(nslab, 1, LANES), w_glu.astype(BF16), b_glu.reshape(1, 2 * d), ctx, x, gt1)


def _s5_layer(xs_ctx, xs_lat, mod, g_mix, a_re, a_im, log_dt, b_re, b_im, c_re, c_im, d_skip, w_glu, b_glu):
    nb, n_ctx, d = xs_ctx.shape
    n_tot = n_ctx + xs_lat.shape[1]
    u_t = _prenorm_tm(xs_ctx, xs_lat, mod[:, 0:2 * d], g_mix.reshape(1, d))
    lam_re, lam_im, bb_re, bb_im = _s5_disc(a_re, a_im, log_dt, b_re, b_im)
    bm, cr, ci, lam_r, lam_i = _s5_slab_params(lam_re, lam_im, bb_re, bb_im, c_re, c_im)
    y_t = _s5_scan(u_t, bm, cr, ci, lam_r, lam_i, nb, n_ctx, n_tot)
    return _s5_out(y_t, u_t, d_skip, w_glu, b_glu, xs_ctx, xs_lat, mod[:, 2 * d:3 * d])


NEG_PAD = -1e30
NEG_MASK = -3e38


def _split_bf16(a):
    hi = a.astype(BF16)
    return hi, (a - hi.astype(F32)).astype(BF16)


def _mod_row(piece, pieces_per_batch, n_ctx_pieces, nb):
    b = piece // pieces_per_batch
    return jnp.where(piece - b * pieces_per_batch < n_ctx_pieces, nb, b)


def _router_kernel(x_ref, mod_ref, g_ref, w_ref, bias_ref, h_ref, sel_ref, selt_ref, meta_ref, cnt_ref,
                   carry_ref, *, pieces_per_batch, n_ctx_pieces, nb, d):
    i = pl.program_id(0)
    rows = x_ref.shape[0]

    @pl.when(i == 0)
    def _():
        carry_ref[...] = jnp.zeros_like(carry_ref)

    parts = []
    for p in range(rows // ROW_TILE):
        row = _mod_row(i * (rows // ROW_TILE) + p, pieces_per_batch, n_ctx_pieces, nb)
        mrow = mod_ref[pl.ds(row, 1), :]
        xp = x_ref[p * ROW_TILE:(p + 1) * ROW_TILE, :]
        parts.append((_rms(xp) * g_ref[...]) * (1.0 + mrow[:, d:2 * d]) + mrow[:, 0:d])
    h = jnp.concatenate(parts, axis=0)
    h_ref[...] = h.astype(BF16)

    hi, lo = _split_bf16(h)
    whi, wlo = _split_bf16(w_ref[...])
    dot = functools.partial(jnp.dot, preferred_element_type=F32)
    logits = dot(hi, whi) + dot(lo, whi) + dot(hi, wlo) + bias_ref[...]

    lane = lax.broadcasted_iota(jnp.int32, (rows, LANES), 1)
    lane_f = lane.astype(F32)
    work = logits
    tops, idxs, hots = [], [], []
    for _ in range(TOP_K):
        m = jnp.max(work, axis=-1, keepdims=True)
        idx = jnp.min(jnp.where(work == m, lane_f, float(LANES)), axis=-1, keepdims=True)
        hot = lane_f == idx
        work = jnp.where(hot, NEG_MASK, work)
        tops.append(m)
        idxs.append(idx)
        hots.append(hot)
    exps = [jnp.exp(m - tops[0]) for m in tops]
    denom = exps[0] + exps[1] + exps[2] + exps[3]

    multi = jnp.zeros((rows, LANES), F32)
    for hot in hots:
        multi = multi + jnp.where(hot, 1.0, 0.0)
    r_i = lax.broadcasted_iota(jnp.int32, (rows, rows), 0)
    c_i = lax.broadcasted_iota(jnp.int32, (rows, rows), 1)
    tri = jnp.where(c_i < r_i, 1.0, 0.0).astype(BF16)
    in_tile = dot(tri, multi.astype(BF16))
    cnt_tile = jnp.sum(multi, axis=0, keepdims=True)
    e_r = lax.broadcasted_iota(jnp.int32, (LANES, LANES), 0)
    e_c = lax.broadcasted_iota(jnp.int32, (LANES, LANES), 1)
    below = jnp.where(e_r < e_c, 1.0, 0.0).astype(BF16)
    c_hi, c_lo = _split_bf16(jnp.broadcast_to(cnt_tile, (SUBLANES, LANES)))
    prefix = (dot(c_hi, below) + dot(c_lo, below))[0:1, :]
    slot = in_tile + prefix

    sel = jnp.zeros((rows, LANES), F32)
    for k in range(TOP_K):
        pos = jnp.sum(jnp.where(hots[k], slot, 0.0), axis=-1, keepdims=True)
        sel = jnp.where(lane == k, idxs[k].astype(F32), sel)
        sel = jnp.where(lane == TOP_K + k, exps[k] / denom, sel)
        sel = jnp.where(lane == 2 * TOP_K + k, pos, sel)
    sel_ref[...] = sel
    selt_ref[...] = sel.T[2 * TOP_K:2 * TOP_K + SUBLANES, :].astype(jnp.int32)

    carry = carry_ref[0:1, :]
    sub = lax.broadcasted_iota(jnp.int32, (SUBLANES, LANES), 0)
    meta = jnp.where(sub == 0, cnt_tile, jnp.where(sub == 1, carry, jnp.where(sub == 2, prefix, 0.0)))
    meta_ref[...] = meta.astype(jnp.int32)
    carry_ref[0:1, :] = carry + cnt_tile
    cnt_ref[...] = jnp.broadcast_to(carry + cnt_tile, cnt_ref.shape).astype(jnp.int32)


def _router(xs, n_ctx, mod_shsc, g, w_router, b_router):
    nb, t_len, d = xs.shape
    r = MOE_TILE
    n_tok = nb * t_len
    nt = n_tok // r
    ne = w_router.shape[1]
    w_pad = jnp.zeros((d, LANES), F32).at[:, :ne].set(w_router)
    b_pad = jnp.full((1, LANES), NEG_PAD, F32).at[0, :ne].set(b_router)
    kern = functools.partial(_router_kernel, pieces_per_batch=t_len // ROW_TILE,
                             n_ctx_pieces=n_ctx // ROW_TILE, nb=nb, d=d)
    tile = lambda i: (i, 0)
    const = lambda i: (0, 0)
    return pl.pallas_call(
        kern,
        out_shape=[jax.ShapeDtypeStruct((n_tok, d), BF16),
                   jax.ShapeDtypeStruct((n_tok, LANES), F32),
                   jax.ShapeDtypeStruct((nt * SUBLANES, r), jnp.int32),
                   jax.ShapeDtypeStruct((nt, SUBLANES, LANES), jnp.int32),
                   jax.ShapeDtypeStruct((SUBLANES, LANES), jnp.int32)],
        grid=(nt,),
        in_specs=[pl.BlockSpec((r, d), tile),
                  pl.BlockSpec((MOD_ROWS, 2 * d), const),
                  pl.BlockSpec((1, d), const),
                  pl.BlockSpec((d, LANES), const),
                  pl.BlockSpec((1, LANES), const)],
        out_specs=[pl.BlockSpec((r, d), tile),
                   pl.BlockSpec((r, LANES), tile),
                   pl.BlockSpec((SUBLANES, r), tile),
                   pl.BlockSpec((None, SUBLANES, LANES), lambda i: (i, 0, 0)),
                   pl.BlockSpec((SUBLANES, LANES), const)],
        scratch_shapes=[pltpu.VMEM((SUBLANES, LANES), F32)],
        compiler_params=_cparams(("arbitrary",)),
        name="router",
    )(xs.reshape(n_tok, d), mod_shsc, g.reshape(1, d), w_pad, b_pad)


def _plan_kernel(cnt_ref, ps_ref, be_ref, first_ref, nused_ref, *, ne, rb):
    n_blocks = be_ref.shape[0]

    def fill(i, carry):
        be_ref[i] = ne - 1
        first_ref[i] = 0
        return carry
    lax.fori_loop(0, n_blocks, fill, 0)

    def per_expert(e, acc):
        nblk = (cnt_ref[0, e] + (rb - 1)) // rb
        ps_ref[e] = acc
        b0 = acc // rb

        def mark(i, carry):
            be_ref[b0 + i] = e
            first_ref[b0 + i] = jnp.where(i == 0, 1, 0)
            return carry
        lax.fori_loop(0, nblk, mark, 0)
        return acc + nblk * rb

    total = lax.fori_loop(0, ne, per_expert, 0)

    def tail(e, carry):
        ps_ref[e] = total
        return carry
    lax.fori_loop(ne, ps_ref.shape[0], tail, 0)
    nused_ref[0] = total // rb


def _plan(cnt, ne, n_blocks):
    smem = pl.BlockSpec(memory_space=pltpu.SMEM)
    return pl.pallas_call(
        functools.partial(_plan_kernel, ne=ne, rb=EXPERT_ROWS),
        out_shape=[jax.ShapeDtypeStruct((LANES,), jnp.int32),
                   jax.ShapeDtypeStruct((n_blocks,), jnp.int32),
                   jax.ShapeDtypeStruct((n_blocks,), jnp.int32),
                   jax.ShapeDtypeStruct((1,), jnp.int32)],
        in_specs=[smem],
        out_specs=[smem, smem, smem, smem],
        name="plan",
    )(cnt)


_SEG_SIZES = tuple(MOE_TILE >> s for s in range(MOE_TILE.bit_length()))
TILE_SLOTS = MOE_TILE * TOP_K
TOKEN_ROWS = SUBLANES


def _segment_copies(n, src, dst, make_copy, wait=False):
    for k, p in enumerate(_SEG_SIZES):
        take = (n & p) != 0

        @pl.when(take)
        def _():
            cp = make_copy(pl.multiple_of(src * TOKEN_ROWS, TOKEN_ROWS),
                           pl.multiple_of(dst * TOKEN_ROWS, TOKEN_ROWS), p * TOKEN_ROWS)
            cp.wait() if wait else cp.start(priority=k % 2)
        step = jnp.where(take, p, 0)
        src = src + step
        dst = dst + step


def _tile_major(x):
    return [x[:, s * LANES:(s + 1) * LANES] for s in range(x.shape[1] // LANES)]


def _dispatch_kernel(meta_ref, metap_ref, ps_ref, cnt_ref, h_ref, selt_ref, xbuf_ref, xs_ref, zero_ref, sem,
                     zsem, *, ne, rb):
    rows = h_ref.shape[0]
    slots = xs_ref.shape[1] // TOKEN_ROWS

    @pl.when(pl.program_id(0) == 0)
    def _():
        zero_ref[...] = jnp.zeros_like(zero_ref)

        def pad_copy(_, dst, p):
            return pltpu.make_async_copy(zero_ref.at[pl.ds(0, p), :], xbuf_ref.at[pl.ds(dst, p), :], zsem)

        for wait in (False, True):
            def pad_rows(e, carry, wait=wait):
                cnt = cnt_ref[0, e]
                _segment_copies((-cnt) & (rb - 1), 0, ps_ref[e] + cnt, pad_copy, wait)
                return carry
            lax.fori_loop(0, ne, pad_rows, 0)

    pos = selt_ref[0:TOP_K, :]
    s_iota = lax.broadcasted_iota(jnp.int32, (slots, rows), 0)
    perm = jnp.zeros((slots, rows), F32)
    for k in range(TOP_K):
        perm = perm + jnp.where(s_iota == pos[k:k + 1, :], 1.0, 0.0)
    step = pl.program_id(0)
    slot = lax.rem(step, 2)
    sorted_rows = jnp.dot(perm.astype(BF16), h_ref[...], preferred_element_type=F32)
    for s, slab in enumerate(_tile_major(sorted_rows)):
        xs_ref[slot, pl.ds(s, slots, stride=TOKEN_ROWS), :] = slab

    def segments(mref, which, wait):
        def seg_copy(src, dst, p):
            return pltpu.make_async_copy(xs_ref.at[which, pl.ds(src, p), :], xbuf_ref.at[pl.ds(dst, p), :],
                                         sem.at[which])

        def per_expert(e, carry):
            _segment_copies(mref[0, e], mref[2, e], ps_ref[e] + mref[1, e], seg_copy, wait)
            return carry
        lax.fori_loop(0, ne, per_expert, 0)

    segments(meta_ref, slot, False)

    @pl.when(step > 0)
    def _():
        segments(metap_ref, 1 - slot, True)

    @pl.when(step == pl.num_programs(0) - 1)
    def _():
        segments(meta_ref, slot, True)


def _dispatch(h2, selt, meta, ps, cnt, ne, n_buf):
    n_tok, d = h2.shape
    r = MOE_TILE
    smem = pl.BlockSpec(memory_space=pltpu.SMEM)
    return pl.pallas_call(
        functools.partial(_dispatch_kernel, ne=ne, rb=EXPERT_ROWS),
        out_shape=jax.ShapeDtypeStruct((n_buf * TOKEN_ROWS, LANES), F32),
        grid=(n_tok // r,),
        in_specs=[pl.BlockSpec((None, SUBLANES, LANES), lambda i: (i, 0, 0), memory_space=pltpu.SMEM),
                  pl.BlockSpec((None, SUBLANES, LANES), lambda i: (jnp.maximum(i - 1, 0), 0, 0),
                               memory_space=pltpu.SMEM),
                  smem, smem,
                  pl.BlockSpec((r, d), lambda i: (i, 0)),
                  pl.BlockSpec((SUBLANES, r), lambda i: (i, 0))],
        out_specs=pl.BlockSpec(memory_space=pl.ANY),
        scratch_shapes=[pltpu.VMEM((2, TILE_SLOTS * TOKEN_ROWS, LANES), F32),
                        pltpu.VMEM((_SEG_SIZES[0] * TOKEN_ROWS, LANES), F32),
                        pltpu.SemaphoreType.DMA((2,)), pltpu.SemaphoreType.DMA(())],
        compiler_params=_cparams(("arbitrary",)),
        name="dispatch",
    )(meta, meta, ps, cnt, h2, selt)


def _expert_kernel(be_ref, first_ref, nused_ref, x_ref, wgu_ref, bgu_ref, wd_ref, bd_ref, y_ref,
                   wgu_bf, wd_bf):
    i = pl.program_id(0)
    dff = wd_ref.shape[0]

    @pl.when(i < nused_ref[0])
    def _():
        @pl.when(first_ref[i] == 1)
        def _():
            wgu_bf[...] = wgu_ref[...].astype(BF16)
            wd_bf[...] = wd_ref[...].astype(BF16)

        rows = x_ref.shape[0] // TOKEN_ROWS
        x = jnp.concatenate([x_ref[pl.ds(s, rows, stride=TOKEN_ROWS), :] for s in range(TOKEN_ROWS)], axis=-1)
        gu = jnp.dot(x.astype(BF16), wgu_bf[...], preferred_element_type=F32) + bgu_ref[...]
        gate = jnp.minimum(gu[:, 0:dff], SWIGLU_LIMIT)
        up = jnp.clip(gu[:, dff:2 * dff], -SWIGLU_LIMIT, SWIGLU_LIMIT)
        act = (up + 1.0) * gate * jax.nn.sigmoid(SWIGLU_ALPHA * gate)
        y = jnp.dot(act.astype(BF16), wd_bf[...], preferred_element_type=F32) + bd_ref[...]
        for s, slab in enumerate(_tile_major(y)):
            y_ref[pl.ds(s, rows, stride=TOKEN_ROWS), :] = slab

    @pl.when(i >= nused_ref[0])
    def _():
        y_ref[...] = jnp.zeros_like(y_ref)


def _experts(xbuf, block_e, first, n_used, layer, w_gu, b_gu, w_down, b_down):
    depth, ne, d, dff2 = w_gu.shape
    assert d == TOKEN_ROWS * LANES
    dff = dff2 // 2
    rb = EXPERT_ROWS * TOKEN_ROWS
    blk = lambda i, nu: jnp.minimum(i, nu[0] - 1)
    wmap = lambda i, be, fi, nu: (layer, be[blk(i, nu)], 0, 0)
    grid_spec = pltpu.PrefetchScalarGridSpec(
        num_scalar_prefetch=3,
        grid=(xbuf.shape[0] // rb,),
        in_specs=[pl.BlockSpec((rb, LANES), lambda i, be, fi, nu: (blk(i, nu), 0)),
                  pl.BlockSpec((None, None, d, dff2), wmap),
                  pl.BlockSpec((None, None, 1, dff2), wmap),
                  pl.BlockSpec((None, None, dff, d), wmap),
                  pl.BlockSpec((None, None, 1, d), wmap)],
        out_specs=pl.BlockSpec((rb, LANES), lambda i, be, fi, nu: (i, 0)),
        scratch_shapes=[pltpu.VMEM((d, dff2), BF16), pltpu.VMEM((dff, d), BF16)])
    return pl.pallas_call(
        _expert_kernel,
        out_shape=jax.ShapeDtypeStruct(xbuf.shape, F32),
        grid_spec=grid_spec,
        compiler_params=_cparams(("arbitrary",)),
        name="experts",
    )(block_e, first, n_used, xbuf, w_gu, b_gu.reshape(depth, ne, 1, dff2), w_down,
      b_down.reshape(depth, ne, 1, d))


def _combine_kernel(meta_ref, metan_ref, ps_ref, sel_ref, x_ref, mod_ref, ybuf_ref, o_ref, ys_ref, sem,
                    *, pieces_per_batch, n_ctx_pieces, nb, ne):
    step = pl.program_id(0)
    n_steps = pl.num_programs(0)
    slot = lax.rem(step, 2)
    rows = x_ref.shape[0]
    slots = ys_ref.shape[1] // TOKEN_ROWS

    def segments(mref, to_slot, wait):
        def seg_copy(src, dst, p):
            return pltpu.make_async_copy(ybuf_ref.at[pl.ds(dst, p), :], ys_ref.at[to_slot, pl.ds(src, p), :],
                                         sem.at[to_slot])

        def per_expert(e, carry):
            _segment_copies(mref[0, e], mref[2, e], ps_ref[e] + mref[1, e], seg_copy, wait)
            return carry
        lax.fori_loop(0, ne, per_expert, 0)

    @pl.when(step == 0)
    def _():
        segments(meta_ref, 0, False)

    @pl.when(step + 1 < n_steps)
    def _():
        segments(metan_ref, 1 - slot, False)

    segments(meta_ref, slot, True)

    sel = sel_ref[...]
    l_iota = lax.broadcasted_iota(jnp.int32, (rows, slots), 1).astype(F32)
    w = jnp.zeros((rows, slots), F32)
    for k in range(TOP_K):
        w = w + jnp.where(l_iota == sel[:, 2 * TOP_K + k:2 * TOP_K + k + 1], sel[:, TOP_K + k:TOP_K + k + 1], 0.0)
    ys = jnp.concatenate([ys_ref[slot, pl.ds(s, slots, stride=TOKEN_ROWS), :] for s in range(TOKEN_ROWS)],
                         axis=-1).astype(BF16)
    acc = jnp.dot(w.astype(BF16), ys, preferred_element_type=F32)
    for p in range(rows // ROW_TILE):
        row = _mod_row(step * (rows // ROW_TILE) + p, pieces_per_batch, n_ctx_pieces, nb)
        psl = slice(p * ROW_TILE, (p + 1) * ROW_TILE)
        o_ref[psl, :] = x_ref[psl, :] + mod_ref[pl.ds(row, 1), :] * acc[psl, :]


def _combine(ybuf, meta, ps, sel, xs, n_ctx, mod_gt, ne):
    nb, t_len, d = xs.shape
    r = MOE_TILE
    n_tok = nb * t_len
    nt = n_tok // r
    kern = functools.partial(_combine_kernel, pieces_per_batch=t_len // ROW_TILE,
                             n_ctx_pieces=n_ctx // ROW_TILE, nb=nb, ne=ne)
    mspec = lambda f: pl.BlockSpec((None, SUBLANES, LANES), f, memory_space=pltpu.SMEM)
    out = pl.pallas_call(
        kern,
        out_shape=jax.ShapeDtypeStruct((n_tok, d), F32),
        grid=(nt,),
        in_specs=[mspec(lambda i: (i, 0, 0)),
                  mspec(lambda i: (jnp.minimum(i + 1, nt - 1), 0, 0)),
                  pl.BlockSpec(memory_space=pltpu.SMEM),
                  pl.BlockSpec((r, LANES), lambda i: (i, 0)),
                  pl.BlockSpec((r, d), lambda i: (i, 0)),
                  pl.BlockSpec((MOD_ROWS, d), lambda i: (0, 0)),
                  pl.BlockSpec(memory_space=pl.ANY)],
        out_specs=pl.BlockSpec((r, d), lambda i: (i, 0)),
        scratch_shapes=[pltpu.VMEM((2, TILE_SLOTS * TOKEN_ROWS, LANES), F32), pltpu.SemaphoreType.DMA((2,))],
        compiler_params=_cparams(("arbitrary",)),
        name="combine",
    )(meta, meta, ps, sel, xs.reshape(n_tok, d), mod_gt, ybuf)
    return out.reshape(nb, t_len, d)


def _moe_layer(xs, n_ctx, mod_shsc, mod_gt, g, layer, w_router, b_router, w_gu, b_gu, w_down, b_down):
    nb, t_len, d = xs.shape
    ne = w_router.shape[-1]
    n_tiles = nb * t_len // MOE_TILE
    n_blocks = n_tiles * TILE_SLOTS // EXPERT_ROWS + ne
    h2, sel, selt, meta, cnt = _router(xs, n_ctx, mod_shsc, g, w_router[layer], b_router[layer])
    ps, block_e, first, n_used = _plan(cnt, ne, n_blocks)
    xbuf = _dispatch(h2, selt, meta, ps, cnt, ne, n_blocks * EXPERT_ROWS)
    ybuf = _experts(xbuf, block_e, first, n_used, layer, w_gu, b_gu, w_down, b_down)
    return _combine(ybuf, meta, ps, sel, xs, n_ctx, mod_gt, ne)


def _qkv_kernel(x_ref, mod_ref, g_ref, w_ref, q_ref, k_ref, v_ref, *, n_ctx_tiles, nb, d):
    b = pl.program_id(0)
    j = pl.program_id(1)
    row = jnp.where(j < n_ctx_tiles, nb, b)
    mrow = mod_ref[pl.ds(row, 1), :]
    h = (_rms(x_ref[...]) * g_ref[...]) * (1.0 + mrow[:, d:2 * d]) + mrow[:, 0:d]
    qkv = jnp.dot(h.astype(BF16), w_ref[...], preferred_element_type=F32)
    for hd in range(d // DA_VDIM):
        lo = hd * DA_VDIM
        q_ref[hd] = qkv[:, lo:lo + DA_VDIM].astype(BF16)
        k_ref[hd] = qkv[:, d + lo:d + lo + DA_VDIM].astype(BF16)
        v_ref[hd] = qkv[:, 2 * d + lo:2 * d + lo + DA_VDIM].astype(BF16)


def _qkv(xs, n_ctx, mod_shsc, g, w_qkv):
    nb, t_len, d = xs.shape
    r = ROW_TILE
    nh = d // DA_VDIM
    kern = functools.partial(_qkv_kernel, n_ctx_tiles=n_ctx // r, nb=nb, d=d)
    nct = n_ctx // r
    hspec = pl.BlockSpec((None, nh, r, DA_VDIM), lambda b, j: (b, 0, j, 0))
    qspec = pl.BlockSpec((None, nh, r, DA_VDIM), lambda b, j: (b, 0, jnp.maximum(j - nct, 0), 0))
    kv_shape = jax.ShapeDtypeStruct((nb, nh, t_len, DA_VDIM), BF16)
    return pl.pallas_call(
        kern,
        out_shape=[jax.ShapeDtypeStruct((nb, nh, t_len - n_ctx, DA_VDIM), BF16), kv_shape, kv_shape],
        grid=(nb, t_len // r),
        in_specs=[pl.BlockSpec((None, r, d), lambda b, j: (b, j, 0)),
                  pl.BlockSpec((MOD_ROWS, 2 * d), lambda b, j: (0, 0)),
                  pl.BlockSpec((1, d), lambda b, j: (0, 0)),
                  pl.BlockSpec((d, 3 * d), lambda b, j: (0, 0))],
        out_specs=[qspec, hspec, hspec],
        compiler_params=_cparams(("parallel", "arbitrary")),
        name="qkv",
    )(xs, mod_shsc, g.reshape(1, d), w_qkv.astype(BF16))


def _attn_kernel(q_ref, k_ref, v_ref, qg_ref, kg_ref, cosq_ref, sinq_ref, cosk_ref, sink_ref, lam_ref,
                 sg_ref, o_ref, kn_ref, *, n_ctx, lambda_init):
    qi = pl.program_id(2)
    n_heads, n_tot = k_ref.shape[0], k_ref.shape[1]
    tq = q_ref.shape[1]
    dh = DA_HEAD_DIM
    lane = lax.broadcasted_iota(jnp.int32, (1, DA_VDIM), 1)
    comp0 = lane < dh
    low_half = lax.rem(lane, dh) < dh // 2

    def head_norm(xv, gain):
        x2 = xv * xv
        s0 = jnp.sum(jnp.where(comp0, x2, 0.0), axis=-1, keepdims=True)
        s1 = jnp.sum(jnp.where(comp0, 0.0, x2), axis=-1, keepdims=True)
        ms = jnp.where(comp0, s0, s1) * (1.0 / dh)
        return xv * lax.rsqrt(ms + NORM_EPS) * gain

    def rope(xv, cos, sin_signed):
        partner = jnp.where(low_half, pltpu.roll(xv, DA_VDIM - dh // 2, axis=1),
                            pltpu.roll(xv, dh // 2, axis=1))
        return xv * cos + partner * sin_signed

    @pl.when(qi == 0)
    def _():
        kg = kg_ref[...]
        for g in range(n_heads):
            kn_ref[g, 0:n_ctx, :] = head_norm(k_ref[g, 0:n_ctx, :].astype(F32), kg).astype(BF16)
            for c0 in range(n_ctx, n_tot, tq):
                kk = head_norm(k_ref[g, c0:c0 + tq, :].astype(F32), kg)
                kn_ref[g, c0:c0 + tq, :] = rope(kk, cosk_ref[c0 - n_ctx:c0 - n_ctx + tq, :],
                                                sink_ref[c0 - n_ctx:c0 - n_ctx + tq, :]).astype(BF16)

    lam_v = lam_ref[...]
    lam = (jnp.exp(jnp.sum(lam_v[0:1] * lam_v[1:2], axis=-1, keepdims=True))
           - jnp.exp(jnp.sum(lam_v[2:3] * lam_v[3:4], axis=-1, keepdims=True)) + lambda_init)
    dims = (((1,), (1,)), ((), ()))
    rb = ATTN_ROW_BLOCK
    units = [(g, r * rb) for g in range(n_heads) for r in range(tq // rb)]

    def scores(g, r0):
        qn = rope(head_norm(q_ref[g, r0:r0 + rb, :].astype(F32), qg_ref[...]),
                  cosq_ref[r0:r0 + rb, :], sinq_ref[r0:r0 + rb, :]) * (dh ** -0.5 * LOG2_E)
        qq = jnp.concatenate([jnp.where(comp0, qn, 0.0), jnp.where(comp0, 0.0, qn)], axis=0).astype(BF16)
        return lax.dot_general(qq, kn_ref[g], dims, preferred_element_type=F32)

    def softmax_mix(s):
        s0, s1 = s[0:rb], s[rb:2 * rb]
        e0 = jnp.exp2(s0 - jnp.max(s0, axis=-1, keepdims=True))
        e1 = jnp.exp2(s1 - jnp.max(s1, axis=-1, keepdims=True))
        l0 = jnp.sum(e0, axis=-1, keepdims=True)
        l1 = jnp.sum(e1, axis=-1, keepdims=True)
        return e0.astype(BF16) - (lam * l0 / l1).astype(BF16) * e1.astype(BF16), l0

    def values(g, r0, a, l0):
        o = jnp.dot(a, v_ref[g], preferred_element_type=F32) * (1.0 / l0)
        o_ref[r0:r0 + rb, g * DA_VDIM:(g + 1) * DA_VDIM] = (
            _rms(o) * sg_ref[...] * (1.0 - lambda_init)).astype(BF16)

    s_of, a_of = {}, {}
    for i in range(len(units) + 2):
        if i < len(units):
            s_of[i] = scores(*units[i])
        if 0 <= i - 1 < len(units):
            a_of[i - 1] = softmax_mix(s_of.pop(i - 1))
        if 0 <= i - 2 < len(units):
            values(*units[i - 2], *a_of.pop(i - 2))


def _rope_tables(n_tokens):
    rows = n_tokens // GRID_W
    row = jnp.repeat(jnp.arange(rows, dtype=F32), GRID_W)
    col = jnp.tile(jnp.arange(GRID_W, dtype=F32), rows)
    n_freq = DA_HEAD_DIM // 4
    inv_freq = jnp.exp(-math.log(ROPE_BASE) * jnp.arange(n_freq, dtype=F32) / n_freq)
    ang = jnp.concatenate([row[:, None] * inv_freq, col[:, None] * inv_freq], axis=-1)
    cos, sin = jnp.cos(ang), jnp.sin(ang)
    return jnp.tile(cos, (1, 4)), jnp.tile(jnp.concatenate([-sin, sin], axis=-1), (1, 2))


def _attention(q, k, v, n_ctx, q_gain, k_gain, lam_vecs, sub_gain, lambda_init):
    nb, nh, n_tot, dv = k.shape
    n_lat = n_tot - n_ctx
    tq = Q_TILE
    cos_t, sin_t = _rope_tables(n_lat)
    gain2 = lambda gv: jnp.tile(gv, 2).reshape(1, dv)
    lam_pad = jnp.zeros((SUBLANES, dv), F32).at[0:4, 0:DA_HEAD_DIM].set(lam_vecs)
    kern = functools.partial(_attn_kernel, n_ctx=n_ctx, lambda_init=lambda_init)
    const = lambda shape: pl.BlockSpec(shape, lambda b, h, i: (0, 0))
    hg = HEADS_PER_STEP
    return pl.pallas_call(
        kern,
        out_shape=jax.ShapeDtypeStruct((nb, n_lat, nh * dv), BF16),
        grid=(nb, nh // hg, n_lat // tq),
        in_specs=[pl.BlockSpec((None, hg, tq, dv), lambda b, h, i: (b, h, i, 0)),
                  pl.BlockSpec((None, hg, n_tot, dv), lambda b, h, i: (b, h, 0, 0)),
                  pl.BlockSpec((None, hg, n_tot, dv), lambda b, h, i: (b, h, 0, 0)),
                  const((1, dv)), const((1, dv)),
                  pl.BlockSpec((tq, dv), lambda b, h, i: (i, 0)),
                  pl.BlockSpec((tq, dv), lambda b, h, i: (i, 0)),
                  const((n_lat, dv)), const((n_lat, dv)),
                  const((SUBLANES, dv)), const((1, dv))],
        out_specs=pl.BlockSpec((None, tq, hg * dv), lambda b, h, i: (b, i, h)),
        scratch_shapes=[pltpu.VMEM((hg, n_tot, dv), BF16)],
        compiler_params=_cparams(("parallel", "parallel", "arbitrary")),
        name="attention",
    )(q, k, v, gain2(q_gain), gain2(k_gain), cos_t, sin_t, cos_t, sin_t, lam_pad, sub_gain.reshape(1, dv))


def _oproj_kernel(o_ref, w_ref, x_ref, mod_ref, out_ref):
    b = pl.program_id(0)
    y = jnp.dot(o_ref[...], w_ref[...], preferred_element_type=F32)
    out_ref[...] = x_ref[...] + mod_ref[pl.ds(b, 1), :] * y


def _oproj(o, w_o, xs, n_ctx, mod_gt):
    nb, n_lat, d = o.shape
    r = ROW_TILE
    off = n_ctx // r
    return pl.pallas_call(
        _oproj_kernel,
        out_shape=jax.ShapeDtypeStruct((nb, n_lat, d), F32),
        grid=(nb, n_lat // r),
        in_specs=[pl.BlockSpec((None, r, d), lambda b, j: (b, j, 0)),
                  pl.BlockSpec((d, d), lambda b, j: (0, 0)),
                  pl.BlockSpec((None, r, d), lambda b, j: (b, j + off, 0)),
                  pl.BlockSpec((MOD_ROWS, d), lambda b, j: (0, 0))],
        out_specs=pl.BlockSpec((None, r, d), lambda b, j: (b, j, 0)),
        compiler_params=_cparams(("parallel", "parallel")),
        name="oproj",
    )(o, w_o.astype(BF16), xs, mod_gt)


def _attn_layer(xs, n_ctx, mod, g_mix, w_qkv, w_o, q_gain, k_gain, lq1, lk1, lq2, lk2, sub_gain, lambda_init):
    d = xs.shape[-1]
    q, k, v = _qkv(xs, n_ctx, mod[:, 0:2 * d], g_mix, w_qkv)
    o = _attention(q, k, v, n_ctx, q_gain, k_gain, jnp.stack([lq1, lk1, lq2, lk2]), sub_gain, lambda_init)
    return _oproj(o, w_o, xs, n_ctx, mod[:, 2 * d:3 * d])


def kernel(x, c, ctx, c_ctx, w_ada, b_ada, g_mix, g_ffn, s5_a_re, s5_a_im, s5_log_dt, s5_b_re, s5_b_im, s5_c_re, s5_c_im, s5_d, s5_w_glu, s5_b_glu, da_w_qkv, da_w_o, da_q_gain, da_k_gain, da_lam_q1, da_lam_k1, da_lam_q2, da_lam_k2, da_sub_gain, moe_w_router, moe_b_router, moe_w_gu, moe_b_gu, moe_w_down, moe_b_down):
    nb, n_lat, d = x.shape
    n_ctx = ctx.shape[1]
    depth = w_ada.shape[0]
    assert depth == 2 and nb % SUBLANES == 0 and nb + 1 <= MOD_ROWS
    cc = jnp.concatenate([c, c_ctx[None], jnp.zeros((MOD_ROWS - nb - 1, d), F32)], axis=0)
    mod = _ada(cc, w_ada, b_ada)
    xs = _s5_layer(ctx, x, mod[0], g_mix[0], s5_a_re[0], s5_a_im[0], s5_log_dt[0], s5_b_re[0], s5_b_im[0],
                   s5_c_re[0], s5_c_im[0], s5_d[0], s5_w_glu[0], s5_b_glu[0])
    xs = _moe_layer(xs, n_ctx, mod[0][:, 3 * d:5 * d], mod[0][:, 5 * d:6 * d], g_ffn[0], 0,
                    moe_w_router, moe_b_router, moe_w_gu, moe_b_gu, moe_w_down, moe_b_down)
    lambda_init = 0.8 - 0.6 * math.exp(-0.3 * 1)
    x1 = _attn_layer(xs, n_ctx, mod[1], g_mix[1], da_w_qkv[0], da_w_o[0], da_q_gain[0], da_k_gain[0],
                     da_lam_q1[0], da_lam_k1[0], da_lam_q2[0], da_lam_k2[0], da_sub_gain[0], lambda_init)
    return _moe_layer(x1, 0, mod[1][:, 3 * d:5 * d], mod[1][:, 5 * d:6 * d], g_ffn[1], 1,
                      moe_w_router, moe_b_router, moe_w_gu, moe_b_gu, moe_w_down, moe_b_down)
```

```python
import functools
import math

import jax
import jax.numpy as jnp
from jax import lax
from jax.experimental import pallas as pl
from jax.experimental.pallas import tpu as pltpu

NORM_EPS = 1e-6
N_MIXERS = 2
S5_GROUP = 16
S5_STATE = 64
DA_HEAD_DIM = 64
DA_VDIM = 2 * DA_HEAD_DIM
GRID_W = 64
ROPE_BASE = 10000.0
N_EXPERTS = 32
TOP_K = 4
SWIGLU_LIMIT = 7.0
SWIGLU_ALPHA = 1.702
LOG2_E = 1.4426950408889634

LANES = 128
SUBLANES = 8
VMEM_LIMIT_BYTES = 56 * 1024 * 1024

S5_SLAB_GROUPS = LANES // S5_GROUP
S5_SLAB_STATE = S5_SLAB_GROUPS * S5_STATE
S5_CHUNK = 64
S5_ROW_BLOCKS = 4
TM_TILE = 32
ROW_TILE = 256
MOE_TILE = 512
ROUTER_TILE = 512
EXPERT_ROWS = 512
COMBINE_TILE = 256
Q_TILE = 512
ATTN_ROW_BLOCK = 128
HEADS_PER_STEP = 2
MOD_ROWS = 24

F32 = jnp.float32
BF16 = jnp.bfloat16


def _cparams(sem, vmem=VMEM_LIMIT_BYTES):
    return pltpu.CompilerParams(dimension_semantics=sem, vmem_limit_bytes=vmem)


def _rms(x):
    return x * lax.rsqrt(jnp.mean(x * x, axis=-1, keepdims=True) + NORM_EPS)


def _ada_kernel(c_ref, w_ref, b_ref, o_ref):
    cv = c_ref[...]
    s = cv * jax.nn.sigmoid(cv)
    o_ref[...] = jnp.dot(s, w_ref[...], precision=lax.Precision.HIGHEST,
                         preferred_element_type=F32) + b_ref[...]


def _ada(cc, w_ada, b_ada):
    depth, d, d6 = w_ada.shape
    nj = d6 // d
    return pl.pallas_call(
        _ada_kernel,
        out_shape=jax.ShapeDtypeStruct((depth, MOD_ROWS, d6), F32),
        grid=(depth, nj),
        in_specs=[pl.BlockSpec((MOD_ROWS, d), lambda i, j: (0, 0)),
                  pl.BlockSpec((None, d, d), lambda i, j: (i, 0, j)),
                  pl.BlockSpec((None, 1, d), lambda i, j: (i, 0, j))],
        out_specs=pl.BlockSpec((None, MOD_ROWS, d), lambda i, j: (i, 0, j)),
        compiler_params=_cparams(("parallel", "parallel")),
        name="ada",
    )(cc, w_ada, b_ada.reshape(depth, 1, d6))


def _prenorm_tm_kernel(ctx_ref, x_ref, mod_ref, g_ref, o_ref, *, n_ctx_tiles, nb, d):
    i = pl.program_id(0)
    tt = o_ref.shape[1] // nb
    g = g_ref[...]

    def emit(src_ref, mod_row):
        for b in range(nb):
            r = mod_row(b)
            sh = mod_ref[r:r + 1, 0:d]
            sc = mod_ref[r:r + 1, d:2 * d]
            h = (_rms(src_ref[b]) * g) * (1.0 + sc) + sh
            for s in range(d // LANES):
                o_ref[s, pl.ds(b, tt, stride=nb), :] = h[:, s * LANES:(s + 1) * LANES]

    @pl.when(i < n_ctx_tiles)
    def _():
        emit(ctx_ref, lambda b: nb)

    @pl.when(i >= n_ctx_tiles)
    def _():
        emit(x_ref, lambda b: b)


def _prenorm_tm(ctx, x, mod, g):
    nb, n_ctx, d = ctx.shape
    n_lat = x.shape[1]
    tt = TM_TILE
    nct, nlt = n_ctx // tt, n_lat // tt
    kern = functools.partial(_prenorm_tm_kernel, n_ctx_tiles=nct, nb=nb, d=d)
    return pl.pallas_call(
        kern,
        out_shape=jax.ShapeDtypeStruct((d // LANES, (n_ctx + n_lat) * nb, LANES), F32),
        grid=(nct + nlt,),
        in_specs=[pl.BlockSpec((nb, tt, d), lambda i: (0, jnp.minimum(i, nct - 1), 0)),
                  pl.BlockSpec((nb, tt, d), lambda i: (0, jnp.maximum(i - nct, 0), 0)),
                  pl.BlockSpec((MOD_ROWS, 2 * d), lambda i: (0, 0)),
                  pl.BlockSpec((1, d), lambda i: (0, 0))],
        out_specs=pl.BlockSpec((d // LANES, tt * nb, LANES), lambda i: (0, i, 0)),
        compiler_params=_cparams(("parallel",)),
        name="prenorm_tm",
    )(ctx, x, mod, g)


def _s5_disc_kernel(ar_ref, ai_ref, ldt_ref, br_ref, bi_ref, lr_ref, li_ref, bbr_ref, bbi_ref):
    ar, ai = ar_ref[...], ai_ref[...]
    dt = jnp.exp(ldt_ref[...])
    mag = jnp.exp(ar * dt)
    lr = mag * jnp.cos(ai * dt)
    li = mag * jnp.sin(ai * dt)
    nr, ni = lr - 1.0, li
    den = ar * ar + ai * ai
    qr = (nr * ar + ni * ai) / den
    qi = (ni * ar - nr * ai) / den
    br, bi = br_ref[...], bi_ref[...]
    lr_ref[...] = lr
    li_ref[...] = li
    bbr_ref[...] = qr * br - qi * bi
    bbi_ref[...] = qr * bi + qi * br


def _s5_disc(a_re, a_im, log_dt, b_re, b_im):
    nd, g, p = a_re.shape
    cg = b_re.shape[-1]
    gp = g * p
    col = lambda a: a.reshape(nd, gp, 1)
    ldt = jnp.broadcast_to(log_dt[:, :, None], (nd, g, p))
    spec1 = pl.BlockSpec((None, gp, 1), lambda i: (i, 0, 0))
    specb = pl.BlockSpec((None, gp, cg), lambda i: (i, 0, 0))
    lr, li, bbr, bbi = pl.pallas_call(
        _s5_disc_kernel,
        out_shape=[jax.ShapeDtypeStruct((nd, gp, 1), F32)] * 2 + [jax.ShapeDtypeStruct((nd, gp, cg), F32)] * 2,
        grid=(nd,),
        in_specs=[spec1, spec1, spec1, specb, specb],
        out_specs=[spec1, spec1, specb, specb],
        compiler_params=_cparams(("parallel",)),
        name="s5_disc",
    )(col(a_re), col(a_im), col(ldt), b_re.reshape(nd, gp, cg), b_im.reshape(nd, gp, cg))
    return (lr.reshape(nd, g, p), li.reshape(nd, g, p),
            bbr.reshape(nd, g, p, cg), bbi.reshape(nd, g, p, cg))


def _s5_slab_params(lam_re, lam_im, bb_re, bb_im, c_re, c_im):
    nd, g, p = lam_re.shape
    cg = bb_re.shape[-1]
    ns, sg = g // S5_SLAB_GROUPS, S5_SLAB_GROUPS
    eye = jnp.eye(sg, dtype=F32)

    def b_mat(bb):
        bb = bb.reshape(nd, ns, sg, p, cg)
        return jnp.einsum('dsjpc,jk->dsjckp', bb, eye).reshape(nd, ns, sg * cg, sg * p)

    def c_mat(cm):
        cm = cm.reshape(nd, ns, sg, cg, p)
        return jnp.einsum('dsjcp,jk->dskpjc', cm, eye).reshape(nd, ns, sg * p, sg * cg)

    bm = jnp.concatenate([b_mat(bb_re), b_mat(bb_im)], axis=-1).astype(BF16)
    lam = lambda a: a.reshape(nd, ns, 1, sg * p)
    return bm, c_mat(c_re).astype(BF16), c_mat(c_im).astype(BF16), lam(lam_re), lam(lam_im)


def _s5_scan_kernel(u_ref, bm_ref, cr_ref, ci_ref, lr_ref, li_ref, y_ref, x0_ref, x1_ref, h0_ref, h1_ref, st_ref):
    d = pl.program_id(0)
    c = pl.program_id(2)
    t_len, nb = u_ref.shape[0], u_ref.shape[1] * u_ref.shape[2]
    ns = S5_SLAB_STATE
    xs, hs = (x0_ref, x1_ref), (h0_ref, h1_ref)

    @pl.when(c == 0)
    def _():
        for ref in xs + hs:
            ref[...] = jnp.zeros_like(ref)

    def stages(in_ref, bu_ref, h_ref, out_ref):
        lr = jnp.broadcast_to(lr_ref[...], (SUBLANES, ns))
        li = jnp.broadcast_to(li_ref[...], (SUBLANES, ns))
        fresh = c <= 1
        state = [(jnp.where(fresh, 0.0, st_ref[0, o]), jnp.where(fresh, 0.0, st_ref[1, o]))
                 for o in range(nb // SUBLANES)]
        tb = t_len // S5_ROW_BLOCKS
        for j in range(S5_ROW_BLOCKS):
            rsl = slice(j * tb * nb, (j + 1) * tb * nb)
            u = u_ref[j * tb:(j + 1) * tb].reshape(tb * nb, LANES).astype(BF16)
            in_ref[rsl, :] = jnp.dot(u, bm_ref[...], preferred_element_type=F32)

            for s in range(j * tb, (j + 1) * tb):
                t = jnp.where(d == 0, s, t_len - 1 - s)
                row = pl.multiple_of(t * nb, nb)
                for o in range(nb // SUBLANES):
                    hr, hi = state[o]
                    br = bu_ref[pl.ds(row + o * SUBLANES, SUBLANES), 0:ns]
                    bi = bu_ref[pl.ds(row + o * SUBLANES, SUBLANES), ns:2 * ns]
                    state[o] = (lr * hr - li * hi + br, lr * hi + li * hr + bi)
                h_ref[pl.ds(row, nb), 0:ns] = jnp.concatenate([st[0] for st in state], axis=0).astype(BF16)
                h_ref[pl.ds(row, nb), ns:2 * ns] = jnp.concatenate([st[1] for st in state], axis=0).astype(BF16)

            y = (jnp.dot(out_ref[rsl, 0:ns], cr_ref[...], preferred_element_type=F32)
                 - jnp.dot(out_ref[rsl, ns:2 * ns], ci_ref[...], preferred_element_type=F32))
            y_ref[j * tb:(j + 1) * tb] = y.reshape((tb,) + y_ref.shape[1:])
        for o in range(nb // SUBLANES):
            st_ref[0, o] = state[o][0]
            st_ref[1, o] = state[o][1]

    phase = lax.rem(c, 2)
    for r in range(2):
        @pl.when(phase == r)
        def _(r=r):
            stages(xs[r], xs[1 - r], hs[1 - r], hs[r])


def _s5_scan(u_t, bm, cr, ci, lam_r, lam_i, nb, n_ctx, n_tot):
    nslab = u_t.shape[0]
    nbo = nb // SUBLANES
    t_len = S5_CHUNK
    ncc, nc = n_ctx // t_len, n_tot // t_len
    u5 = u_t.reshape(nslab, n_tot, nbo, SUBLANES, LANES)

    def chunk(d, c):
        c = jnp.clip(c, 0, nc - 1)
        back = jnp.where(c < ncc, ncc - 1 - c, ncc + nc - 1 - c)
        return jnp.where(d == 0, c, back)

    pspec = lambda r, k: pl.BlockSpec((None, None, r, k), lambda d, s, c: (d, s, 0, 0))
    rows = t_len * nb
    return pl.pallas_call(
        _s5_scan_kernel,
        out_shape=jax.ShapeDtypeStruct((2, nslab, n_tot, nbo, SUBLANES, LANES), F32),
        grid=(2, nslab, nc + 2),
        in_specs=[pl.BlockSpec((None, t_len, nbo, SUBLANES, LANES), lambda d, s, c: (s, chunk(d, c), 0, 0, 0)),
                  pspec(LANES, 2 * S5_SLAB_STATE), pspec(S5_SLAB_STATE, LANES), pspec(S5_SLAB_STATE, LANES),
                  pspec(1, S5_SLAB_STATE), pspec(1, S5_SLAB_STATE)],
        out_specs=pl.BlockSpec((None, None, t_len, nbo, SUBLANES, LANES),
                               lambda d, s, c: (d, s, chunk(d, c - 2), 0, 0, 0)),
        scratch_shapes=[pltpu.VMEM((rows, 2 * S5_SLAB_STATE), F32)] * 2
                       + [pltpu.VMEM((rows, 2 * S5_SLAB_STATE), BF16)] * 2
                       + [pltpu.VMEM((2, nbo, SUBLANES, S5_SLAB_STATE), F32)],
        compiler_params=_cparams(("parallel", "parallel", "arbitrary")),
        name="s5_scan",
    )(u5, bm, cr, ci, lam_r, lam_i)


def _s5_out_kernel(y_ref, u_ref, dsk_ref, w_ref, bg_ref, ctx_ref, x_ref, mod_ref, o_ref, tr_ref,
                   *, n_ctx_tiles, nb, d):
    i = pl.program_id(0)
    nslab = d // LANES
    rows = u_ref.shape[1]
    tt = rows // nb
    y = (y_ref[0] + y_ref[1]).reshape(nslab, rows, LANES)
    v = jax.nn.gelu(y + dsk_ref[...] * u_ref[...])
    lhs = jnp.concatenate([v[s] for s in range(nslab)], axis=-1).astype(BF16)
    z = jnp.dot(lhs, w_ref[...], preferred_element_type=F32) + bg_ref[...]
    m = z[:, 0:d] * jax.nn.sigmoid(z[:, d:2 * d])
    for s in range(nslab):
        tr_ref[s] = m[:, s * LANES:(s + 1) * LANES]

    def emit(src_ref, mod_row):
        for b in range(nb):
            r = mod_row(b)
            gt = mod_ref[r:r + 1, :]
            mb = jnp.concatenate([tr_ref[s, pl.ds(b, tt, stride=nb), :] for s in range(nslab)], axis=-1)
            o_ref[b] = src_ref[b] + gt * mb

    @pl.when(i < n_ctx_tiles)
    def _():
        emit(ctx_ref, lambda b: nb)

    @pl.when(i >= n_ctx_tiles)
    def _():
        emit(x_ref, lambda b: b)


def _s5_out(y_t, u_t, d_skip, w_glu, b_glu, ctx, x, gt1):
    nb, n_ctx, d = ctx.shape
    n_lat = x.shape[1]
    n_tot = n_ctx + n_lat
    nslab, nbo = d // LANES, nb // SUBLANES
    tt = TM_TILE
    nct, nlt = n_ctx // tt, n_lat // tt
    kern = functools.partial(_s5_out_kernel, n_ctx_tiles=nct, nb=nb, d=d)
    return pl.pallas_call(
        kern,
        out_shape=jax.ShapeDtypeStruct((nb, n_tot, d), F32),
        grid=(nct + nlt,),
        in_specs=[pl.BlockSpec((2, nslab, tt, nbo, SUBLANES, LANES), lambda i: (0, 0, i, 0, 0, 0)),
                  pl.BlockSpec((nslab, tt * nb, LANES), lambda i: (0, i, 0)),
                  pl.BlockSpec((nslab, 1, LANES), lambda i: (0, 0, 0)),
                  pl.BlockSpec((d, 2 * d), lambda i: (0, 0)),
                  pl.BlockSpec((1, 2 * d), lambda i: (0, 0)),
                  pl.BlockSpec((nb, tt, d), lambda i: (0, jnp.minimum(i, nct - 1), 0)),
                  pl.BlockSpec((nb, tt, d), lambda i: (0, jnp.maximum(i - nct, 0), 0)),
                  pl.BlockSpec((MOD_ROWS, d), lambda i: (0, 0))],
        out_specs=pl.BlockSpec((nb, tt, d), lambda i: (0, i, 0)),
        scratch_shapes=[pltpu.VMEM((nslab, tt * nb, LANES), F32)],
        compiler_params=_cparams(("parallel",)),
        name="s5_out",
    )(y_t, u_t, d_skip.reshape(nslab, 1, LANES), w_glu.astype(BF16), b_glu.reshape(1, 2 * d), ctx, x, gt1)


def _s5_layer(xs_ctx, xs_lat, mod, g_mix, a_re, a_im, log_dt, b_re, b_im, c_re, c_im, d_skip, w_glu, b_glu):
    nb, n_ctx, d = xs_ctx.shape
    n_tot = n_ctx + xs_lat.shape[1]
    u_t = _prenorm_tm(xs_ctx, xs_lat, mod[:, 0:2 * d], g_mix.reshape(1, d))
    lam_re, lam_im, bb_re, bb_im = _s5_disc(a_re, a_im, log_dt, b_re, b_im)
    bm, cr, ci, lam_r, lam_i = _s5_slab_params(lam_re, lam_im, bb_re, bb_im, c_re, c_im)
    y_t = _s5_scan(u_t, bm, cr, ci, lam_r, lam_i, nb, n_ctx, n_tot)
    return _s5_out(y_t, u_t, d_skip, w_glu, b_glu, xs_ctx, xs_lat, mod[:, 2 * d:3 * d])


NEG_PAD = -1e30
NEG_MASK = -3e38


def _split_bf16(a):
    hi = a.astype(BF16)
    return hi, (a - hi.astype(F32)).astype(BF16)


def _mod_row(piece, pieces_per_batch, n_ctx_pieces, nb):
    b = piece // pieces_per_batch
    return jnp.where(piece - b * pieces_per_batch < n_ctx_pieces, nb, b)


def _router_kernel(x_ref, mod_ref, g_ref, w_ref, bias_ref, h_ref, sel_ref, selt_ref, meta_ref, cnt_ref,
                   carry_ref, *, pieces_per_batch, n_ctx_pieces, nb, d):
    i = pl.program_id(0)
    rows = x_ref.shape[0]

    @pl.when(i == 0)
    def _():
        carry_ref[...] = jnp.zeros_like(carry_ref)

    parts = []
    for p in range(rows // ROW_TILE):
        row = _mod_row(i * (rows // ROW_TILE) + p, pieces_per_batch, n_ctx_pieces, nb)
        mrow = mod_ref[pl.ds(row, 1), :]
        xp = x_ref[p * ROW_TILE:(p + 1) * ROW_TILE, :]
        parts.append((_rms(xp) * g_ref[...]) * (1.0 + mrow[:, d:2 * d]) + mrow[:, 0:d])
    h = jnp.concatenate(parts, axis=0)
    h_ref[...] = h.astype(BF16)

    hi, lo = _split_bf16(h)
    whi, wlo = _split_bf16(w_ref[...])
    dot = functools.partial(jnp.dot, preferred_element_type=F32)
    logits = dot(hi, whi) + dot(lo, whi) + dot(hi, wlo) + bias_ref[...]

    lane = lax.broadcasted_iota(jnp.int32, (rows, LANES), 1)
    lane_f = lane.astype(F32)
    work = logits
    tops, idxs, hots = [], [], []
    for _ in range(TOP_K):
        m = jnp.max(work, axis=-1, keepdims=True)
        idx = jnp.min(jnp.where(work == m, lane_f, float(LANES)), axis=-1, keepdims=True)
        hot = lane_f == idx
        work = jnp.where(hot, NEG_MASK, work)
        tops.append(m)
        idxs.append(idx)
        hots.append(hot)
    exps = [jnp.exp(m - tops[0]) for m in tops]
    denom = exps[0] + exps[1] + exps[2] + exps[3]

    multi = jnp.zeros((rows, LANES), F32)
    for hot in hots:
        multi = multi + jnp.where(hot, 1.0, 0.0)
    r_i = lax.broadcasted_iota(jnp.int32, (rows, rows), 0)
    c_i = lax.broadcasted_iota(jnp.int32, (rows, rows), 1)
    tri = jnp.where(c_i < r_i, 1.0, 0.0).astype(BF16)
    in_tile = dot(tri, multi.astype(BF16))
    cnt_tile = jnp.sum(multi, axis=0, keepdims=True)
    e_r = lax.broadcasted_iota(jnp.int32, (LANES, LANES), 0)
    e_c = lax.broadcasted_iota(jnp.int32, (LANES, LANES), 1)
    below = jnp.where(e_r < e_c, 1.0, 0.0).astype(BF16)
    c_hi, c_lo = _split_bf16(jnp.broadcast_to(cnt_tile, (SUBLANES, LANES)))
    prefix = (dot(c_hi, below) + dot(c_lo, below))[0:1, :]
    slot = in_tile + prefix

    sel = jnp.zeros((rows, LANES), F32)
    for k in range(TOP_K):
        pos = jnp.sum(jnp.where(hots[k], slot, 0.0), axis=-1, keepdims=True)
        sel = jnp.where(lane == k, idxs[k].astype(F32), sel)
        sel = jnp.where(lane == TOP_K + k, exps[k] / denom, sel)
        sel = jnp.where(lane == 2 * TOP_K + k, pos, sel)
    sel_ref[...] = sel
    selt_ref[...] = sel.T[2 * TOP_K:2 * TOP_K + SUBLANES, :].astype(jnp.int32)

    carry = carry_ref[0:1, :]
    sub = lax.broadcasted_iota(jnp.int32, (SUBLANES, LANES), 0)
    meta = jnp.where(sub == 0, cnt_tile, jnp.where(sub == 1, carry, jnp.where(sub == 2, prefix, 0.0)))
    meta_ref[...] = meta.astype(jnp.int32)
    carry_ref[0:1, :] = carry + cnt_tile
    cnt_ref[...] = jnp.broadcast_to(carry + cnt_tile, cnt_ref.shape).astype(jnp.int32)


def _router(xs, n_ctx, mod_shsc, g, w_router, b_router):
    nb, t_len, d = xs.shape
    r = MOE_TILE
    n_tok = nb * t_len
    nt = n_tok // r
    ne = w_router.shape[1]
    w_pad = jnp.zeros((d, LANES), F32).at[:, :ne].set(w_router)
    b_pad = jnp.full((1, LANES), NEG_PAD, F32).at[0, :ne].set(b_router)
    kern = functools.partial(_router_kernel, pieces_per_batch=t_len // ROW_TILE,
                             n_ctx_pieces=n_ctx // ROW_TILE, nb=nb, d=d)
    tile = lambda i: (i, 0)
    const = lambda i: (0, 0)
    return pl.pallas_call(
        kern,
        out_shape=[jax.ShapeDtypeStruct((n_tok, d), BF16),
                   jax.ShapeDtypeStruct((n_tok, LANES), F32),
                   jax.ShapeDtypeStruct((nt * SUBLANES, r), jnp.int32),
                   jax.ShapeDtypeStruct((nt, SUBLANES, LANES), jnp.int32),
                   jax.ShapeDtypeStruct((SUBLANES, LANES), jnp.int32)],
        grid=(nt,),
        in_specs=[pl.BlockSpec((r, d), tile),
                  pl.BlockSpec((MOD_ROWS, 2 * d), const),
                  pl.BlockSpec((1, d), const),
                  pl.BlockSpec((d, LANES), const),
                  pl.BlockSpec((1, LANES), const)],
        out_specs=[pl.BlockSpec((r, d), tile),
                   pl.BlockSpec((r, LANES), tile),
                   pl.BlockSpec((SUBLANES, r), tile),
                   pl.BlockSpec((None, SUBLANES, LANES), lambda i: (i, 0, 0)),
                   pl.BlockSpec((SUBLANES, LANES), const)],
        scratch_shapes=[pltpu.VMEM((SUBLANES, LANES), F32)],
        compiler_params=_cparams(("arbitrary",)),
        name="router",
    )(xs.reshape(n_tok, d), mod_shsc, g.reshape(1, d), w_pad, b_pad)


def _plan_kernel(cnt_ref, ps_ref, be_ref, first_ref, nused_ref, *, ne, rb):
    n_blocks = be_ref.shape[0]

    def fill(i, carry):
        be_ref[i] = ne - 1
        first_ref[i] = 0
        return carry
    lax.fori_loop(0, n_blocks, fill, 0)

    def per_expert(e, acc):
        nblk = (cnt_ref[0, e] + (rb - 1)) // rb
        ps_ref[e] = acc
        b0 = acc // rb

        def mark(i, carry):
            be_ref[b0 + i] = e
            first_ref[b0 + i] = jnp.where(i == 0, 1, 0)
            return carry
        lax.fori_loop(0, nblk, mark, 0)
        return acc + nblk * rb

    total = lax.fori_loop(0, ne, per_expert, 0)

    def tail(e, carry):
        ps_ref[e] = total
        return carry
    lax.fori_loop(ne, ps_ref.shape[0], tail, 0)
    nused_ref[0] = total // rb


def _plan(cnt, ne, n_blocks):
    smem = pl.BlockSpec(memory_space=pltpu.SMEM)
    return pl.pallas_call(
        functools.partial(_plan_kernel, ne=ne, rb=EXPERT_ROWS),
        out_shape=[jax.ShapeDtypeStruct((LANES,), jnp.int32),
                   jax.ShapeDtypeStruct((n_blocks,), jnp.int32),
                   jax.ShapeDtypeStruct((n_blocks,), jnp.int32),
                   jax.ShapeDtypeStruct((1,), jnp.int32)],
        in_specs=[smem],
        out_specs=[smem, smem, smem, smem],
        name="plan",
    )(cnt)


_SEG_SIZES = tuple(MOE_TILE >> s for s in range(MOE_TILE.bit_length()))
TILE_SLOTS = MOE_TILE * TOP_K
TOKEN_ROWS = SUBLANES


def _segment_copies(n, src, dst, make_copy, wait=False):
    for k, p in enumerate(_SEG_SIZES):
        take = (n & p) != 0

        @pl.when(take)
        def _():
            cp = make_copy(pl.multiple_of(src * TOKEN_ROWS, TOKEN_ROWS),
                           pl.multiple_of(dst * TOKEN_ROWS, TOKEN_ROWS), p * TOKEN_ROWS)
            cp.wait() if wait else cp.start(priority=k % 2)
        step = jnp.where(take, p, 0)
        src = src + step
        dst = dst + step


def _tile_major(x):
    return [x[:, s * LANES:(s + 1) * LANES] for s in range(x.shape[1] // LANES)]


def _dispatch_kernel(meta_ref, metap_ref, metapp_ref, ps_ref, cnt_ref, h_ref, selt_ref, xbuf_ref, xs_ref,
                     zero_ref, sem, zsem, *, ne, rb):
    rows = h_ref.shape[0]
    slots = xs_ref.shape[1] // TOKEN_ROWS

    @pl.when(pl.program_id(0) == 0)
    def _():
        zero_ref[...] = jnp.zeros_like(zero_ref)

        def pad_copy(_, dst, p):
            return pltpu.make_async_copy(zero_ref.at[pl.ds(0, p), :], xbuf_ref.at[pl.ds(dst, p), :], zsem)

        for wait in (False, True):
            def pad_rows(e, carry, wait=wait):
                cnt = cnt_ref[0, e]
                _segment_copies((-cnt) & (rb - 1), 0, ps_ref[e] + cnt, pad_copy, wait)
                return carry
            lax.fori_loop(0, ne, pad_rows, 0)

    step = pl.program_id(0)
    last = pl.num_programs(0) - 1
    slot = lax.rem(step, 3)
    slot_p = lax.rem(step + 2, 3)
    slot_pp = lax.rem(step + 1, 3)

    def segments(mref, which, wait, e0=0, e1=ne):
        def seg_copy(src, dst, p):
            return pltpu.make_async_copy(xs_ref.at[which, pl.ds(src, p), :], xbuf_ref.at[pl.ds(dst, p), :],
                                         sem.at[which])

        def per_expert(e, carry):
            _segment_copies(mref[0, e], mref[2, e], ps_ref[e] + mref[1, e], seg_copy, wait)
            return carry
        lax.fori_loop(e0, e1, per_expert, 0)

    def start_previous(half):
        @pl.when(step > 0)
        def _():
            segments(metap_ref, slot_p, False, half * ne // 2, (half + 1) * ne // 2)

    start_previous(0)
    pos = selt_ref[0:TOP_K, :]
    s_iota = lax.broadcasted_iota(jnp.int32, (slots, rows), 0)
    perm = jnp.zeros((slots, rows), F32)
    for k in range(TOP_K):
        perm = perm + jnp.where(s_iota == pos[k:k + 1, :], 1.0, 0.0)
    sorted_rows = jnp.dot(perm.astype(BF16), h_ref[...], preferred_element_type=F32)
    for s, slab in enumerate(_tile_major(sorted_rows)):
        xs_ref[slot, pl.ds(s, slots, stride=TOKEN_ROWS), :] = slab
    start_previous(1)

    @pl.when(step > 1)
    def _():
        segments(metapp_ref, slot_pp, True)

    @pl.when(step == last)
    def _():
        segments(meta_ref, slot, False)

        @pl.when(step > 0)
        def _():
            segments(metap_ref, slot_p, True)
        segments(meta_ref, slot, True)


def _dispatch(h2, selt, meta, ps, cnt, ne, n_buf):
    n_tok, d = h2.shape
    r = MOE_TILE
    smem = pl.BlockSpec(memory_space=pltpu.SMEM)
    return pl.pallas_call(
        functools.partial(_dispatch_kernel, ne=ne, rb=EXPERT_ROWS),
        out_shape=jax.ShapeDtypeStruct((n_buf * TOKEN_ROWS, LANES), F32),
        grid=(n_tok // r,),
        in_specs=[pl.BlockSpec((None, SUBLANES, LANES), lambda i: (i, 0, 0), memory_space=pltpu.SMEM),
                  pl.BlockSpec((None, SUBLANES, LANES), lambda i: (jnp.maximum(i - 1, 0), 0, 0),
                               memory_space=pltpu.SMEM),
                  pl.BlockSpec((None, SUBLANES, LANES), lambda i: (jnp.maximum(i - 2, 0), 0, 0),
                               memory_space=pltpu.SMEM),
                  smem, smem,
                  pl.BlockSpec((r, d), lambda i: (i, 0)),
                  pl.BlockSpec((SUBLANES, r), lambda i: (i, 0))],
        out_specs=pl.BlockSpec(memory_space=pl.ANY),
        scratch_shapes=[pltpu.VMEM((3, TILE_SLOTS * TOKEN_ROWS, LANES), F32),
                        pltpu.VMEM((_SEG_SIZES[0] * TOKEN_ROWS, LANES), F32),
                        pltpu.SemaphoreType.DMA((3,)), pltpu.SemaphoreType.DMA(())],
        compiler_params=_cparams(("arbitrary",)),
        name="dispatch",
    )(meta, meta, meta, ps, cnt, h2, selt)


def _expert_kernel(be_ref, first_ref, nused_ref, x_ref, wgu_ref, bgu_ref, wd_ref, bd_ref, y_ref,
                   wgu_bf, wd_bf):
    i = pl.program_id(0)
    dff = wd_ref.shape[0]

    @pl.when(i < nused_ref[0])
    def _():
        @pl.when(first_ref[i] == 1)
        def _():
            wgu_bf[...] = wgu_ref[...].astype(BF16)
            wd_bf[...] = wd_ref[...].astype(BF16)

        rows = x_ref.shape[0] // TOKEN_ROWS
        x = jnp.concatenate([x_ref[pl.ds(s, rows, stride=TOKEN_ROWS), :] for s in range(TOKEN_ROWS)], axis=-1)
        gu = jnp.dot(x.astype(BF16), wgu_bf[...], preferred_element_type=F32) + bgu_ref[...]
        gate = jnp.minimum(gu[:, 0:dff], SWIGLU_LIMIT)
        up = jnp.clip(gu[:, dff:2 * dff], -SWIGLU_LIMIT, SWIGLU_LIMIT)
        act = (up + 1.0) * gate * jax.nn.sigmoid(SWIGLU_ALPHA * gate)
        y = jnp.dot(act.astype(BF16), wd_bf[...], preferred_element_type=F32) + bd_ref[...]
        for s, slab in enumerate(_tile_major(y)):
            y_ref[pl.ds(s, rows, stride=TOKEN_ROWS), :] = slab

    @pl.when(i >= nused_ref[0])
    def _():
        y_ref[...] = jnp.zeros_like(y_ref)


def _experts(xbuf, block_e, first, n_used, layer, w_gu, b_gu, w_down, b_down):
    depth, ne, d, dff2 = w_gu.shape
    assert d == TOKEN_ROWS * LANES
    dff = dff2 // 2
    rb = EXPERT_ROWS * TOKEN_ROWS
    blk = lambda i, nu: jnp.minimum(i, nu[0] - 1)
    wmap = lambda i, be, fi, nu: (layer, be[blk(i, nu)], 0, 0)
    grid_spec = pltpu.PrefetchScalarGridSpec(
        num_scalar_prefetch=3,
        grid=(xbuf.shape[0] // rb,),
        in_specs=[pl.BlockSpec((rb, LANES), lambda i, be, fi, nu: (blk(i, nu), 0)),
                  pl.BlockSpec((None, None, d, dff2), wmap),
                  pl.BlockSpec((None, None, 1, dff2), wmap),
                  pl.BlockSpec((None, None, dff, d), wmap),
                  pl.BlockSpec((None, None, 1, d), wmap)],
        out_specs=pl.BlockSpec((rb, LANES), lambda i, be, fi, nu: (i, 0)),
        scratch_shapes=[pltpu.VMEM((d, dff2), BF16), pltpu.VMEM((dff, d), BF16)])
    return pl.pallas_call(
        _expert_kernel,
        out_shape=jax.ShapeDtypeStruct(xbuf.shape, F32),
        grid_spec=grid_spec,
        compiler_params=_cparams(("arbitrary",)),
        name="experts",
    )(block_e, first, n_used, xbuf, w_gu, b_gu.reshape(depth, ne, 1, dff2), w_down,
      b_down.reshape(depth, ne, 1, d))


def _combine_kernel(meta_ref, metan_ref, ps_ref, sel_ref, x_ref, mod_ref, ybuf_ref, o_ref, ys_ref, sem,
                    *, pieces_per_batch, n_ctx_pieces, nb, ne):
    step = pl.program_id(0)
    n_steps = pl.num_programs(0)
    slot = lax.rem(step, 2)
    rows = x_ref.shape[0]
    slots = ys_ref.shape[1] // TOKEN_ROWS

    def segments(mref, to_slot, wait, e0=0, e1=ne):
        def seg_copy(src, dst, p):
            return pltpu.make_async_copy(ybuf_ref.at[pl.ds(dst, p), :], ys_ref.at[to_slot, pl.ds(src, p), :],
                                         sem.at[to_slot])

        def per_expert(e, carry):
            _segment_copies(mref[0, e], mref[2, e], ps_ref[e] + mref[1, e], seg_copy, wait)
            return carry
        lax.fori_loop(e0, e1, per_expert, 0)

    def start_next(half):
        @pl.when(step + 1 < n_steps)
        def _():
            segments(metan_ref, 1 - slot, False, half * ne // 2, (half + 1) * ne // 2)

    @pl.when(step == 0)
    def _():
        segments(meta_ref, 0, False)

    segments(meta_ref, slot, True)

    start_next(0)
    sel = sel_ref[...]
    l_iota = lax.broadcasted_iota(jnp.int32, (rows, slots), 1).astype(F32)
    w = jnp.zeros((rows, slots), F32)
    for k in range(TOP_K):
        w = w + jnp.where(l_iota == sel[:, 2 * TOP_K + k:2 * TOP_K + k + 1], sel[:, TOP_K + k:TOP_K + k + 1], 0.0)
    w = w.astype(BF16)
    ys = jnp.concatenate([ys_ref[slot, pl.ds(s, slots, stride=TOKEN_ROWS), :] for s in range(TOKEN_ROWS)],
                         axis=-1).astype(BF16)
    acc = jnp.dot(w, ys, preferred_element_type=F32)
    start_next(1)
    for p in range(rows // ROW_TILE):
        row = _mod_row(step * (rows // ROW_TILE) + p, pieces_per_batch, n_ctx_pieces, nb)
        psl = slice(p * ROW_TILE, (p + 1) * ROW_TILE)
        o_ref[psl, :] = x_ref[psl, :] + mod_ref[pl.ds(row, 1), :] * acc[psl, :]


def _combine(ybuf, meta, ps, sel, xs, n_ctx, mod_gt, ne):
    nb, t_len, d = xs.shape
    r = MOE_TILE
    n_tok = nb * t_len
    nt = n_tok // r
    kern = functools.partial(_combine_kernel, pieces_per_batch=t_len // ROW_TILE,
                             n_ctx_pieces=n_ctx // ROW_TILE, nb=nb, ne=ne)
    mspec = lambda f: pl.BlockSpec((None, SUBLANES, LANES), f, memory_space=pltpu.SMEM)
    out = pl.pallas_call(
        kern,
        out_shape=jax.ShapeDtypeStruct((n_tok, d), F32),
        grid=(nt,),
        in_specs=[mspec(lambda i: (i, 0, 0)),
                  mspec(lambda i: (jnp.minimum(i + 1, nt - 1), 0, 0)),
                  pl.BlockSpec(memory_space=pltpu.SMEM),
                  pl.BlockSpec((r, LANES), lambda i: (i, 0)),
                  pl.BlockSpec((r, d), lambda i: (i, 0)),
                  pl.BlockSpec((MOD_ROWS, d), lambda i: (0, 0)),
                  pl.BlockSpec(memory_space=pl.ANY)],
        out_specs=pl.BlockSpec((r, d), lambda i: (i, 0)),
        scratch_shapes=[pltpu.VMEM((2, TILE_SLOTS * TOKEN_ROWS, LANES), F32), pltpu.SemaphoreType.DMA((2,))],
        compiler_params=_cparams(("arbitrary",)),
        name="combine",
    )(meta, meta, ps, sel, xs.reshape(n_tok, d), mod_gt, ybuf)
    return out.reshape(nb, t_len, d)


def _moe_layer(xs, n_ctx, mod_shsc, mod_gt, g, layer, w_router, b_router, w_gu, b_gu, w_down, b_down):
    nb, t_len, d = xs.shape
    ne = w_router.shape[-1]
    n_tiles = nb * t_len // MOE_TILE
    n_blocks = n_tiles * TILE_SLOTS // EXPERT_ROWS + ne
    h2, sel, selt, meta, cnt = _router(xs, n_ctx, mod_shsc, g, w_router[layer], b_router[layer])
    ps, block_e, first, n_used = _plan(cnt, ne, n_blocks)
    xbuf = _dispatch(h2, selt, meta, ps, cnt, ne, n_blocks * EXPERT_ROWS)
    ybuf = _experts(xbuf, block_e, first, n_used, layer, w_gu, b_gu, w_down, b_down)
    return _combine(ybuf, meta, ps, sel, xs, n_ctx, mod_gt, ne)


def _qkv_kernel(x_ref, mod_ref, g_ref, w_ref, q_ref, k_ref, v_ref, *, n_ctx_tiles, nb, d):
    b = pl.program_id(0)
    j = pl.program_id(1)
    row = jnp.where(j < n_ctx_tiles, nb, b)
    mrow = mod_ref[pl.ds(row, 1), :]
    h = (_rms(x_ref[...]) * g_ref[...]) * (1.0 + mrow[:, d:2 * d]) + mrow[:, 0:d]
    qkv = jnp.dot(h.astype(BF16), w_ref[...], preferred_element_type=F32)
    for hd in range(d // DA_VDIM):
        lo = hd * DA_VDIM
        q_ref[hd] = qkv[:, lo:lo + DA_VDIM].astype(BF16)
        k_ref[hd] = qkv[:, d + lo:d + lo + DA_VDIM].astype(BF16)
        v_ref[hd] = qkv[:, 2 * d + lo:2 * d + lo + DA_VDIM].astype(BF16)


def _qkv(xs, n_ctx, mod_shsc, g, w_qkv):
    nb, t_len, d = xs.shape
    r = ROW_TILE
    nh = d // DA_VDIM
    kern = functools.partial(_qkv_kernel, n_ctx_tiles=n_ctx // r, nb=nb, d=d)
    nct = n_ctx // r
    hspec = pl.BlockSpec((None, nh, r, DA_VDIM), lambda b, j: (b, 0, j, 0))
    qspec = pl.BlockSpec((None, nh, r, DA_VDIM), lambda b, j: (b, 0, jnp.maximum(j - nct, 0), 0))
    kv_shape = jax.ShapeDtypeStruct((nb, nh, t_len, DA_VDIM), BF16)
    return pl.pallas_call(
        kern,
        out_shape=[jax.ShapeDtypeStruct((nb, nh, t_len - n_ctx, DA_VDIM), BF16), kv_shape, kv_shape],
        grid=(nb, t_len // r),
        in_specs=[pl.BlockSpec((None, r, d), lambda b, j: (b, j, 0)),
                  pl.BlockSpec((MOD_ROWS, 2 * d), lambda b, j: (0, 0)),
                  pl.BlockSpec((1, d), lambda b, j: (0, 0)),
                  pl.BlockSpec((d, 3 * d), lambda b, j: (0, 0))],
        out_specs=[qspec, hspec, hspec],
        compiler_params=_cparams(("parallel", "arbitrary")),
        name="qkv",
    )(xs, mod_shsc, g.reshape(1, d), w_qkv.astype(BF16))


def _attn_kernel(q_ref, k_ref, v_ref, qg_ref, kg_ref, cosq_ref, sinq_ref, cosk_ref, sink_ref, lam_ref,
                 sg_ref, o_ref, kn_ref, *, n_ctx, lambda_init):
    qi = pl.program_id(2)
    n_heads, n_tot = k_ref.shape[0], k_ref.shape[1]
    tq = q_ref.shape[1]
    dh = DA_HEAD_DIM
    lane = lax.broadcasted_iota(jnp.int32, (1, DA_VDIM), 1)
    comp0 = lane < dh
    low_half = lax.rem(lane, dh) < dh // 2

    def head_norm(xv, gain):
        x2 = xv * xv
        s0 = jnp.sum(jnp.where(comp0, x2, 0.0), axis=-1, keepdims=True)
        s1 = jnp.sum(jnp.where(comp0, 0.0, x2), axis=-1, keepdims=True)
        ms = jnp.where(comp0, s0, s1) * (1.0 / dh)
        return xv * lax.rsqrt(ms + NORM_EPS) * gain

    def rope(xv, cos, sin_signed):
        partner = jnp.where(low_half, pltpu.roll(xv, DA_VDIM - dh // 2, axis=1),
                            pltpu.roll(xv, dh // 2, axis=1))
        return xv * cos + partner * sin_signed

    @pl.when(qi == 0)
    def _():
        kg = kg_ref[...]
        for g in range(n_heads):
            kn_ref[g, 0:n_ctx, :] = head_norm(k_ref[g, 0:n_ctx, :].astype(F32), kg).astype(BF16)
            for c0 in range(n_ctx, n_tot, tq):
                kk = head_norm(k_ref[g, c0:c0 + tq, :].astype(F32), kg)
                kn_ref[g, c0:c0 + tq, :] = rope(kk, cosk_ref[c0 - n_ctx:c0 - n_ctx + tq, :],
                                                sink_ref[c0 - n_ctx:c0 - n_ctx + tq, :]).astype(BF16)

    lam_v = lam_ref[...]
    lam = (jnp.exp(jnp.sum(lam_v[0:1] * lam_v[1:2], axis=-1, keepdims=True))
           - jnp.exp(jnp.sum(lam_v[2:3] * lam_v[3:4], axis=-1, keepdims=True)) + lambda_init)
    dims = (((1,), (1,)), ((), ()))
    rb = ATTN_ROW_BLOCK
    units = [(g, r * rb) for g in range(n_heads) for r in range(tq // rb)]

    def scores(g, r0):
        qn = rope(head_norm(q_ref[g, r0:r0 + rb, :].astype(F32), qg_ref[...]),
                  cosq_ref[r0:r0 + rb, :], sinq_ref[r0:r0 + rb, :]) * (dh ** -0.5 * LOG2_E)
        qq = jnp.concatenate([jnp.where(comp0, qn, 0.0), jnp.where(comp0, 0.0, qn)], axis=0).astype(BF16)
        return lax.dot_general(qq, kn_ref[g], dims, preferred_element_type=F32)

    def softmax_mix(s):
        s0, s1 = s[0:rb], s[rb:2 * rb]
        e0 = jnp.exp2(s0 - jnp.max(s0, axis=-1, keepdims=True))
        e1 = jnp.exp2(s1 - jnp.max(s1, axis=-1, keepdims=True))
        l0 = jnp.sum(e0, axis=-1, keepdims=True)
        l1 = jnp.sum(e1, axis=-1, keepdims=True)
        return e0.astype(BF16) - (lam * l0 / l1).astype(BF16) * e1.astype(BF16), l0

    def values(g, r0, a, l0):
        o = jnp.dot(a, v_ref[g], preferred_element_type=F32) * (1.0 / l0)
        o_ref[r0:r0 + rb, g * DA_VDIM:(g + 1) * DA_VDIM] = (
            _rms(o) * sg_ref[...] * (1.0 - lambda_init)).astype(BF16)

    s_of, a_of = {}, {}
    for i in range(len(units) + 2):
        if i < len(units):
            s_of[i] = scores(*units[i])
        if 0 <= i - 1 < len(units):
            a_of[i - 1] = softmax_mix(s_of.pop(i - 1))
        if 0 <= i - 2 < len(units):
            values(*units[i - 2], *a_of.pop(i - 2))


def _rope_tables(n_tokens):
    rows = n_tokens // GRID_W
    row = jnp.repeat(jnp.arange(rows, dtype=F32), GRID_W)
    col = jnp.tile(jnp.arange(GRID_W, dtype=F32), rows)
    n_freq = DA_HEAD_DIM // 4
    inv_freq = jnp.exp(-math.log(ROPE_BASE) * jnp.arange(n_freq, dtype=F32) / n_freq)
    ang = jnp.concatenate([row[:, None] * inv_freq, col[:, None] * inv_freq], axis=-1)
    cos, sin = jnp.cos(ang), jnp.sin(ang)
    return jnp.tile(cos, (1, 4)), jnp.tile(jnp.concatenate([-sin, sin], axis=-1), (1, 2))


def _attention(q, k, v, n_ctx, q_gain, k_gain, lam_vecs, sub_gain, lambda_init):
    nb, nh, n_tot, dv = k.shape
    n_lat = n_tot - n_ctx
    tq = Q_TILE
    cos_t, sin_t = _rope_tables(n_lat)
    gain2 = lambda gv: jnp.tile(gv, 2).reshape(1, dv)
    lam_pad = jnp.zeros((SUBLANES, dv), F32).at[0:4, 0:DA_HEAD_DIM].set(lam_vecs)
    kern = functools.partial(_attn_kernel, n_ctx=n_ctx, lambda_init=lambda_init)
    const = lambda shape: pl.BlockSpec(shape, lambda b, h, i: (0, 0))
    hg = HEADS_PER_STEP
    return pl.pallas_call(
        kern,
        out_shape=jax.ShapeDtypeStruct((nb, n_lat, nh * dv), BF16),
        grid=(nb, nh // hg, n_lat // tq),
        in_specs=[pl.BlockSpec((None, hg, tq, dv), lambda b, h, i: (b, h, i, 0)),
                  pl.BlockSpec((None, hg, n_tot, dv), lambda b, h, i: (b, h, 0, 0)),
                  pl.BlockSpec((None, hg, n_tot, dv), lambda b, h, i: (b, h, 0, 0)),
                  const((1, dv)), const((1, dv)),
                  pl.BlockSpec((tq, dv), lambda b, h, i: (i, 0)),
                  pl.BlockSpec((tq, dv), lambda b, h, i: (i, 0)),
                  const((n_lat, dv)), const((n_lat, dv)),
                  const((SUBLANES, dv)), const((1, dv))],
        out_specs=pl.BlockSpec((None, tq, hg * dv), lambda b, h, i: (b, i, h)),
        scratch_shapes=[pltpu.VMEM((hg, n_tot, dv), BF16)],
        compiler_params=_cparams(("parallel", "parallel", "arbitrary")),
        name="attention",
    )(q, k, v, gain2(q_gain), gain2(k_gain), cos_t, sin_t, cos_t, sin_t, lam_pad, sub_gain.reshape(1, dv))


def _oproj_kernel(o_ref, w_ref, x_ref, mod_ref, out_ref):
    b = pl.program_id(0)
    y = jnp.dot(o_ref[...], w_ref[...], preferred_element_type=F32)
    out_ref[...] = x_ref[...] + mod_ref[pl.ds(b, 1), :] * y


def _oproj(o, w_o, xs, n_ctx, mod_gt):
    nb, n_lat, d = o.shape
    r = ROW_TILE
    off = n_ctx // r
    return pl.pallas_call(
        _oproj_kernel,
        out_shape=jax.ShapeDtypeStruct((nb, n_lat, d), F32),
        grid=(nb, n_lat // r),
        in_specs=[pl.BlockSpec((None, r, d), lambda b, j: (b, j, 0)),
                  pl.BlockSpec((d, d), lambda b, j: (0, 0)),
                  pl.BlockSpec((None, r, d), lambda b, j: (b, j + off, 0)),
                  pl.BlockSpec((MOD_ROWS, d), lambda b, j: (0, 0))],
        out_specs=pl.BlockSpec((None, r, d), lambda b, j: (b, j, 0)),
        compiler_params=_cparams(("parallel", "parallel")),
        name="oproj",
    )(o, w_o.astype(BF16), xs, mod_gt)


def _attn_layer(xs, n_ctx, mod, g_mix, w_qkv, w_o, q_gain, k_gain, lq1, lk1, lq2, lk2, sub_gain, lambda_init):
    d = xs.shape[-1]
    q, k, v = _qkv(xs, n_ctx, mod[:, 0:2 * d], g_mix, w_qkv)
    o = _attention(q, k, v, n_ctx, q_gain, k_gain, jnp.stack([lq1, lk1, lq2, lk2]), sub_gain, lambda_init)
    return _oproj(o, w_o, xs, n_ctx, mod[:, 2 * d:3 * d])


def kernel(x, c, ctx, c_ctx, w_ada, b_ada, g_mix, g_ffn, s5_a_re, s5_a_im, s5_log_dt, s5_b_re, s5_b_im, s5_c_re, s5_c_im, s5_d, s5_w_glu, s5_b_glu, da_w_qkv, da_w_o, da_q_gain, da_k_gain, da_lam_q1, da_lam_k1, da_lam_q2, da_lam_k2, da_sub_gain, moe_w_router, moe_b_router, moe_w_gu, moe_b_gu, moe_w_down, moe_b_down):
    nb, n_lat, d = x.shape
    n_ctx = ctx.shape[1]
    depth = w_ada.shape[0]
    assert depth == 2 and nb % SUBLANES == 0 and nb + 1 <= MOD_ROWS
    cc = jnp.concatenate([c, c_ctx[None], jnp.zeros((MOD_ROWS - nb - 1, d), F32)], axis=0)
    mod = _ada(cc, w_ada, b_ada)
    xs = _s5_layer(ctx, x, mod[0], g_mix[0], s5_a_re[0], s5_a_im[0], s5_log_dt[0], s5_b_re[0], s5_b_im[0],
                   s5_c_re[0], s5_c_im[0], s5_d[0], s5_w_glu[0], s5_b_glu[0])
    xs = _moe_layer(xs, n_ctx, mod[0][:, 3 * d:5 * d], mod[0][:, 5 * d:6 * d], g_ffn[0], 0,
                    moe_w_router, moe_b_router, moe_w_gu, moe_b_gu, moe_w_down, moe_b_down)
    lambda_init = 0.8 - 0.6 * math.exp(-0.3 * 1)
    x1 = _attn_layer(xs, n_ctx, mod[1], g_mix[1], da_w_qkv[0], da_w_o[0], da_q_gain[0], da_k_gain[0],
                     da_lam_q1[0], da_lam_k1[0], da_lam_q2[0], da_lam_k2[0], da_sub_gain[0], lambda_init)
    return _moe_layer(x1, 0, mod[1][:, 3 * d:5 * d], mod[1][:, 5 * d:6 * d], g_ffn[1], 1,
                      moe_w_router, moe_b_router, moe_w_gu, moe_b_gu, moe_w_down, moe_b_down)
```

```python
import functools
import math

import jax
import jax.numpy as jnp
from jax import lax
from jax.experimental import pallas as pl
from jax.experimental.pallas import tpu as pltpu

NORM_EPS = 1e-6
N_MIXERS = 2
S5_GROUP = 16
S5_STATE = 64
DA_HEAD_DIM = 64
DA_VDIM = 2 * DA_HEAD_DIM
GRID_W = 64
ROPE_BASE = 10000.0
N_EXPERTS = 32
TOP_K = 4
SWIGLU_LIMIT = 7.0
SWIGLU_ALPHA = 1.702
LOG2_E = 1.4426950408889634

LANES = 128
SUBLANES = 8
VMEM_LIMIT_BYTES = 56 * 1024 * 1024

S5_SLAB_GROUPS = LANES // S5_GROUP
S5_SLAB_STATE = S5_SLAB_GROUPS * S5_STATE
S5_CHUNK = 64
S5_ROW_BLOCKS = 4
TM_TILE = 32
ROW_TILE = 256
MOE_TILE = 512
ROUTER_TILE = 512
EXPERT_ROWS = 512
COMBINE_TILE = 256
Q_TILE = 512
ATTN_ROW_BLOCK = 128
HEADS_PER_STEP = 2
MOD_ROWS = 24

F32 = jnp.float32
BF16 = jnp.bfloat16


def _cparams(sem, vmem=VMEM_LIMIT_BYTES):
    return pltpu.CompilerParams(dimension_semantics=sem, vmem_limit_bytes=vmem)


def _rms(x):
    return x * lax.rsqrt(jnp.mean(x * x, axis=-1, keepdims=True) + NORM_EPS)


def _ada_kernel(c_ref, w_ref, b_ref, o_ref):
    cv = c_ref[...]
    s = cv * jax.nn.sigmoid(cv)
    o_ref[...] = jnp.dot(s, w_ref[...], precision=lax.Precision.HIGHEST,
                         preferred_element_type=F32) + b_ref[...]


def _ada(cc, w_ada, b_ada):
    depth, d, d6 = w_ada.shape
    nj = d6 // d
    return pl.pallas_call(
        _ada_kernel,
        out_shape=jax.ShapeDtypeStruct((depth, MOD_ROWS, d6), F32),
        grid=(depth, nj),
        in_specs=[pl.BlockSpec((MOD_ROWS, d), lambda i, j: (0, 0)),
                  pl.BlockSpec((None, d, d), lambda i, j: (i, 0, j)),
                  pl.BlockSpec((None, 1, d), lambda i, j: (i, 0, j))],
        out_specs=pl.BlockSpec((None, MOD_ROWS, d), lambda i, j: (i, 0, j)),
        compiler_params=_cparams(("parallel", "parallel")),
        name="ada",
    )(cc, w_ada, b_ada.reshape(depth, 1, d6))


def _prenorm_tm_kernel(ctx_ref, x_ref, mod_ref, g_ref, o_ref, *, n_ctx_tiles, nb, d):
    i = pl.program_id(0)
    tt = o_ref.shape[1] // nb
    g = g_ref[...]

    def emit(src_ref, mod_row):
        for b in range(nb):
            r = mod_row(b)
            sh = mod_ref[r:r + 1, 0:d]
            sc = mod_ref[r:r + 1, d:2 * d]
            h = (_rms(src_ref[b]) * g) * (1.0 + sc) + sh
            for s in range(d // LANES):
                o_ref[s, pl.ds(b, tt, stride=nb), :] = h[:, s * LANES:(s + 1) * LANES]

    @pl.when(i < n_ctx_tiles)
    def _():
        emit(ctx_ref, lambda b: nb)

    @pl.when(i >= n_ctx_tiles)
    def _():
        emit(x_ref, lambda b: b)


def _prenorm_tm(ctx, x, mod, g):
    nb, n_ctx, d = ctx.shape
    n_lat = x.shape[1]
    tt = TM_TILE
    nct, nlt = n_ctx // tt, n_lat // tt
    kern = functools.partial(_prenorm_tm_kernel, n_ctx_tiles=nct, nb=nb, d=d)
    return pl.pallas_call(
        kern,
        out_shape=jax.ShapeDtypeStruct((d // LANES, (n_ctx + n_lat) * nb, LANES), F32),
        grid=(nct + nlt,),
        in_specs=[pl.BlockSpec((nb, tt, d), lambda i: (0, jnp.minimum(i, nct - 1), 0)),
                  pl.BlockSpec((nb, tt, d), lambda i: (0, jnp.maximum(i - nct, 0), 0)),
                  pl.BlockSpec((MOD_ROWS, 2 * d), lambda i: (0, 0)),
                  pl.BlockSpec((1, d), lambda i: (0, 0))],
        out_specs=pl.BlockSpec((d // LANES, tt * nb, LANES), lambda i: (0, i, 0)),
        compiler_params=_cparams(("parallel",)),
        name="prenorm_tm",
    )(ctx, x, mod, g)


def _s5_disc_kernel(ar_ref, ai_ref, ldt_ref, br_ref, bi_ref, lr_ref, li_ref, bbr_ref, bbi_ref):
    ar, ai = ar_ref[...], ai_ref[...]
    dt = jnp.exp(ldt_ref[...])
    mag = jnp.exp(ar * dt)
    lr = mag * jnp.cos(ai * dt)
    li = mag * jnp.sin(ai * dt)
    nr, ni = lr - 1.0, li
    den = ar * ar + ai * ai
    qr = (nr * ar + ni * ai) / den
    qi = (ni * ar - nr * ai) / den
    br, bi = br_ref[...], bi_ref[...]
    lr_ref[...] = lr
    li_ref[...] = li
    bbr_ref[...] = qr * br - qi * bi
    bbi_ref[...] = qr * bi + qi * br


def _s5_disc(a_re, a_im, log_dt, b_re, b_im):
    nd, g, p = a_re.shape
    cg = b_re.shape[-1]
    gp = g * p
    col = lambda a: a.reshape(nd, gp, 1)
    ldt = jnp.broadcast_to(log_dt[:, :, None], (nd, g, p))
    spec1 = pl.BlockSpec((None, gp, 1), lambda i: (i, 0, 0))
    specb = pl.BlockSpec((None, gp, cg), lambda i: (i, 0, 0))
    lr, li, bbr, bbi = pl.pallas_call(
        _s5_disc_kernel,
        out_shape=[jax.ShapeDtypeStruct((nd, gp, 1), F32)] * 2 + [jax.ShapeDtypeStruct((nd, gp, cg), F32)] * 2,
        grid=(nd,),
        in_specs=[spec1, spec1, spec1, specb, specb],
        out_specs=[spec1, spec1, specb, specb],
        compiler_params=_cparams(("parallel",)),
        name="s5_disc",
    )(col(a_re), col(a_im), col(ldt), b_re.reshape(nd, gp, cg), b_im.reshape(nd, gp, cg))
    return (lr.reshape(nd, g, p), li.reshape(nd, g, p),
            bbr.reshape(nd, g, p, cg), bbi.reshape(nd, g, p, cg))


def _s5_slab_params(lam_re, lam_im, bb_re, bb_im, c_re, c_im):
    nd, g, p = lam_re.shape
    cg = bb_re.shape[-1]
    ns, sg = g // S5_SLAB_GROUPS, S5_SLAB_GROUPS
    eye = jnp.eye(sg, dtype=F32)

    def b_mat(bb):
        bb = bb.reshape(nd, ns, sg, p, cg)
        return jnp.einsum('dsjpc,jk->dsjckp', bb, eye).reshape(nd, ns, sg * cg, sg * p)

    def c_mat(cm):
        cm = cm.reshape(nd, ns, sg, cg, p)
        return jnp.einsum('dsjcp,jk->dskpjc', cm, eye).reshape(nd, ns, sg * p, sg * cg)

    bm = jnp.concatenate([b_mat(bb_re), b_mat(bb_im)], axis=-1).astype(BF16)
    lam = lambda a: a.reshape(nd, ns, 1, sg * p)
    return bm, c_mat(c_re).astype(BF16), c_mat(c_im).astype(BF16), lam(lam_re), lam(lam_im)


def _s5_scan_kernel(u_ref, bm_ref, cr_ref, ci_ref, lr_ref, li_ref, y_ref, x0_ref, x1_ref, h0_ref, h1_ref, st_ref):
    d = pl.program_id(0)
    c = pl.program_id(2)
    t_len, nb = u_ref.shape[0], u_ref.shape[1] * u_ref.shape[2]
    ns = S5_SLAB_STATE
    xs, hs = (x0_ref, x1_ref), (h0_ref, h1_ref)

    @pl.when(c == 0)
    def _():
        for ref in xs + hs:
            ref[...] = jnp.zeros_like(ref)

    def stages(in_ref, bu_ref, h_ref, out_ref):
        lr = jnp.broadcast_to(lr_ref[...], (SUBLANES, ns))
        li = jnp.broadcast_to(li_ref[...], (SUBLANES, ns))
        fresh = c <= 1
        state = [(jnp.where(fresh, 0.0, st_ref[0, o]), jnp.where(fresh, 0.0, st_ref[1, o]))
                 for o in range(nb // SUBLANES)]
        tb = t_len // S5_ROW_BLOCKS
        for j in range(S5_ROW_BLOCKS):
            rsl = slice(j * tb * nb, (j + 1) * tb * nb)
            u = u_ref[j * tb:(j + 1) * tb].reshape(tb * nb, LANES).astype(BF16)
            in_ref[rsl, :] = jnp.dot(u, bm_ref[...], preferred_element_type=F32)

            for s in range(j * tb, (j + 1) * tb):
                t = jnp.where(d == 0, s, t_len - 1 - s)
                row = pl.multiple_of(t * nb, nb)
                for o in range(nb // SUBLANES):
                    hr, hi = state[o]
                    br = bu_ref[pl.ds(row + o * SUBLANES, SUBLANES), 0:ns]
                    bi = bu_ref[pl.ds(row + o * SUBLANES, SUBLANES), ns:2 * ns]
                    state[o] = (lr * hr - li * hi + br, lr * hi + li * hr + bi)
                h_ref[pl.ds(row, nb), 0:ns] = jnp.concatenate([st[0] for st in state], axis=0).astype(BF16)
                h_ref[pl.ds(row, nb), ns:2 * ns] = jnp.concatenate([st[1] for st in state], axis=0).astype(BF16)

            y = (jnp.dot(out_ref[rsl, 0:ns], cr_ref[...], preferred_element_type=F32)
                 - jnp.dot(out_ref[rsl, ns:2 * ns], ci_ref[...], preferred_element_type=F32))
            y_ref[j * tb:(j + 1) * tb] = y.reshape((tb,) + y_ref.shape[1:])
        for o in range(nb // SUBLANES):
            st_ref[0, o] = state[o][0]
            st_ref[1, o] = state[o][1]

    phase = lax.rem(c, 2)
    for r in range(2):
        @pl.when(phase == r)
        def _(r=r):
            stages(xs[r], xs[1 - r], hs[1 - r], hs[r])


def _s5_scan(u_t, bm, cr, ci, lam_r, lam_i, nb, n_ctx, n_tot):
    nslab = u_t.shape[0]
    nbo = nb // SUBLANES
    t_len = S5_CHUNK
    ncc, nc = n_ctx // t_len, n_tot // t_len
    u5 = u_t.reshape(nslab, n_tot, nbo, SUBLANES, LANES)

    def chunk(d, c):
        c = jnp.clip(c, 0, nc - 1)
        back = jnp.where(c < ncc, ncc - 1 - c, ncc + nc - 1 - c)
        return jnp.where(d == 0, c, back)

    pspec = lambda r, k: pl.BlockSpec((None, None, r, k), lambda d, s, c: (d, s, 0, 0))
    rows = t_len * nb
    return pl.pallas_call(
        _s5_scan_kernel,
        out_shape=jax.ShapeDtypeStruct((2, nslab, n_tot, nbo, SUBLANES, LANES), F32),
        grid=(2, nslab, nc + 2),
        in_specs=[pl.BlockSpec((None, t_len, nbo, SUBLANES, LANES), lambda d, s, c: (s, chunk(d, c), 0, 0, 0)),
                  pspec(LANES, 2 * S5_SLAB_STATE), pspec(S5_SLAB_STATE, LANES), pspec(S5_SLAB_STATE, LANES),
                  pspec(1, S5_SLAB_STATE), pspec(1, S5_SLAB_STATE)],
        out_specs=pl.BlockSpec((None, None, t_len, nbo, SUBLANES, LANES),
                               lambda d, s, c: (d, s, chunk(d, c - 2), 0, 0, 0)),
        scratch_shapes=[pltpu.VMEM((rows, 2 * S5_SLAB_STATE), F32)] * 2
                       + [pltpu.VMEM((rows, 2 * S5_SLAB_STATE), BF16)] * 2
                       + [pltpu.VMEM((2, nbo, SUBLANES, S5_SLAB_STATE), F32)],
        compiler_params=_cparams(("parallel", "parallel", "arbitrary")),
        name="s5_scan",
    )(u5, bm, cr, ci, lam_r, lam_i)


def _s5_out_kernel(y_ref, u_ref, dsk_ref, w_ref, bg_ref, ctx_ref, x_ref, mod_ref, o_ref, tr_ref,
                   *, n_ctx_tiles, nb, d):
    i = pl.program_id(0)
    nslab = d // LANES
    rows = u_ref.shape[1]
    tt = rows // nb
    y = (y_ref[0] + y_ref[1]).reshape(nslab, rows, LANES)
    v = jax.nn.gelu(y + dsk_ref[...] * u_ref[...])
    lhs = jnp.concatenate([v[s] for s in range(nslab)], axis=-1).astype(BF16)
    z = jnp.dot(lhs, w_ref[...], preferred_element_type=F32) + bg_ref[...]
    m = z[:, 0:d] * jax.nn.sigmoid(z[:, d:2 * d])
    for s in range(nslab):
        tr_ref[s] = m[:, s * LANES:(s + 1) * LANES]

    def emit(src_ref, mod_row):
        for b in range(nb):
            r = mod_row(b)
            gt = mod_ref[r:r + 1, :]
            mb = jnp.concatenate([tr_ref[s, pl.ds(b, tt, stride=nb), :] for s in range(nslab)], axis=-1)
            o_ref[b] = src_ref[b] + gt * mb

    @pl.when(i < n_ctx_tiles)
    def _():
        emit(ctx_ref, lambda b: nb)

    @pl.when(i >= n_ctx_tiles)
    def _():
        emit(x_ref, lambda b: b)


def _s5_out(y_t, u_t, d_skip, w_glu, b_glu, ctx, x, gt1):
    nb, n_ctx, d = ctx.shape
    n_lat = x.shape[1]
    n_tot = n_ctx + n_lat
    nslab, nbo = d // LANES, nb // SUBLANES
    tt = TM_TILE
    nct, nlt = n_ctx // tt, n_lat // tt
    kern = functools.partial(_s5_out_kernel, n_ctx_tiles=nct, nb=nb, d=d)
    return pl.pallas_call(
        kern,
        out_shape=jax.ShapeDtypeStruct((nb, n_tot, d), F32),
        grid=(nct + nlt,),
        in_specs=[pl.BlockSpec((2, nslab, tt, nbo, SUBLANES, LANES), lambda i: (0, 0, i, 0, 0, 0)),
                  pl.BlockSpec((nslab, tt * nb, LANES), lambda i: (0, i, 0)),
                  pl.BlockSpec((nslab, 1, LANES), lambda i: (0, 0, 0)),
                  pl.BlockSpec((d, 2 * d), lambda i: (0, 0)),
                  pl.BlockSpec((1, 2 * d), lambda i: (0, 0)),
                  pl.BlockSpec((nb, tt, d), lambda i: (0, jnp.minimum(i, nct - 1), 0)),
                  pl.BlockSpec((nb, tt, d), lambda i: (0, jnp.maximum(i - nct, 0), 0)),
                  pl.BlockSpec((MOD_ROWS, d), lambda i: (0, 0))],
        out_specs=pl.BlockSpec((nb, tt, d), lambda i: (0, i, 0)),
        scratch_shapes=[pltpu.VMEM((nslab, tt * nb, LANES), F32)],
        compiler_params=_cparams(("parallel",)),
        name="s5_out",
    )(y_t, u_t, d_skip.reshape(nslab, 1, LANES), w_glu.astype(BF16), b_glu.reshape(1, 2 * d), ctx, x, gt1)


def _s5_layer(xs_ctx, xs_lat, mod, g_mix, a_re, a_im, log_dt, b_re, b_im, c_re, c_im, d_skip, w_glu, b_glu):
    nb, n_ctx, d = xs_ctx.shape
    n_tot = n_ctx + xs_lat.shape[1]
    u_t = _prenorm_tm(xs_ctx, xs_lat, mod[:, 0:2 * d], g_mix.reshape(1, d))
    lam_re, lam_im, bb_re, bb_im = _s5_disc(a_re, a_im, log_dt, b_re, b_im)
    bm, cr, ci, lam_r, lam_i = _s5_slab_params(lam_re, lam_im, bb_re, bb_im, c_re, c_im)
    y_t = _s5_scan(u_t, bm, cr, ci, lam_r, lam_i, nb, n_ctx, n_tot)
    return _s5_out(y_t, u_t, d_skip, w_glu, b_glu, xs_ctx, xs_lat, mod[:, 2 * d:3 * d])


NEG_PAD = -1e30
NEG_MASK = -3e38


def _split_bf16(a):
    hi = a.astype(BF16)
    return hi, (a - hi.astype(F32)).astype(BF16)


def _mod_row(piece, pieces_per_batch, n_ctx_pieces, nb):
    b = piece // pieces_per_batch
    return jnp.where(piece - b * pieces_per_batch < n_ctx_pieces, nb, b)


def _router_kernel(x_ref, mod_ref, g_ref, w_ref, bias_ref, h_ref, sel_ref, selt_ref, meta_ref, cnt_ref,
                   carry_ref, *, pieces_per_batch, n_ctx_pieces, nb, d):
    i = pl.program_id(0)
    rows = x_ref.shape[0]

    @pl.when(i == 0)
    def _():
        carry_ref[...] = jnp.zeros_like(carry_ref)

    parts = []
    for p in range(rows // ROW_TILE):
        row = _mod_row(i * (rows // ROW_TILE) + p, pieces_per_batch, n_ctx_pieces, nb)
        mrow = mod_ref[pl.ds(row, 1), :]
        xp = x_ref[p * ROW_TILE:(p + 1) * ROW_TILE, :]
        parts.append((_rms(xp) * g_ref[...]) * (1.0 + mrow[:, d:2 * d]) + mrow[:, 0:d])
    h = jnp.concatenate(parts, axis=0)
    h_ref[...] = h.astype(BF16)

    hi, lo = _split_bf16(h)
    whi, wlo = _split_bf16(w_ref[...])
    dot = functools.partial(jnp.dot, preferred_element_type=F32)
    logits = dot(hi, whi) + dot(lo, whi) + dot(hi, wlo) + bias_ref[...]

    lane = lax.broadcasted_iota(jnp.int32, (rows, LANES), 1)
    lane_f = lane.astype(F32)
    work = logits
    tops, idxs, hots = [], [], []
    for _ in range(TOP_K):
        m = jnp.max(work, axis=-1, keepdims=True)
        idx = jnp.min(jnp.where(work == m, lane_f, float(LANES)), axis=-1, keepdims=True)
        hot = lane_f == idx
        work = jnp.where(hot, NEG_MASK, work)
        tops.append(m)
        idxs.append(idx)
        hots.append(hot)
    exps = [jnp.exp(m - tops[0]) for m in tops]
    denom = exps[0] + exps[1] + exps[2] + exps[3]

    multi = jnp.zeros((rows, LANES), F32)
    for hot in hots:
        multi = multi + jnp.where(hot, 1.0, 0.0)
    r_i = lax.broadcasted_iota(jnp.int32, (rows, rows), 0)
    c_i = lax.broadcasted_iota(jnp.int32, (rows, rows), 1)
    tri = jnp.where(c_i < r_i, 1.0, 0.0).astype(BF16)
    in_tile = dot(tri, multi.astype(BF16))
    cnt_tile = jnp.sum(multi, axis=0, keepdims=True)
    e_r = lax.broadcasted_iota(jnp.int32, (LANES, LANES), 0)
    e_c = lax.broadcasted_iota(jnp.int32, (LANES, LANES), 1)
    below = jnp.where(e_r < e_c, 1.0, 0.0).astype(BF16)
    c_hi, c_lo = _split_bf16(jnp.broadcast_to(cnt_tile, (SUBLANES, LANES)))
    prefix = (dot(c_hi, below) + dot(c_lo, below))[0:1, :]
    slot = in_tile + prefix

    sel = jnp.zeros((rows, LANES), F32)
    for k in range(TOP_K):
        pos = jnp.sum(jnp.where(hots[k], slot, 0.0), axis=-1, keepdims=True)
        sel = jnp.where(lane == k, idxs[k].astype(F32), sel)
        sel = jnp.where(lane == TOP_K + k, exps[k] / denom, sel)
        sel = jnp.where(lane == 2 * TOP_K + k, pos, sel)
    sel_ref[...] = sel
    selt_ref[...] = sel.T[2 * TOP_K:2 * TOP_K + SUBLANES, :].astype(jnp.int32)

    carry = carry_ref[0:1, :]
    sub = lax.broadcasted_iota(jnp.int32, (SUBLANES, LANES), 0)
    meta = jnp.where(sub == 0, cnt_tile, jnp.where(sub == 1, carry, jnp.where(sub == 2, prefix, 0.0)))
    meta_ref[...] = meta.astype(jnp.int32)
    carry_ref[0:1, :] = carry + cnt_tile
    cnt_ref[...] = jnp.broadcast_to(carry + cnt_tile, cnt_ref.shape).astype(jnp.int32)


def _router(xs, n_ctx, mod_shsc, g, w_router, b_router):
    nb, t_len, d = xs.shape
    r = MOE_TILE
    n_tok = nb * t_len
    nt = n_tok // r
    ne = w_router.shape[1]
    w_pad = jnp.zeros((d, LANES), F32).at[:, :ne].set(w_router)
    b_pad = jnp.full((1, LANES), NEG_PAD, F32).at[0, :ne].set(b_router)
    kern = functools.partial(_router_kernel, pieces_per_batch=t_len // ROW_TILE,
                             n_ctx_pieces=n_ctx // ROW_TILE, nb=nb, d=d)
    tile = lambda i: (i, 0)
    const = lambda i: (0, 0)
    return pl.pallas_call(
        kern,
        out_shape=[jax.ShapeDtypeStruct((n_tok, d), BF16),
                   jax.ShapeDtypeStruct((n_tok, LANES), F32),
                   jax.ShapeDtypeStruct((nt * SUBLANES, r), jnp.int32),
                   jax.ShapeDtypeStruct((nt, SUBLANES, LANES), jnp.int32),
                   jax.ShapeDtypeStruct((SUBLANES, LANES), jnp.int32)],
        grid=(nt,),
        in_specs=[pl.BlockSpec((r, d), tile),
                  pl.BlockSpec((MOD_ROWS, 2 * d), const),
                  pl.BlockSpec((1, d), const),
                  pl.BlockSpec((d, LANES), const),
                  pl.BlockSpec((1, LANES), const)],
        out_specs=[pl.BlockSpec((r, d), tile),
                   pl.BlockSpec((r, LANES), tile),
                   pl.BlockSpec((SUBLANES, r), tile),
                   pl.BlockSpec((None, SUBLANES, LANES), lambda i: (i, 0, 0)),
                   pl.BlockSpec((SUBLANES, LANES), const)],
        scratch_shapes=[pltpu.VMEM((SUBLANES, LANES), F32)],
        compiler_params=_cparams(("arbitrary",)),
        name="router",
    )(xs.reshape(n_tok, d), mod_shsc, g.reshape(1, d), w_pad, b_pad)


def _plan_kernel(cnt_ref, ps_ref, be_ref, first_ref, nused_ref, *, ne, rb):
    n_blocks = be_ref.shape[0]

    def fill(i, carry):
        be_ref[i] = ne - 1
        first_ref[i] = 0
        return carry
    lax.fori_loop(0, n_blocks, fill, 0)

    def per_expert(e, acc):
        nblk = (cnt_ref[0, e] + (rb - 1)) // rb
        ps_ref[e] = acc
        b0 = acc // rb

        def mark(i, carry):
            be_ref[b0 + i] = e
            first_ref[b0 + i] = jnp.where(i == 0, 1, 0)
            return carry
        lax.fori_loop(0, nblk, mark, 0)
        return acc + nblk * rb

    total = lax.fori_loop(0, ne, per_expert, 0)

    def tail(e, carry):
        ps_ref[e] = total
        return carry
    lax.fori_loop(ne, ps_ref.shape[0], tail, 0)
    nused_ref[0] = total // rb


def _plan(cnt, ne, n_blocks):
    smem = pl.BlockSpec(memory_space=pltpu.SMEM)
    return pl.pallas_call(
        functools.partial(_plan_kernel, ne=ne, rb=EXPERT_ROWS),
        out_shape=[jax.ShapeDtypeStruct((LANES,), jnp.int32),
                   jax.ShapeDtypeStruct((n_blocks,), jnp.int32),
                   jax.ShapeDtypeStruct((n_blocks,), jnp.int32),
                   jax.ShapeDtypeStruct((1,), jnp.int32)],
        in_specs=[smem],
        out_specs=[smem, smem, smem, smem],
        name="plan",
    )(cnt)


_SEG_SIZES = tuple(MOE_TILE >> s for s in range(MOE_TILE.bit_length()))
TILE_SLOTS = MOE_TILE * TOP_K
TOKEN_ROWS = SUBLANES


def _segment_copies(n, src, dst, make_copy, wait=False):
    for k, p in enumerate(_SEG_SIZES):
        take = (n & p) != 0

        @pl.when(take)
        def _():
            cp = make_copy(pl.multiple_of(src * TOKEN_ROWS, TOKEN_ROWS),
                           pl.multiple_of(dst * TOKEN_ROWS, TOKEN_ROWS), p * TOKEN_ROWS)
            cp.wait() if wait else cp.start(priority=k % 2)
        step = jnp.where(take, p, 0)
        src = src + step
        dst = dst + step


def _tile_major(x):
    return [x[:, s * LANES:(s + 1) * LANES] for s in range(x.shape[1] // LANES)]


def _dispatch_kernel(meta_ref, metap_ref, ps_ref, cnt_ref, h_ref, selt_ref, xbuf_ref, xs_ref, zero_ref, sem,
                     zsem, *, ne, rb):
    rows = h_ref.shape[0]
    slots = xs_ref.shape[1] // TOKEN_ROWS

    @pl.when(pl.program_id(0) == 0)
    def _():
        zero_ref[...] = jnp.zeros_like(zero_ref)

        def pad_copy(_, dst, p):
            return pltpu.make_async_copy(zero_ref.at[pl.ds(0, p), :], xbuf_ref.at[pl.ds(dst, p), :], zsem)

        for wait in (False, True):
            def pad_rows(e, carry, wait=wait):
                cnt = cnt_ref[0, e]
                _segment_copies((-cnt) & (rb - 1), 0, ps_ref[e] + cnt, pad_copy, wait)
                return carry
            lax.fori_loop(0, ne, pad_rows, 0)

    pos = selt_ref[0:TOP_K, :]
    s_iota = lax.broadcasted_iota(jnp.int32, (slots, rows), 0)
    perm = jnp.zeros((slots, rows), F32)
    for k in range(TOP_K):
        perm = perm + jnp.where(s_iota == pos[k:k + 1, :], 1.0, 0.0)
    step = pl.program_id(0)
    slot = lax.rem(step, 2)
    sorted_rows = jnp.dot(perm.astype(BF16), h_ref[...], preferred_element_type=F32)
    for s, slab in enumerate(_tile_major(sorted_rows)):
        xs_ref[slot, pl.ds(s, slots, stride=TOKEN_ROWS), :] = slab

    def segments(mref, which, wait):
        def seg_copy(src, dst, p):
            return pltpu.make_async_copy(xs_ref.at[which, pl.ds(src, p), :], xbuf_ref.at[pl.ds(dst, p), :],
                                         sem.at[which])

        def per_expert(e, carry):
            _segment_copies(mref[0, e], mref[2, e], ps_ref[e] + mref[1, e], seg_copy, wait)
            return carry
        lax.fori_loop(0, ne, per_expert, 0)

    segments(meta_ref, slot, False)

    @pl.when(step > 0)
    def _():
        segments(metap_ref, 1 - slot, True)

    @pl.when(step == pl.num_programs(0) - 1)
    def _():
        segments(meta_ref, slot, True)


def _dispatch(h2, selt, meta, ps, cnt, ne, n_buf):
    n_tok, d = h2.shape
    r = MOE_TILE
    smem = pl.BlockSpec(memory_space=pltpu.SMEM)
    return pl.pallas_call(
        functools.partial(_dispatch_kernel, ne=ne, rb=EXPERT_ROWS),
        out_shape=jax.ShapeDtypeStruct((n_buf * TOKEN_ROWS, LANES), F32),
        grid=(n_tok // r,),
        in_specs=[pl.BlockSpec((None, SUBLANES, LANES), lambda i: (i, 0, 0), memory_space=pltpu.SMEM),
                  pl.BlockSpec((None, SUBLANES, LANES), lambda i: (jnp.maximum(i - 1, 0), 0, 0),
                               memory_space=pltpu.SMEM),
                  smem, smem,
                  pl.BlockSpec((r, d), lambda i: (i, 0)),
                  pl.BlockSpec((SUBLANES, r), lambda i: (i, 0))],
        out_specs=pl.BlockSpec(memory_space=pl.ANY),
        scratch_shapes=[pltpu.VMEM((2, TILE_SLOTS * TOKEN_ROWS, LANES), F32),
                        pltpu.VMEM((_SEG_SIZES[0] * TOKEN_ROWS, LANES), F32),
                        pltpu.SemaphoreType.DMA((2,)), pltpu.SemaphoreType.DMA(())],
        compiler_params=_cparams(("arbitrary",)),
        name="dispatch",
    )(meta, meta, ps, cnt, h2, selt)


def _expert_kernel(be_ref, first_ref, nused_ref, x_ref, wgu_ref, bgu_ref, wd_ref, bd_ref, y_ref,
                   wgu_bf, wd_bf):
    i = pl.program_id(0)
    dff = wd_ref.shape[0]

    @pl.when(i < nused_ref[0])
    def _():
        @pl.when(first_ref[i] == 1)
        def _():
            wgu_bf[...] = wgu_ref[...].astype(BF16)
            wd_bf[...] = wd_ref[...].astype(BF16)

        rows = x_ref.shape[0] // TOKEN_ROWS
        x = jnp.concatenate([x_ref[pl.ds(s, rows, stride=TOKEN_ROWS), :] for s in range(TOKEN_ROWS)], axis=-1)
        gu = jnp.dot(x.astype(BF16), wgu_bf[...], preferred_element_type=F32) + bgu_ref[...]
        gate = jnp.minimum(gu[:, 0:dff], SWIGLU_LIMIT)
        up = jnp.clip(gu[:, dff:2 * dff], -SWIGLU_LIMIT, SWIGLU_LIMIT)
        act = (up + 1.0) * gate * jax.nn.sigmoid(SWIGLU_ALPHA * gate)
        y = jnp.dot(act.astype(BF16), wd_bf[...], preferred_element_type=F32) + bd_ref[...]
        for s, slab in enumerate(_tile_major(y)):
            y_ref[pl.ds(s, rows, stride=TOKEN_ROWS), :] = slab

    @pl.when(i >= nused_ref[0])
    def _():
        y_ref[...] = jnp.zeros_like(y_ref)


def _experts(xbuf, block_e, first, n_used, layer, w_gu, b_gu, w_down, b_down):
    depth, ne, d, dff2 = w_gu.shape
    assert d == TOKEN_ROWS * LANES
    dff = dff2 // 2
    rb = EXPERT_ROWS * TOKEN_ROWS
    blk = lambda i, nu: jnp.minimum(i, nu[0] - 1)
    wmap = lambda i, be, fi, nu: (layer, be[blk(i, nu)], 0, 0)
    grid_spec = pltpu.PrefetchScalarGridSpec(
        num_scalar_prefetch=3,
        grid=(xbuf.shape[0] // rb,),
        in_specs=[pl.BlockSpec((rb, LANES), lambda i, be, fi, nu: (blk(i, nu), 0)),
                  pl.BlockSpec((None, None, d, dff2), wmap),
                  pl.BlockSpec((None, None, 1, dff2), wmap),
                  pl.BlockSpec((None, None, dff, d), wmap),
                  pl.BlockSpec((None, None, 1, d), wmap)],
        out_specs=pl.BlockSpec((rb, LANES), lambda i, be, fi, nu: (i, 0)),
        scratch_shapes=[pltpu.VMEM((d, dff2), BF16), pltpu.VMEM((dff, d), BF16)])
    return pl.pallas_call(
        _expert_kernel,
        out_shape=jax.ShapeDtypeStruct(xbuf.shape, F32),
        grid_spec=grid_spec,
        compiler_params=_cparams(("arbitrary",)),
        name="experts",
    )(block_e, first, n_used, xbuf, w_gu, b_gu.reshape(depth, ne, 1, dff2), w_down,
      b_down.reshape(depth, ne, 1, d))


def _combine_kernel(meta_ref, metan_ref, ps_ref, sel_ref, x_ref, mod_ref, ybuf_ref, o_ref, ys_ref, sem,
                    *, pieces_per_batch, n_ctx_pieces, nb, ne):
    step = pl.program_id(0)
    n_steps = pl.num_programs(0)
    slot = lax.rem(step, 2)
    rows = x_ref.shape[0]
    slots = ys_ref.shape[1] // TOKEN_ROWS

    def segments(mref, to_slot, wait):
        def seg_copy(src, dst, p):
            return pltpu.make_async_copy(ybuf_ref.at[pl.ds(dst, p), :], ys_ref.at[to_slot, pl.ds(src, p), :],
                                         sem.at[to_slot])

        def per_expert(e, carry):
            _segment_copies(mref[0, e], mref[2, e], ps_ref[e] + mref[1, e], seg_copy, wait)
            return carry
        lax.fori_loop(0, ne, per_expert, 0)

    @pl.when(step == 0)
    def _():
        segments(meta_ref, 0, False)

    @pl.when(step + 1 < n_steps)
    def _():
        segments(metan_ref, 1 - slot, False)

    segments(meta_ref, slot, True)

    sel = sel_ref[...]
    l_iota = lax.broadcasted_iota(jnp.int32, (rows, slots), 1).astype(F32)
    w = jnp.zeros((rows, slots), F32)
    for k in range(TOP_K):
        w = w + jnp.where(l_iota == sel[:, 2 * TOP_K + k:2 * TOP_K + k + 1], sel[:, TOP_K + k:TOP_K + k + 1], 0.0)
    ys = jnp.concatenate([ys_ref[slot, pl.ds(s, slots, stride=TOKEN_ROWS), :] for s in range(TOKEN_ROWS)],
                         axis=-1).astype(BF16)
    acc = jnp.dot(w.astype(BF16), ys, preferred_element_type=F32)
    for p in range(rows // ROW_TILE):
        row = _mod_row(step * (rows // ROW_TILE) + p, pieces_per_batch, n_ctx_pieces, nb)
        psl = slice(p * ROW_TILE, (p + 1) * ROW_TILE)
        o_ref[psl, :] = x_ref[psl, :] + mod_ref[pl.ds(row, 1), :] * acc[psl, :]


def _combine(ybuf, meta, ps, sel, xs, n_ctx, mod_gt, ne):
    nb, t_len, d = xs.shape
    r = MOE_TILE
    n_tok = nb * t_len
    nt = n_tok // r
    kern = functools.partial(_combine_kernel, pieces_per_batch=t_len // ROW_TILE,
                             n_ctx_pieces=n_ctx // ROW_TILE, nb=nb, ne=ne)
    mspec = lambda f: pl.BlockSpec((None, SUBLANES, LANES), f, memory_space=pltpu.SMEM)
    out = pl.pallas_call(
        kern,
        out_shape=jax.ShapeDtypeStruct((n_tok, d), F32),
        grid=(nt,),
        in_specs=[mspec(lambda i: (i, 0, 0)),
                  mspec(lambda i: (jnp.minimum(i + 1, nt - 1), 0, 0)),
                  pl.BlockSpec(memory_space=pltpu.SMEM),
                  pl.BlockSpec((r, LANES), lambda i: (i, 0)),
                  pl.BlockSpec((r, d), lambda i: (i, 0)),
                  pl.BlockSpec((MOD_ROWS, d), lambda i: (0, 0)),
                  pl.BlockSpec(memory_space=pl.ANY)],
        out_specs=pl.BlockSpec((r, d), lambda i: (i, 0)),
        scratch_shapes=[pltpu.VMEM((2, TILE_SLOTS * TOKEN_ROWS, LANES), F32), pltpu.SemaphoreType.DMA((2,))],
        compiler_params=_cparams(("arbitrary",)),
        name="combine",
    )(meta, meta, ps, sel, xs.reshape(n_tok, d), mod_gt, ybuf)
    return out.reshape(nb, t_len, d)


def _moe_layer(xs, n_ctx, mod_shsc, mod_gt, g, layer, w_router, b_router, w_gu, b_gu, w_down, b_down):
    nb, t_len, d = xs.shape
    ne = w_router.shape[-1]
    n_tiles = nb * t_len // MOE_TILE
    n_blocks = n_tiles * TILE_SLOTS // EXPERT_ROWS + ne
    h2, sel, selt, meta, cnt = _router(xs, n_ctx, mod_shsc, g, w_router[layer], b_router[layer])
    ps, block_e, first, n_used = _plan(cnt, ne, n_blocks)
    xbuf = _dispatch(h2, selt, meta, ps, cnt, ne, n_blocks * EXPERT_ROWS)
    ybuf = _experts(xbuf, block_e, first, n_used, layer, w_gu, b_gu, w_down, b_down)
    return _combine(ybuf, meta, ps, sel, xs, n_ctx, mod_gt, ne)


def _qkv_kernel(x_ref, mod_ref, g_ref, w_ref, q_ref, k_ref, v_ref, *, n_ctx_tiles, nb, d):
    b = pl.program_id(0)
    j = pl.program_id(1)
    row = jnp.where(j < n_ctx_tiles, nb, b)
    mrow = mod_ref[pl.ds(row, 1), :]
    h = (_rms(x_ref[...]) * g_ref[...]) * (1.0 + mrow[:, d:2 * d]) + mrow[:, 0:d]
    qkv = jnp.dot(h.astype(BF16), w_ref[...], preferred_element_type=F32)
    for hd in range(d // DA_VDIM):
        lo = hd * DA_VDIM
        q_ref[hd] = qkv[:, lo:lo + DA_VDIM].astype(BF16)
        k_ref[hd] = qkv[:, d + lo:d + lo + DA_VDIM].astype(BF16)
        v_ref[hd] = qkv[:, 2 * d + lo:2 * d + lo + DA_VDIM].astype(BF16)


def _qkv(xs, n_ctx, mod_shsc, g, w_qkv):
    nb, t_len, d = xs.shape
    r = ROW_TILE
    nh = d // DA_VDIM
    kern = functools.partial(_qkv_kernel, n_ctx_tiles=n_ctx // r, nb=nb, d=d)
    nct = n_ctx // r
    hspec = pl.BlockSpec((None, nh, r, DA_VDIM), lambda b, j: (b, 0, j, 0))
    qspec = pl.BlockSpec((None, nh, r, DA_VDIM), lambda b, j: (b, 0, jnp.maximum(j - nct, 0), 0))
    kv_shape = jax.ShapeDtypeStruct((nb, nh, t_len, DA_VDIM), BF16)
    return pl.pallas_call(
        kern,
        out_shape=[jax.ShapeDtypeStruct((nb, nh, t_len - n_ctx, DA_VDIM), BF16), kv_shape, kv_shape],
        grid=(nb, t_len // r),
        in_specs=[pl.BlockSpec((None, r, d), lambda b, j: (b, j, 0)),
                  pl.BlockSpec((MOD_ROWS, 2 * d), lambda b, j: (0, 0)),
                  pl.BlockSpec((1, d), lambda b, j: (0, 0)),
                  pl.BlockSpec((d, 3 * d), lambda b, j: (0, 0))],
        out_specs=[qspec, hspec, hspec],
        compiler_params=_cparams(("parallel", "arbitrary")),
        name="qkv",
    )(xs, mod_shsc, g.reshape(1, d), w_qkv.astype(BF16))


def _attn_kernel(q_ref, k_ref, v_ref, qg_ref, kg_ref, cosq_ref, sinq_ref, cosk_ref, sink_ref, lam_ref,
                 sg_ref, o_ref, kn_ref, *, n_ctx, lambda_init):
    qi = pl.program_id(2)
    n_heads, n_tot = k_ref.shape[0], k_ref.shape[1]
    tq = q_ref.shape[1]
    dh = DA_HEAD_DIM
    lane = lax.broadcasted_iota(jnp.int32, (1, DA_VDIM), 1)
    comp0 = lane < dh
    low_half = lax.rem(lane, dh) < dh // 2

    def head_norm(xv, gain):
        x2 = xv * xv
        s0 = jnp.sum(jnp.where(comp0, x2, 0.0), axis=-1, keepdims=True)
        s1 = jnp.sum(jnp.where(comp0, 0.0, x2), axis=-1, keepdims=True)
        ms = jnp.where(comp0, s0, s1) * (1.0 / dh)
        return xv * lax.rsqrt(ms + NORM_EPS) * gain

    def rope(xv, cos, sin_signed):
        partner = jnp.where(low_half, pltpu.roll(xv, DA_VDIM - dh // 2, axis=1),
                            pltpu.roll(xv, dh // 2, axis=1))
        return xv * cos + partner * sin_signed

    def head_norm_mxu(xv, gain):
        r_i = lax.broadcasted_iota(jnp.int32, (DA_VDIM, DA_VDIM), 0)
        c_i = lax.broadcasted_iota(jnp.int32, (DA_VDIM, DA_VDIM), 1)
        same = jnp.where((r_i < dh) == (c_i < dh), 1.0, 0.0).astype(BF16)
        hi, lo = _split_bf16(xv * xv)
        ms = (jnp.dot(hi, same, preferred_element_type=F32)
              + jnp.dot(lo, same, preferred_element_type=F32)) * (1.0 / dh)
        return xv * lax.rsqrt(ms + NORM_EPS) * gain

    @pl.when(qi == 0)
    def _():
        kg = kg_ref[...]
        for g in range(n_heads):
            kn_ref[g, 0:n_ctx, :] = head_norm_mxu(k_ref[g, 0:n_ctx, :].astype(F32), kg).astype(BF16)
            for c0 in range(n_ctx, n_tot, tq):
                kk = head_norm_mxu(k_ref[g, c0:c0 + tq, :].astype(F32), kg)
                kn_ref[g, c0:c0 + tq, :] = rope(kk, cosk_ref[c0 - n_ctx:c0 - n_ctx + tq, :],
                                                sink_ref[c0 - n_ctx:c0 - n_ctx + tq, :]).astype(BF16)

    lam_v = lam_ref[...]
    lam = (jnp.exp(jnp.sum(lam_v[0:1] * lam_v[1:2], axis=-1, keepdims=True))
           - jnp.exp(jnp.sum(lam_v[2:3] * lam_v[3:4], axis=-1, keepdims=True)) + lambda_init)
    dims = (((1,), (1,)), ((), ()))
    rb = ATTN_ROW_BLOCK
    units = [(g, r * rb) for g in range(n_heads) for r in range(tq // rb)]

    def scores(g, r0):
        qn = rope(head_norm(q_ref[g, r0:r0 + rb, :].astype(F32), qg_ref[...]),
                  cosq_ref[r0:r0 + rb, :], sinq_ref[r0:r0 + rb, :]) * (dh ** -0.5 * LOG2_E)
        qq = jnp.concatenate([jnp.where(comp0, qn, 0.0), jnp.where(comp0, 0.0, qn)], axis=0).astype(BF16)
        return lax.dot_general(qq, kn_ref[g], dims, preferred_element_type=F32)

    def softmax_mix(s):
        s0, s1 = s[0:rb], s[rb:2 * rb]
        e0 = jnp.exp2(s0 - jnp.max(s0, axis=-1, keepdims=True))
        e1 = jnp.exp2(s1 - jnp.max(s1, axis=-1, keepdims=True))
        l0 = jnp.sum(e0, axis=-1, keepdims=True)
        l1 = jnp.sum(e1, axis=-1, keepdims=True)
        return e0.astype(BF16) - (lam * l0 / l1).astype(BF16) * e1.astype(BF16), l0

    def values(g, r0, a, l0):
        o = jnp.dot(a, v_ref[g], preferred_element_type=F32) * (1.0 / l0)
        o_ref[r0:r0 + rb, g * DA_VDIM:(g + 1) * DA_VDIM] = (
            _rms(o) * sg_ref[...] * (1.0 - lambda_init)).astype(BF16)

    s_of, a_of = {}, {}
    for i in range(len(units) + 2):
        if i < len(units):
            s_of[i] = scores(*units[i])
        if 0 <= i - 1 < len(units):
            a_of[i - 1] = softmax_mix(s_of.pop(i - 1))
        if 0 <= i - 2 < len(units):
            values(*units[i - 2], *a_of.pop(i - 2))


def _rope_tables(n_tokens):
    rows = n_tokens // GRID_W
    row = jnp.repeat(jnp.arange(rows, dtype=F32), GRID_W)
    col = jnp.tile(jnp.arange(GRID_W, dtype=F32), rows)
    n_freq = DA_HEAD_DIM // 4
    inv_freq = jnp.exp(-math.log(ROPE_BASE) * jnp.arange(n_freq, dtype=F32) / n_freq)
    ang = jnp.concatenate([row[:, None] * inv_freq, col[:, None] * inv_freq], axis=-1)
    cos, sin = jnp.cos(ang), jnp.sin(ang)
    return jnp.tile(cos, (1, 4)), jnp.tile(jnp.concatenate([-sin, sin], axis=-1), (1, 2))


def _attention(q, k, v, n_ctx, q_gain, k_gain, lam_vecs, sub_gain, lambda_init):
    nb, nh, n_tot, dv = k.shape
    n_lat = n_tot - n_ctx
    tq = Q_TILE
    cos_t, sin_t = _rope_tables(n_lat)
    gain2 = lambda gv: jnp.tile(gv, 2).reshape(1, dv)
    lam_pad = jnp.zeros((SUBLANES, dv), F32).at[0:4, 0:DA_HEAD_DIM].set(lam_vecs)
    kern = functools.partial(_attn_kernel, n_ctx=n_ctx, lambda_init=lambda_init)
    const = lambda shape: pl.BlockSpec(shape, lambda b, h, i: (0, 0))
    hg = HEADS_PER_STEP
    return pl.pallas_call(
        kern,
        out_shape=jax.ShapeDtypeStruct((nb, n_lat, nh * dv), BF16),
        grid=(nb, nh // hg, n_lat // tq),
        in_specs=[pl.BlockSpec((None, hg, tq, dv), lambda b, h, i: (b, h, i, 0)),
                  pl.BlockSpec((None, hg, n_tot, dv), lambda b, h, i: (b, h, 0, 0)),
                  pl.BlockSpec((None, hg, n_tot, dv), lambda b, h, i: (b, h, 0, 0)),
                  const((1, dv)), const((1, dv)),
                  pl.BlockSpec((tq, dv), lambda b, h, i: (i, 0)),
                  pl.BlockSpec((tq, dv), lambda b, h, i: (i, 0)),
                  const((n_lat, dv)), const((n_lat, dv)),
                  const((SUBLANES, dv)), const((1, dv))],
        out_specs=pl.BlockSpec((None, tq, hg * dv), lambda b, h, i: (b, i, h)),
        scratch_shapes=[pltpu.VMEM((hg, n_tot, dv), BF16)],
        compiler_params=_cparams(("parallel", "parallel", "arbitrary")),
        name="attention",
    )(q, k, v, gain2(q_gain), gain2(k_gain), cos_t, sin_t, cos_t, sin_t, lam_pad, sub_gain.reshape(1, dv))


def _oproj_kernel(o_ref, w_ref, x_ref, mod_ref, out_ref):
    b = pl.program_id(0)
    y = jnp.dot(o_ref[...], w_ref[...], preferred_element_type=F32)
    out_ref[...] = x_ref[...] + mod_ref[pl.ds(b, 1), :] * y


def _oproj(o, w_o, xs, n_ctx, mod_gt):
    nb, n_lat, d = o.shape
    r = ROW_TILE
    off = n_ctx // r
    return pl.pallas_call(
        _oproj_kernel,
        out_shape=jax.ShapeDtypeStruct((nb, n_lat, d), F32),
        grid=(nb, n_lat // r),
        in_specs=[pl.BlockSpec((None, r, d), lambda b, j: (b, j, 0)),
                  pl.BlockSpec((d, d), lambda b, j: (0, 0)),
                  pl.BlockSpec((None, r, d), lambda b, j: (b, j + off, 0)),
                  pl.BlockSpec((MOD_ROWS, d), lambda b, j: (0, 0))],
        out_specs=pl.BlockSpec((None, r, d), lambda b, j: (b, j, 0)),
        compiler_params=_cparams(("parallel", "parallel")),
        name="oproj",
    )(o, w_o.astype(BF16), xs, mod_gt)


def _attn_layer(xs, n_ctx, mod, g_mix, w_qkv, w_o, q_gain, k_gain, lq1, lk1, lq2, lk2, sub_gain, lambda_init):
    d = xs.shape[-1]
    q, k, v = _qkv(xs, n_ctx, mod[:, 0:2 * d], g_mix, w_qkv)
    o = _attention(q, k, v, n_ctx, q_gain, k_gain, jnp.stack([lq1, lk1, lq2, lk2]), sub_gain, lambda_init)
    return _oproj(o, w_o, xs, n_ctx, mod[:, 2 * d:3 * d])


def kernel(x, c, ctx, c_ctx, w_ada, b_ada, g_mix, g_ffn, s5_a_re, s5_a_im, s5_log_dt, s5_b_re, s5_b_im, s5_c_re, s5_c_im, s5_d, s5_w_glu, s5_b_glu, da_w_qkv, da_w_o, da_q_gain, da_k_gain, da_lam_q1, da_lam_k1, da_lam_q2, da_lam_k2, da_sub_gain, moe_w_router, moe_b_router, moe_w_gu, moe_b_gu, moe_w_down, moe_b_down):
    nb, n_lat, d = x.shape
    n_ctx = ctx.shape[1]
    depth = w_ada.shape[0]
    assert depth == 2 and nb % SUBLANES == 0 and nb + 1 <= MOD_ROWS
    cc = jnp.concatenate([c, c_ctx[None], jnp.zeros((MOD_ROWS - nb - 1, d), F32)], axis=0)
    mod = _ada(cc, w_ada, b_ada)
    xs = _s5_layer(ctx, x, mod[0], g_mix[0], s5_a_re[0], s5_a_im[0], s5_log_dt[0], s5_b_re[0], s5_b_im[0],
                   s5_c_re[0], s5_c_im[0], s5_d[0], s5_w_glu[0], s5_b_glu[0])
    xs = _moe_layer(xs, n_ctx, mod[0][:, 3 * d:5 * d], mod[0][:, 5 * d:6 * d], g_ffn[0], 0,
                    moe_w_router, moe_b_router, moe_w_gu, moe_b_gu, moe_w_down, moe_b_down)
    lambda_init = 0.8 - 0.6 * math.exp(-0.3 * 1)
    x1 = _attn_layer(xs, n_ctx, mod[1], g_mix[1], da_w_qkv[0], da_w_o[0], da_q_gain[0], da_k_gain[0],
                     da_lam_q1[0], da_lam_k1[0], da_lam_q2[0], da_lam_k2[0], da_sub_gain[0], lambda_init)
    return _moe_layer(x1, 0, mod[1][:, 3 * d:5 * d], mod[1][:, 5 * d:6 * d], g_ffn[1], 1,
                      moe_w_router, moe_b_router, moe_w_gu, moe_b_gu, moe_w_down, moe_b_down)
```
